```python
import math
import jax, jax.numpy as jnp
from jax import lax
import numpy as np

D_MODEL = 2048
BATCH = 4
SEQ = 4096
DEPTH = 1

POOL_WIDTH = D_MODEL // 2
POOL_WINDOWS = (2, 4, 8, 16)
POOL_GROUPS = len(POOL_WINDOWS)
POOL_GROUP_DIM = POOL_WIDTH // POOL_GROUPS
SSM_WIDTH = D_MODEL // 2
SSM_GROUP_DIM = 16
SSM_GROUPS = SSM_WIDTH // SSM_GROUP_DIM
SSM_STATE = 64
DT_MIN = 1e-3
DT_MAX = 1e-1
LAMBDA_RE_MAX = -1e-4
IN_WIDTH = POOL_WIDTH + SSM_WIDTH + 2 * D_MODEL
N_EXPERTS = 32
TOP_K = 4
D_EXPERT = D_MODEL
SWIGLU_ALPHA = 1.702
SWIGLU_LIMIT = 7.0
MOE_BLOCK = 256
N_ADA = 6
RMS_EPS = 1e-6

kernel_name = "pool_s5_gated_moe_adaln_block"


def rms_norm(x, g):
    xf = x.astype(jnp.float32)
    y = xf * lax.rsqrt(jnp.mean(xf * xf, axis=-1, keepdims=True) + RMS_EPS)
    return y * g.astype(jnp.float32)


def modulate(h, shift, scale):
    return h * (1.0 + scale[:, None, :]) + shift[:, None, :]


def pool_mixer(u, pool_w, pool_scale):
    b, s, _ = u.shape
    uf = u.astype(jnp.float32).reshape(b, s, POOL_GROUPS, POOL_GROUP_DIM)
    cs = jnp.cumsum(uf, axis=1)
    pos = jnp.arange(1, s + 1, dtype=jnp.float32)
    outs = []
    for g, w in enumerate(POOL_WINDOWS):
        c_g = cs[:, :, g]
        lower = jnp.pad(c_g[:, : s - w], ((0, 0), (w, 0), (0, 0)))
        count = jnp.minimum(pos, float(w))[None, :, None]
        outs.append((c_g - lower) / count - uf[:, :, g])
    pooled = jnp.stack(outs, axis=2)
    mixed = jnp.einsum('bsgc,gcd->bsgd', pooled, pool_w.astype(jnp.float32))
    return mixed.reshape(b, s, POOL_WIDTH) * pool_scale.astype(jnp.float32)


def _complex_linear_combine(left, right):
    a1r, a1i, b1r, b1i = left
    a2r, a2i, b2r, b2i = right
    ar = a2r * a1r - a2i * a1i
    ai = a2r * a1i + a2i * a1r
    br = a2r * b1r - a2i * b1i + b2r
    bi = a2r * b1i + a2i * b1r + b2i
    return (ar, ai, br, bi)


def s5_mixer(u, lam_re, lam_im, log_dt, b_re, b_im, c_re, c_im, d_skip):
    b, s, _ = u.shape
    f32 = jnp.float32
    uf = u.astype(f32).reshape(b, s, SSM_GROUPS, SSM_GROUP_DIM)
    dt = jnp.exp(log_dt.astype(f32))[:, None]
    lre = jnp.minimum(lam_re.astype(f32), LAMBDA_RE_MAX)
    lim = lam_im.astype(f32)
    mag = jnp.exp(lre * dt)
    ang = lim * dt
    ab_re = mag * jnp.cos(ang)
    ab_im = mag * jnp.sin(ang)
    den = lre * lre + lim * lim
    nr = ab_re - 1.0
    ni = ab_im
    f_re = (nr * lre + ni * lim) / den
    f_im = (ni * lre - nr * lim) / den
    br_, bi_ = b_re.astype(f32), b_im.astype(f32)
    bb_re = f_re[..., None] * br_ - f_im[..., None] * bi_
    bb_im = f_re[..., None] * bi_ + f_im[..., None] * br_
    bu_re = jnp.einsum('bsgh,gph->bsgp', uf, bb_re)
    bu_im = jnp.einsum('bsgh,gph->bsgp', uf, bb_im)
    a_re = jnp.broadcast_to(ab_re[None, None], (1, s, SSM_GROUPS, SSM_STATE))
    a_im = jnp.broadcast_to(ab_im[None, None], (1, s, SSM_GROUPS, SSM_STATE))
    _, _, xr, xi = lax.associative_scan(_complex_linear_combine, (a_re, a_im, bu_re, bu_im), axis=1)
    y = (jnp.einsum('bsgp,ghp->bsgh', xr, c_re.astype(f32))
         - jnp.einsum('bsgp,ghp->bsgh', xi, c_im.astype(f32)))
    y = y + d_skip.astype(f32).reshape(SSM_GROUPS, SSM_GROUP_DIM) * uf
    return y.reshape(b, s, SSM_WIDTH)


def hybrid_mixer(h, w_in, pool_w, pool_scale, w_pool_out, lam_re, lam_im, log_dt,
                 b_re, b_im, c_re, c_im, d_skip, w_glu, b_glu, w_out):
    z = h @ w_in
    u_pool, u_ssm, g_pool, g_ssm = jnp.split(
        z, [POOL_WIDTH, POOL_WIDTH + SSM_WIDTH, POOL_WIDTH + SSM_WIDTH + D_MODEL], axis=-1)
    y_pool = pool_mixer(u_pool, pool_w, pool_scale) @ w_pool_out
    y_s = jax.nn.gelu(s5_mixer(u_ssm, lam_re, lam_im, log_dt, b_re, b_im, c_re, c_im, d_skip))
    glu = y_s @ w_glu + b_glu
    glu_a, glu_b = jnp.split(glu, 2, axis=-1)
    y_ssm = glu_a * jax.nn.sigmoid(glu_b)
    merged = jax.nn.sigmoid(g_pool) * y_pool + jax.nn.sigmoid(g_ssm) * y_ssm
    return merged @ w_out


def moe_ffn(h, w_router, b_router, w1, b1, w2, b2):
    b, s, d = h.shape
    t = b * s
    hf = h.reshape(t, d)
    logits = hf.astype(jnp.float32) @ w_router.astype(jnp.float32) + b_router.astype(jnp.float32)
    top_val, top_idx = lax.top_k(logits, TOP_K)
    weights = jax.nn.softmax(top_val, axis=-1)
    n_assign = t * TOP_K
    flat_e = top_idx.reshape(-1).astype(jnp.int32)
    flat_t = jnp.repeat(jnp.arange(t, dtype=jnp.int32), TOP_K)
    flat_w = weights.reshape(-1)
    order = jnp.argsort(flat_e)
    se = flat_e[order]
    counts = jnp.zeros((N_EXPERTS,), jnp.int32).at[flat_e].add(1)
    starts = jnp.cumsum(counts) - counts
    pcounts = (counts + MOE_BLOCK - 1) // MOE_BLOCK * MOE_BLOCK
    pends = jnp.cumsum(pcounts)
    pstarts = pends - pcounts
    rank = jnp.arange(n_assign, dtype=jnp.int32) - starts[se]
    dest = pstarts[se] + rank
    n_blocks = -(-(n_assign + N_EXPERTS * (MOE_BLOCK - 1)) // MOE_BLOCK)
    n_pad = n_blocks * MOE_BLOCK
    buf_tok = jnp.zeros((n_pad,), jnp.int32).at[dest].set(flat_t[order])
    buf_w = jnp.zeros((n_pad,), jnp.float32).at[dest].set(flat_w[order])
    block_starts = jnp.arange(n_blocks, dtype=jnp.int32) * MOE_BLOCK
    block_e = jnp.minimum(jnp.searchsorted(pends, block_starts, side='right'), N_EXPERTS - 1)

    def run_block(args):
        tok, wgt, e = args
        xb = hf[tok]
        gu = xb @ w1[e] + b1[e]
        x_glu = jnp.minimum(gu[:, :D_EXPERT], SWIGLU_LIMIT)
        x_lin = jnp.clip(gu[:, D_EXPERT:], -SWIGLU_LIMIT, SWIGLU_LIMIT)
        act = x_glu * jax.nn.sigmoid(SWIGLU_ALPHA * x_glu) * (x_lin + 1.0)
        y = act @ w2[e] + b2[e]
        return y * wgt[:, None].astype(y.dtype)

    ys = lax.map(run_block, (buf_tok.reshape(n_blocks, MOE_BLOCK),
                             buf_w.reshape(n_blocks, MOE_BLOCK), block_e))
    out = jnp.zeros((t, d), ys.dtype).at[buf_tok].add(ys.reshape(n_pad, d))
    return out.reshape(b, s, d)


def setup_inputs(seed: int = 0) -> dict:
    key = jax.random.key(seed)
    ks = jax.random.split(key, 32)
    f32 = jnp.float32

    def nrm(k, shape, scale):
        return jax.random.normal(k, shape, f32) * scale

    L, D, G, P, H = DEPTH, D_MODEL, SSM_GROUPS, SSM_STATE, SSM_GROUP_DIM
    E, F = N_EXPERTS, D_EXPERT
    lam_im = (math.pi * jnp.arange(P, dtype=f32))[None, None, :] + nrm(ks[10], (L, G, P), 0.01)
    log_dt = jax.random.uniform(ks[11], (L, G), f32, math.log(DT_MIN), math.log(DT_MAX))
    return {
        "x": nrm(ks[0], (BATCH, SEQ, D), 1.0),
        "c": nrm(ks[1], (BATCH, D), 1.0),
        "ada_w": nrm(ks[2], (L, D, N_ADA * D), 0.3 * D ** -0.5),
        "ada_b": nrm(ks[3], (L, N_ADA * D), 0.02),
        "norm1_g": 1.0 + nrm(ks[4], (L, D), 0.02),
        "w_in": nrm(ks[5], (L, D, IN_WIDTH), D ** -0.5),
        "pool_w": nrm(ks[6], (L, POOL_GROUPS, POOL_GROUP_DIM, POOL_GROUP_DIM), POOL_GROUP_DIM ** -0.5),
        "pool_scale": 1.0 + nrm(ks[7], (L, POOL_WIDTH), 0.1),
        "w_pool_out": nrm(ks[8], (L, POOL_WIDTH, D), POOL_WIDTH ** -0.5),
        "ssm_lam_re": -0.5 + nrm(ks[9], (L, G, P), 0.01),
        "ssm_lam_im": lam_im,
        "ssm_log_dt": log_dt,
        "ssm_b_re": nrm(ks[12], (L, G, P, H), (2.0 * H) ** -0.5),
        "ssm_b_im": nrm(ks[13], (L, G, P, H), (2.0 * H) ** -0.5),
        "ssm_c_re": nrm(ks[14], (L, G, H, P), (2.0 * P) ** -0.5),
        "ssm_c_im": nrm(ks[15], (L, G, H, P), (2.0 * P) ** -0.5),
        "ssm_d": nrm(ks[16], (L, SSM_WIDTH), 1.0),
        "w_glu": nrm(ks[17], (L, SSM_WIDTH, 2 * D), SSM_WIDTH ** -0.5),
        "b_glu": nrm(ks[18], (L, 2 * D), 0.01),
        "w_out": nrm(ks[19], (L, D, D), D ** -0.5),
        "norm2_g": 1.0 + nrm(ks[20], (L, D), 0.02),
        "w_router": nrm(ks[21], (L, D, E), D ** -0.5),
        "b_router": nrm(ks[22], (L, E), 0.01),
        "w1": nrm(ks[23], (L, E, D, 2 * F), D ** -0.5),
        "b1": nrm(ks[24], (L, E, 2 * F), 0.01),
        "w2": nrm(ks[25], (L, E, F, D), F ** -0.5),
        "b2": nrm(ks[26], (L, E, D), 0.01),
        "final_ada_w": nrm(ks[27], (D, 2 * D), 0.3 * D ** -0.5),
        "final_ada_b": nrm(ks[28], (2 * D,), 0.02),
        "final_norm_g": 1.0 + nrm(ks[29], (D,), 0.02),
    }


def reference(x, c, ada_w, ada_b, norm1_g, w_in, pool_w, pool_scale, w_pool_out,
              ssm_lam_re, ssm_lam_im, ssm_log_dt, ssm_b_re, ssm_b_im, ssm_c_re, ssm_c_im, ssm_d,
              w_glu, b_glu, w_out, norm2_g, w_router, b_router, w1, b1, w2, b2,
              final_ada_w, final_ada_b, final_norm_g):
    c_act = jax.nn.silu(c.astype(jnp.float32))
    for l in range(DEPTH):
        mod = c_act @ ada_w[l] + ada_b[l]
        sh_m, sc_m, gt_m, sh_f, sc_f, gt_f = jnp.split(mod, N_ADA, axis=-1)
        h = modulate(rms_norm(x, norm1_g[l]), sh_m, sc_m)
        mix = hybrid_mixer(h, w_in[l], pool_w[l], pool_scale[l], w_pool_out[l],
                           ssm_lam_re[l], ssm_lam_im[l], ssm_log_dt[l], ssm_b_re[l], ssm_b_im[l],
                           ssm_c_re[l], ssm_c_im[l], ssm_d[l], w_glu[l], b_glu[l], w_out[l])
        x = x + (gt_m[:, None, :] * mix).astype(x.dtype)
        h2 = modulate(rms_norm(x, norm2_g[l]), sh_f, sc_f)
        ffn = moe_ffn(h2, w_router[l], b_router[l], w1[l], b1[l], w2[l], b2[l])
        x = x + (gt_f[:, None, :] * ffn).astype(x.dtype)
    fmod = c_act @ final_ada_w + final_ada_b
    sh_o, sc_o = jnp.split(fmod, 2, axis=-1)
    y = modulate(rms_norm(x, final_norm_g), sh_o, sc_o)
    return y.astype(x.dtype)
```

```python
import functools
import math

import jax
import jax.numpy as jnp
from jax import lax
from jax.experimental import pallas as pl
from jax.experimental.pallas import tpu as pltpu

F32 = jnp.float32
BF16 = jnp.bfloat16
I32 = jnp.int32
U32 = jnp.uint32

RMS_EPS = 1e-6
POOL_WINDOWS = (2, 4, 8, 16)
SSM_GROUP_DIM = 16
SSM_STATE = 64
LAMBDA_RE_MAX = -1e-4
TOP_K = 4
SWIGLU_ALPHA = 1.702
SWIGLU_LIMIT = 7.0

LANES = 128
V7X_VMEM_LIMIT = 56 * 1024 * 1024

CHUNK = 16
OCT = LANES // SSM_GROUP_DIM
SUB = 256
SUPER = 4 * SUB


def _cparams(sem, vmem=V7X_VMEM_LIMIT):
    return pltpu.CompilerParams(dimension_semantics=sem, vmem_limit_bytes=vmem)


def _const_spec(shape):
    nd = len(shape)
    return pl.BlockSpec(shape, lambda *_: (0,) * nd, pipeline_mode=pl.Buffered(1))


def _rms_mod(x, g, sh, sc):
    y = x * lax.rsqrt(jnp.mean(x * x, axis=-1, keepdims=True) + RMS_EPS) * g
    return y * (1.0 + sc) + sh


def _to_row_tiles(h, scr):
    m, d = h.shape
    nt = d // LANES
    for j in range(nt):
        scr[pl.ds(j, m, stride=nt), :] = h[:, j * LANES:(j + 1) * LANES]
    return scr[...].astype(BF16).reshape(m, nt, LANES)


def _from_row_tiles(x, scr):
    m, nt, _ = x.shape
    scr[...] = x.reshape(m * nt, LANES).astype(F32)
    return jnp.concatenate([scr[pl.ds(j, m, stride=nt), :] for j in range(nt)], axis=1)


def _ada_kernel(c_ref, w_ref, b_ref, o_ref):
    c = c_ref[...]
    ca = c * jax.nn.sigmoid(c)
    o_ref[...] = jnp.dot(ca.astype(BF16), w_ref[...].astype(BF16),
                         preferred_element_type=F32) + b_ref[...]


def _ada(c8, w, b, tn=1024):
    d, n = w.shape
    tn = math.gcd(tn, n)
    return pl.pallas_call(
        _ada_kernel,
        out_shape=jax.ShapeDtypeStruct((c8.shape[0], n), F32),
        grid=(n // tn,),
        in_specs=[pl.BlockSpec(c8.shape, lambda j: (0, 0)),
                  pl.BlockSpec((d, tn), lambda j: (0, j)),
                  pl.BlockSpec((1, tn), lambda j: (0, j))],
        out_specs=pl.BlockSpec((c8.shape[0], tn), lambda j: (0, j)),
        compiler_params=_cparams(("arbitrary",)),
        name="ada",
    )(c8, w, b.reshape(1, n))


def _inproj_kernel(x_ref, g_ref, sh_ref, sc_ref, w_ref, z_ref, *, ncol):
    h = _rms_mod(x_ref[0], g_ref[...], sh_ref[0], sc_ref[0]).astype(BF16)
    n = w_ref.shape[1]
    for c in range(n // ncol):
        z_ref[0, :, c * ncol:(c + 1) * ncol] = jnp.dot(
            h, w_ref[:, c * ncol:(c + 1) * ncol], preferred_element_type=F32).astype(BF16)


def _inproj(x, g, sh, sc, w, tm=512, ncol=1024):
    b, s, d = x.shape
    n = w.shape[1]
    ncol = math.gcd(ncol, n)
    return pl.pallas_call(
        functools.partial(_inproj_kernel, ncol=ncol),
        out_shape=jax.ShapeDtypeStruct((b, s, n), BF16),
        grid=(b, s // tm),
        in_specs=[pl.BlockSpec((1, tm, d), lambda bi, i: (bi, i, 0)),
                  _const_spec((1, d)),
                  pl.BlockSpec((1, 1, d), lambda bi, i: (bi, 0, 0)),
                  pl.BlockSpec((1, 1, d), lambda bi, i: (bi, 0, 0)),
                  _const_spec((d, n))],
        out_specs=pl.BlockSpec((1, tm, n), lambda bi, i: (bi, i, 0)),
        compiler_params=_cparams(("arbitrary", "arbitrary")),
        name="inproj",
    )(x, g.reshape(1, d), sh.reshape(b, 1, d), sc.reshape(b, 1, d), w)


def _s5_weights(lam_re, lam_im, log_dt, b_re, b_im, c_re, c_im):
    hp = lax.Precision.HIGHEST
    g_all, p = lam_re.shape
    h = b_re.shape[-1]
    no = g_all // OCT
    dt = jnp.exp(log_dt.astype(F32))[:, None]
    lre = jnp.minimum(lam_re.astype(F32), LAMBDA_RE_MAX)
    lim = lam_im.astype(F32)
    mag = jnp.exp(lre * dt)
    ang = lim * dt
    ab_re = mag * jnp.cos(ang)
    ab_im = mag * jnp.sin(ang)
    den = lre * lre + lim * lim
    nr = ab_re - 1.0
    ni = ab_im
    f_re = (nr * lre + ni * lim) / den
    f_im = (ni * lre - nr * lim) / den
    br_, bi_ = b_re.astype(F32), b_im.astype(F32)
    bb_re = f_re[..., None] * br_ - f_im[..., None] * bi_
    bb_im = f_re[..., None] * bi_ + f_im[..., None] * br_
    cr, ci = c_re.astype(F32), c_im.astype(F32)
    j = jnp.arange(CHUNK + 1, dtype=F32)[:, None, None]
    pw_mag = jnp.exp(j * (lre * dt)[None])
    pw_re = pw_mag * jnp.cos(j * ang[None])
    pw_im = pw_mag * jnp.sin(j * ang[None])
    cp_re = cr[None] * pw_re[:, :, None, :] - ci[None] * pw_im[:, :, None, :]
    cp_im = cr[None] * pw_im[:, :, None, :] + ci[None] * pw_re[:, :, None, :]
    klag = (jnp.einsum('jghp,gpi->gjhi', cp_re[:CHUNK], bb_re, precision=hp)
            - jnp.einsum('jghp,gpi->gjhi', cp_im[:CHUNK], bb_im, precision=hp))
    eye = jnp.eye(OCT, dtype=F32)
    tk = jnp.arange(CHUNK)
    lag = tk[None, :] - tk[:, None]
    ktoe = jnp.where((lag >= 0)[None, :, :, None, None],
                     klag[:, jnp.clip(lag, 0, CHUNK - 1)], 0.0)
    ktoe = ktoe.reshape(no, OCT, CHUNK, CHUNK, h, h)
    t_op = jnp.einsum('ogkthi,gq->okgitqh', ktoe, eye).reshape(no, CHUNK * LANES, CHUNK * LANES)
    pwr = pw_re[CHUNK - 1 - tk]
    pwi = pw_im[CHUNK - 1 - tk]
    s_re = pwr[..., None] * bb_re[None] - pwi[..., None] * bb_im[None]
    s_im = pwr[..., None] * bb_im[None] + pwi[..., None] * bb_re[None]
    s_ri = jnp.stack([s_re, s_im], axis=0).reshape(2, CHUNK, no, OCT, p, h)
    b_op = jnp.einsum('rkogpi,gq->okgirqp', s_ri, eye).reshape(no, CHUNK * LANES, 2 * OCT * p)
    c_ri = jnp.stack([cp_re[1:], -cp_im[1:]], axis=0).reshape(2, CHUNK, no, OCT, h, p)
    c_op = jnp.einsum('rtoghp,gq->orgptqh', c_ri, eye).reshape(no, 2 * OCT * p, CHUNK * LANES)
    a_re = pw_re[CHUNK].reshape(no, 1, OCT * p)
    a_im = pw_im[CHUNK].reshape(no, 1, OCT * p)
    return t_op.astype(BF16), b_op.astype(BF16), c_op.astype(BF16), a_re, a_im


def _gelu_tanh(x):
    return 0.5 * x * (1.0 + jnp.tanh(math.sqrt(2.0 / math.pi) * (x + 0.044715 * x * x * x)))


def _s5_kernel(u_ref, t_ref, b_ref, c_ref, ar_ref, ai_ref, d_ref, y_ref,
               uscr, sscr, pscr, yscr, state):
    nb, tm, _ = u_ref.shape
    nc = tm // CHUNK
    half = ar_ref.shape[-1]

    @pl.when(pl.program_id(1) == 0)
    def _():
        state[...] = jnp.zeros_like(state)

    for bi in range(nb):
        uscr[bi * tm:(bi + 1) * tm, :] = u_ref[bi].astype(F32)
    xk = jnp.concatenate(
        [uscr[pl.ds(k, nb * nc, stride=CHUNK), :] for k in range(CHUNK)], axis=1).astype(BF16)
    rr = nb * nc
    nq = 2 * half // LANES
    hq = nq // 2
    sc_all = jnp.dot(xk, b_ref[0], preferred_element_type=F32)
    for q in range(nq):
        sscr[q * rr:(q + 1) * rr, :] = sc_all[:, q * LANES:(q + 1) * LANES]

    ar = [ar_ref[0, :, q * LANES:(q + 1) * LANES] for q in range(hq)]
    ai = [ai_ref[0, :, q * LANES:(q + 1) * LANES] for q in range(hq)]
    st = [state[:, q * LANES:(q + 1) * LANES] for q in range(nq)]
    for c in range(nc):
        sc = []
        for q in range(nq):
            pscr[pl.ds(q * rr + c, nb, stride=nc), :] = st[q]
            sc.append(sscr[pl.ds(q * rr + c, nb, stride=nc), :])
        st = ([ar[q] * st[q] - ai[q] * st[hq + q] + sc[q] for q in range(hq)]
              + [ar[q] * st[hq + q] + ai[q] * st[q] + sc[hq + q] for q in range(hq)])
    for q in range(nq):
        state[:, q * LANES:(q + 1) * LANES] = st[q]

    xprev = jnp.concatenate([pscr[q * rr:(q + 1) * rr, :] for q in range(nq)], axis=1)
    y = (jnp.dot(xk, t_ref[0], preferred_element_type=F32)
         + jnp.dot(xprev.astype(BF16), c_ref[0], preferred_element_type=F32))
    for t in range(CHUNK):
        yscr[pl.ds(t, nb * nc, stride=CHUNK), :] = y[:, t * LANES:(t + 1) * LANES]
    out = _gelu_tanh(yscr[...] + d_ref[...] * uscr[...])
    for bi in range(nb):
        y_ref[bi] = out[bi * tm:(bi + 1) * tm, :].astype(BF16)


def _s5(z, col0, t_op, b_op, c_op, a_re, a_im, d_skip, tm=1024):
    nb, s, _ = z.shape
    no = t_op.shape[0]
    w = no * LANES
    ns = b_op.shape[-1]
    tm = min(tm, s)
    rows = nb * tm
    cb0 = col0 // LANES
    return pl.pallas_call(
        _s5_kernel,
        out_shape=jax.ShapeDtypeStruct((nb, s, w), BF16),
        grid=(no, s // tm),
        in_specs=[pl.BlockSpec((nb, tm, LANES), lambda o, i: (0, i, cb0 + o)),
                  pl.BlockSpec((1,) + t_op.shape[1:], lambda o, i: (o, 0, 0)),
                  pl.BlockSpec((1,) + b_op.shape[1:], lambda o, i: (o, 0, 0)),
                  pl.BlockSpec((1,) + c_op.shape[1:], lambda o, i: (o, 0, 0)),
                  pl.BlockSpec((1, 1, ns // 2), lambda o, i: (o, 0, 0)),
                  pl.BlockSpec((1, 1, ns // 2), lambda o, i: (o, 0, 0)),
                  pl.BlockSpec((1, LANES), lambda o, i: (0, o))],
        out_specs=pl.BlockSpec((nb, tm, LANES), lambda o, i: (0, i, o)),
        scratch_shapes=[pltpu.VMEM((rows, LANES), F32),
                        pltpu.VMEM((rows // CHUNK * (ns // LANES), LANES), F32),
                        pltpu.VMEM((rows // CHUNK * (ns // LANES), LANES), F32),
                        pltpu.VMEM((rows, LANES), F32),
                        pltpu.VMEM((nb, ns), F32)],
        compiler_params=_cparams(("arbitrary", "arbitrary")),
        name="s5",
    )(z, t_op, b_op, c_op, a_re, a_im, d_skip.reshape(1, w))


def _mixout_kernel(up_ref, halo_ref, ys_ref, gp_ref, gs_ref, x_ref, gt_ref,
                   pw_ref, ps_ref, wpo_ref, wglu_ref, bglu_ref, wout_ref, o_ref, escr):
    tm = up_ref.shape[1]
    i = pl.program_id(1)
    gw = pw_ref.shape[1]
    hal = halo_ref.shape[1]
    d = o_ref.shape[2]
    escr[0:hal, :] = jnp.where(i > 0, halo_ref[0].astype(F32), 0.0)
    escr[hal:hal + tm, :] = up_ref[0].astype(F32)
    pos = (i * tm + lax.broadcasted_iota(I32, (tm, gw), 0) + 1).astype(F32)
    mixed = []
    for g, win in enumerate(POOL_WINDOWS):
        cols = slice(g * gw, (g + 1) * gw)
        cur = escr[hal:hal + tm, cols]
        acc = cur
        for j in range(1, win):
            acc = acc + escr[hal - j:hal - j + tm, cols]
        pooled = acc / jnp.minimum(pos, float(win)) - cur
        mixed.append(jnp.dot(pooled.astype(BF16), pw_ref[g], preferred_element_type=F32))
    mixed = jnp.concatenate(mixed, axis=1) * ps_ref[...]
    y_pool = jnp.dot(mixed.astype(BF16), wpo_ref[...], preferred_element_type=F32)
    glu = jnp.dot(ys_ref[0], wglu_ref[...], preferred_element_type=F32) + bglu_ref[...]
    y_ssm = glu[:, :d] * jax.nn.sigmoid(glu[:, d:])
    merged = (jax.nn.sigmoid(gp_ref[0].astype(F32)) * y_pool
              + jax.nn.sigmoid(gs_ref[0].astype(F32)) * y_ssm)
    mix = jnp.dot(merged.astype(BF16), wout_ref[...], preferred_element_type=F32)
    o_ref[0] = x_ref[0] + gt_ref[0] * mix


def _mixout(z, ys, x, gt, pool_w, pool_scale, w_pool_out, w_glu, b_glu, w_out, tm=256):
    b, s, d = x.shape
    pwid = w_pool_out.shape[0]
    hal = max(POOL_WINDOWS)
    tpb = tm // hal
    gcol = (pwid + ys.shape[2]) // d
    return pl.pallas_call(
        _mixout_kernel,
        out_shape=jax.ShapeDtypeStruct((b, s, d), F32),
        grid=(b, s // tm),
        in_specs=[pl.BlockSpec((1, tm, pwid), lambda bi, i: (bi, i, 0)),
                  pl.BlockSpec((1, hal, pwid), lambda bi, i: (bi, jnp.maximum(i * tpb - 1, 0), 0)),
                  pl.BlockSpec((1, tm, ys.shape[2]), lambda bi, i: (bi, i, 0)),
                  pl.BlockSpec((1, tm, d), lambda bi, i: (bi, i, gcol)),
                  pl.BlockSpec((1, tm, d), lambda bi, i: (bi, i, gcol + 1)),
                  pl.BlockSpec((1, tm, d), lambda bi, i: (bi, i, 0)),
                  pl.BlockSpec((1, 1, d), lambda bi, i: (bi, 0, 0)),
                  _const_spec(pool_w.shape),
                  _const_spec((1, pwid)),
                  _const_spec(w_pool_out.shape),
                  _const_spec(w_glu.shape),
                  _const_spec((1, w_glu.shape[1])),
                  _const_spec(w_out.shape)],
        out_specs=pl.BlockSpec((1, tm, d), lambda bi, i: (bi, i, 0)),
        scratch_shapes=[pltpu.VMEM((hal + tm, pwid), F32)],
        compiler_params=_cparams(("arbitrary", "arbitrary")),
        name="mixout",
    )(z, z, ys, z, z, x, gt.reshape(b, 1, d), pool_w, pool_scale.reshape(1, pwid),
      w_pool_out, w_glu, b_glu.reshape(1, -1), w_out)


def _router_kernel(x_ref, g_ref, sh_ref, sc_ref, wr_ref, br_ref,
                   hp_ref, idx_ref, wt_ref, rank_ref, cnt_ref, carry, rscr, *, n_exp):
    tm = x_ref.shape[1]
    first = (pl.program_id(0) == 0) & (pl.program_id(1) == 0)

    @pl.when(first)
    def _():
        carry[...] = jnp.zeros_like(carry)

    h2 = _rms_mod(x_ref[0], g_ref[...], sh_ref[0], sc_ref[0])
    hp_ref[...] = _to_row_tiles(h2, rscr)
    logits = jnp.dot(h2, wr_ref[...], preferred_element_type=F32,
                     precision=lax.Precision.HIGHEST) + br_ref[...]
    lt = logits.T[:n_exp, :]
    eid = lax.broadcasted_iota(I32, lt.shape, 0).astype(F32)
    vals, idxs, hots = [], [], []
    v = lt
    for _ in range(TOP_K):
        m = jnp.max(v, axis=0, keepdims=True)
        sel = jnp.min(jnp.where(v == m, eid, float(n_exp)), axis=0, keepdims=True)
        hot = eid == sel
        vals.append(m)
        idxs.append(sel)
        hots.append(hot)
        v = jnp.where(hot, -jnp.inf, v)
    ex = [jnp.exp(m - vals[0]) for m in vals]
    tot = ex[0] + ex[1] + ex[2] + ex[3]
    msum = sum(h.astype(F32) for h in hots)
    tri = (lax.broadcasted_iota(I32, (tm, tm), 0) < lax.broadcasted_iota(I32, (tm, tm), 1))
    before = jnp.dot(msum.astype(BF16), tri.astype(BF16), preferred_element_type=F32) + carry[:, 0:1]
    for k in range(TOP_K):
        idx_ref[k:k + 1, :] = idxs[k].astype(I32)
        wt_ref[k:k + 1, :] = ex[k] / tot
        rank_ref[k:k + 1, :] = jnp.sum(jnp.where(hots[k], before, 0.0), axis=0,
                                       keepdims=True).astype(I32)
    carry[...] = carry[...] + jnp.sum(msum, axis=1, keepdims=True)
    cnt_ref[...] = carry[...].astype(I32)


def _router(x1, g, sh, sc, w_router, b_router, tm=512):
    b, s, d = x1.shape
    t = b * s
    e = w_router.shape[1]
    nt = d // LANES
    wr = jnp.zeros((d, LANES), F32).at[:, :e].set(w_router.astype(F32))
    br = jnp.full((1, LANES), -1e30, F32).at[0, :e].set(b_router.astype(F32))
    spb = s // tm
    return pl.pallas_call(
        functools.partial(_router_kernel, n_exp=e),
        out_shape=[jax.ShapeDtypeStruct((t, nt, LANES), BF16),
                   jax.ShapeDtypeStruct((TOP_K, t), I32),
                   jax.ShapeDtypeStruct((TOP_K, t), F32),
                   jax.ShapeDtypeStruct((TOP_K, t), I32),
                   jax.ShapeDtypeStruct((e, LANES), I32)],
        grid=(b, spb),
        in_specs=[pl.BlockSpec((1, tm, d), lambda bi, i: (bi, i, 0)),
                  _const_spec((1, d)),
                  pl.BlockSpec((1, 1, d), lambda bi, i: (bi, 0, 0)),
                  pl.BlockSpec((1, 1, d), lambda bi, i: (bi, 0, 0)),
                  _const_spec((d, LANES)),
                  _const_spec((1, LANES))],
        out_specs=[pl.BlockSpec((tm, nt, LANES), lambda bi, i: (bi * spb + i, 0, 0)),
                   pl.BlockSpec((TOP_K, tm), lambda bi, i: (0, bi * spb + i)),
                   pl.BlockSpec((TOP_K, tm), lambda bi, i: (0, bi * spb + i)),
                   pl.BlockSpec((TOP_K, tm), lambda bi, i: (0, bi * spb + i)),
                   pl.BlockSpec((e, LANES), lambda bi, i: (0, 0))],
        scratch_shapes=[pltpu.VMEM((e, LANES), F32),
                        pltpu.VMEM((tm * nt, LANES), F32)],
        compiler_params=_cparams(("arbitrary", "arbitrary")),
        name="router",
    )(x1, g.reshape(1, d), sh.reshape(b, 1, d), sc.reshape(b, 1, d), wr, br)


def _dispatch_kernel(dest_ref, zflag_ref, hp_ref, xs_ref, zbuf, sem, zsem, *, n_tok):
    tm = hp_ref.shape[0]
    i = pl.program_id(0)

    @pl.when(i == 0)
    def _():
        zbuf[...] = jnp.zeros_like(zbuf)

        def zcopy(q):
            r0 = pl.multiple_of(q * SUB, SUB)
            return pltpu.make_async_copy(zbuf, xs_ref.at[pl.ds(r0, SUB)], zsem)

        def zstart(q, carry):
            pl.when(zflag_ref[q] != 0)(lambda: zcopy(q).start())
            return carry

        def zwait(q, carry):
            pl.when(zflag_ref[q] != 0)(lambda: zcopy(q).wait())
            return carry

        lax.fori_loop(0, zflag_ref.shape[0], zstart, 0)
        lax.fori_loop(0, zflag_ref.shape[0], zwait, 0)

    def issue(t, carry):
        for k in range(TOP_K):
            dst = dest_ref[k * n_tok + i * tm + t]
            pltpu.make_async_copy(hp_ref.at[pl.ds(t, 1)], xs_ref.at[pl.ds(dst, 1)], sem).start()
        return carry

    lax.fori_loop(0, tm, issue, 0)
    for k in range(TOP_K):
        pltpu.make_async_copy(hp_ref, xs_ref.at[pl.ds(0, tm)], sem).wait()


def _dispatch(dest_flat, zflag, hp, n_rows, tm=256):
    t, nt, _ = hp.shape
    return pl.pallas_call(
        functools.partial(_dispatch_kernel, n_tok=t),
        out_shape=jax.ShapeDtypeStruct((n_rows, nt, LANES), BF16),
        grid_spec=pltpu.PrefetchScalarGridSpec(
            num_scalar_prefetch=2,
            grid=(t // tm,),
            in_specs=[pl.BlockSpec((tm, nt, LANES), lambda i, *_: (i, 0, 0))],
            out_specs=pl.BlockSpec(memory_space=pl.ANY),
            scratch_shapes=[pltpu.VMEM((SUB, nt, LANES), BF16),
                            pltpu.SemaphoreType.DMA,
                            pltpu.SemaphoreType.DMA]),
        compiler_params=_cparams(("arbitrary",)),
        name="dispatch",
    )(dest_flat, zflag, hp)


def _experts_kernel(sbe_ref, nsub_ref, nused_ref, x_ref, w1g_ref, w1l_ref, b1g_ref, b1l_ref,
                    w2_ref, b2_ref, o_ref, xb, acc, wg, wl, w2b, rscr):
    s = pl.program_id(0)
    j = pl.program_id(1)
    nj = pl.num_programs(1)
    n = nsub_ref[s]

    @pl.when(n > 0)
    def _():
        wg[...] = w1g_ref[0].astype(BF16)
        wl[...] = w1l_ref[0].astype(BF16)
        w2b[...] = w2_ref[0].astype(BF16)

    for r in range(SUPER // SUB):
        rows = slice(r * SUB, (r + 1) * SUB)

        @pl.when(r < n)
        def _():
            @pl.when(j == 0)
            def _():
                xb[rows, :] = _from_row_tiles(x_ref[rows], rscr).astype(BF16)

            xr = xb[rows, :]
            g = jnp.dot(xr, wg[...], preferred_element_type=F32) + b1g_ref[0]
            l = jnp.dot(xr, wl[...], preferred_element_type=F32) + b1l_ref[0]
            xg = jnp.minimum(g, SWIGLU_LIMIT)
            xl = jnp.clip(l, -SWIGLU_LIMIT, SWIGLU_LIMIT)
            act = xg * jax.nn.sigmoid(SWIGLU_ALPHA * xg) * (xl + 1.0)
            contrib = jnp.dot(act.astype(BF16), w2b[...], preferred_element_type=F32)

            @pl.when(j == 0)
            def _():
                acc[rows, :] = contrib + b2_ref[0]

            @pl.when(j > 0)
            def _():
                acc[rows, :] = acc[rows, :] + contrib

            @pl.when(j == nj - 1)
            def _():
                o_ref[rows] = _to_row_tiles(acc[rows, :], rscr)

        @pl.when((r >= n) & (j == nj - 1))
        def _():
            o_ref[rows] = jnp.zeros((SUB,) + o_ref.shape[1:], BF16)


def _experts(sbe, nsub, nused, xs, w1, b1, w2, b2, n_super, tf=256):
    n_exp, d, f2 = w1.shape
    f = f2 // 2
    tf = min(tf, f)
    nj = f // tf
    nt = xs.shape[1]

    def x_map(s, j, sbe, nsub, nused):
        return (jnp.minimum(s, nused[0] - 1), 0, 0)

    def jj(s, j, nused):
        return jnp.where(s < nused[0], j, nj - 1)

    return pl.pallas_call(
        _experts_kernel,
        out_shape=jax.ShapeDtypeStruct((n_super * SUPER, nt, LANES), BF16),
        grid_spec=pltpu.PrefetchScalarGridSpec(
            num_scalar_prefetch=3,
            grid=(n_super, nj),
            in_specs=[pl.BlockSpec((SUPER, nt, LANES), x_map),
                      pl.BlockSpec((1, d, tf), lambda s, j, sbe, nsub, nu: (sbe[s], 0, jj(s, j, nu))),
                      pl.BlockSpec((1, d, tf), lambda s, j, sbe, nsub, nu: (sbe[s], 0, nj + jj(s, j, nu))),
                      pl.BlockSpec((1, 1, tf), lambda s, j, sbe, nsub, nu: (sbe[s], 0, jj(s, j, nu))),
                      pl.BlockSpec((1, 1, tf), lambda s, j, sbe, nsub, nu: (sbe[s], 0, nj + jj(s, j, nu))),
                      pl.BlockSpec((1, tf, d), lambda s, j, sbe, nsub, nu: (sbe[s], jj(s, j, nu), 0)),
                      pl.BlockSpec((1, 1, d), lambda s, j, sbe, nsub, nu: (sbe[s], 0, 0))],
            out_specs=pl.BlockSpec((SUPER, nt, LANES), lambda s, j, *_: (s, 0, 0)),
            scratch_shapes=[pltpu.VMEM((SUPER, d), BF16),
                            pltpu.VMEM((SUPER, d), F32),
                            pltpu.VMEM((d, tf), BF16),
                            pltpu.VMEM((d, tf), BF16),
                            pltpu.VMEM((tf, d), BF16),
                            pltpu.VMEM((SUB * nt, LANES), F32)]),
        compiler_params=_cparams(("arbitrary", "arbitrary")),
        name="experts",
    )(sbe, nsub, nused, xs, w1, w1, b1.reshape(n_exp, 1, f2), b1.reshape(n_exp, 1, f2),
      w2, b2.reshape(n_exp, 1, d))


def _combine_kernel(dest_ref, x_ref, wt_ref, gt_ref, g_ref, sh_ref, sc_ref, ys_ref, o_ref,
                    gbuf, rscr, sem, *, n_tok, tiles_per_batch, apply_norm):
    tm = x_ref.shape[1]
    i = pl.program_id(0) * tiles_per_batch + pl.program_id(1)

    def issue(t, carry):
        for k in range(TOP_K):
            src = dest_ref[k * n_tok + i * tm + t]
            pltpu.make_async_copy(ys_ref.at[pl.ds(src, 1)], gbuf.at[k, pl.ds(t, 1)], sem).start()
        return carry

    lax.fori_loop(0, tm, issue, 0)
    for k in range(TOP_K):
        pltpu.make_async_copy(ys_ref.at[pl.ds(0, tm)], gbuf.at[k], sem).wait()
    wt = wt_ref[...]
    ffn = wt[:, 0:1] * _from_row_tiles(gbuf[0], rscr)
    for k in range(1, TOP_K):
        ffn = ffn + wt[:, k:k + 1] * _from_row_tiles(gbuf[k], rscr)
    x2 = x_ref[0] + gt_ref[0] * ffn
    o_ref[0] = _rms_mod(x2, g_ref[...], sh_ref[0], sc_ref[0]) if apply_norm else x2


def _combine(dest_flat, x1, wt_t, gt, g, sh, sc, ys, apply_norm, tm=256):
    b, s, d = x1.shape
    t = b * s
    spb = s // tm
    nt = ys.shape[1]
    return pl.pallas_call(
        functools.partial(_combine_kernel, n_tok=t, tiles_per_batch=spb, apply_norm=apply_norm),
        out_shape=jax.ShapeDtypeStruct((b, s, d), F32),
        grid_spec=pltpu.PrefetchScalarGridSpec(
            num_scalar_prefetch=1,
            grid=(b, spb),
            in_specs=[pl.BlockSpec((1, tm, d), lambda bi, i, *_: (bi, i, 0)),
                      pl.BlockSpec((tm, TOP_K), lambda bi, i, *_: (bi * spb + i, 0)),
                      pl.BlockSpec((1, 1, d), lambda bi, i, *_: (bi, 0, 0)),
                      pl.BlockSpec((1, d), lambda bi, i, *_: (0, 0)),
                      pl.BlockSpec((1, 1, d), lambda bi, i, *_: (bi, 0, 0)),
                      pl.BlockSpec((1, 1, d), lambda bi, i, *_: (bi, 0, 0)),
                      pl.BlockSpec(memory_space=pl.ANY)],
            out_specs=pl.BlockSpec((1, tm, d), lambda bi, i, *_: (bi, i, 0)),
            scratch_shapes=[pltpu.VMEM((TOP_K, tm, nt, LANES), BF16),
                            pltpu.VMEM((tm * nt, LANES), F32),
                            pltpu.SemaphoreType.DMA]),
        compiler_params=_cparams(("arbitrary", "arbitrary")),
        name="combine",
    )(dest_flat, x1, wt_t, gt.reshape(b, 1, d), g.reshape(1, d), sh.reshape(b, 1, d),
      sc.reshape(b, 1, d), ys)


def _plan(idx, rank, counts, n_super):
    n_exp = counts.shape[0]
    nsb = (counts + SUPER - 1) // SUPER
    sb_end = jnp.cumsum(nsb)
    sb_start = sb_end - nsb
    dest = sb_start[idx] * SUPER + rank
    s_ids = jnp.arange(n_super, dtype=I32)
    nused = sb_end[-1]
    sbe = jnp.minimum(jnp.searchsorted(sb_end, jnp.minimum(s_ids, nused - 1), side='right'),
                      n_exp - 1).astype(I32)
    per = SUPER // SUB
    q_ids = jnp.arange(n_super * per, dtype=I32)
    q_s = q_ids // per
    q_e = sbe[q_s]
    left = counts[q_e] - ((q_s - sb_start[q_e]) * SUPER + (q_ids % per) * SUB)
    vrows = jnp.where(q_s < nused, jnp.clip(left, 0, SUB), 0)
    nsub = jnp.sum((vrows > 0).reshape(n_super, per), axis=1).astype(I32)
    zflag = (vrows < SUB).astype(I32)
    return dest.reshape(-1).astype(I32), sbe, nsub, nused.reshape(1).astype(I32), zflag


def kernel(x, c, ada_w, ada_b, norm1_g, w_in, pool_w, pool_scale, w_pool_out, ssm_lam_re, ssm_lam_im,
           ssm_log_dt, ssm_b_re, ssm_b_im, ssm_c_re, ssm_c_im, ssm_d, w_glu, b_glu, w_out, norm2_g,
           w_router, b_router, w1, b1, w2, b2, final_ada_w, final_ada_b, final_norm_g):
    b, s, d = x.shape
    t = b * s
    depth = ada_w.shape[0]
    n_exp = w_router.shape[-1]
    pwid = w_pool_out.shape[1]
    n_super = (t * TOP_K) // SUPER + n_exp

    c8 = jnp.zeros((8, d), F32).at[:b].set(c.astype(F32))
    fmod = _ada(c8, final_ada_w, final_ada_b)[:b]
    sh_o, sc_o = jnp.split(fmod, 2, axis=-1)
    for l in range(depth):
        mod = _ada(c8, ada_w[l], ada_b[l])[:b]
        sh_m, sc_m, gt_m, sh_f, sc_f, gt_f = jnp.split(mod, 6, axis=-1)
        z = _inproj(x, norm1_g[l], sh_m, sc_m, w_in[l].astype(BF16))
        ops = _s5_weights(ssm_lam_re[l], ssm_lam_im[l], ssm_log_dt[l], ssm_b_re[l], ssm_b_im[l],
                          ssm_c_re[l], ssm_c_im[l])
        ys = _s5(z, pwid, *ops, ssm_d[l])
        x = _mixout(z, ys, x, gt_m, pool_w[l].astype(BF16), pool_scale[l], w_pool_out[l].astype(BF16),
                    w_glu[l].astype(BF16), b_glu[l], w_out[l].astype(BF16))
        hp, idx, wt, rank, cnt = _router(x, norm2_g[l], sh_f, sc_f, w_router[l], b_router[l])
        dest, sbe, nsub, nused, zflag = _plan(idx, rank, cnt[:, 0], n_super)
        xs = _dispatch(dest, zflag, hp, n_super * SUPER)
        ye = _experts(sbe, nsub, nused, xs, w1[l], b1[l], w2[l], b2[l], n_super)
        last = l == depth - 1
        x = _combine(dest, x, wt.T, gt_f, final_norm_g, sh_o, sc_o, ye, apply_norm=last)
    return x
```

```python
import functools
import math

import jax
import jax.numpy as jnp
from jax import lax
from jax.experimental import pallas as pl
from jax.experimental.pallas import tpu as pltpu

F32 = jnp.float32
BF16 = jnp.bfloat16
I32 = jnp.int32
U32 = jnp.uint32

RMS_EPS = 1e-6
POOL_WINDOWS = (2, 4, 8, 16)
SSM_GROUP_DIM = 16
SSM_STATE = 64
LAMBDA_RE_MAX = -1e-4
TOP_K = 4
SWIGLU_ALPHA = 1.702
SWIGLU_LIMIT = 7.0

LANES = 128
V7X_VMEM_LIMIT = 56 * 1024 * 1024

CHUNK = 16
OCT = LANES // SSM_GROUP_DIM
SUB = 256
SUPER = 4 * SUB


def _cparams(sem, vmem=V7X_VMEM_LIMIT):
    return pltpu.CompilerParams(dimension_semantics=sem, vmem_limit_bytes=vmem)


def _const_spec(shape):
    nd = len(shape)
    return pl.BlockSpec(shape, lambda *_: (0,) * nd, pipeline_mode=pl.Buffered(1))


def _rms_mod(x, g, sh, sc):
    y = x * lax.rsqrt(jnp.mean(x * x, axis=-1, keepdims=True) + RMS_EPS) * g
    return y * (1.0 + sc) + sh


def _to_row_tiles(h, scr):
    m, d = h.shape
    nt = d // LANES
    for j in range(nt):
        scr[pl.ds(j, m, stride=nt), :] = h[:, j * LANES:(j + 1) * LANES]
    return scr[...].astype(BF16).reshape(m, nt, LANES)


def _from_row_tiles(x, scr):
    m, nt, _ = x.shape
    scr[...] = x.reshape(m * nt, LANES).astype(F32)
    return jnp.concatenate([scr[pl.ds(j, m, stride=nt), :] for j in range(nt)], axis=1)


def _ada_kernel(c_ref, w_ref, b_ref, o_ref):
    c = c_ref[...]
    ca = c * jax.nn.sigmoid(c)
    o_ref[...] = jnp.dot(ca.astype(BF16), w_ref[...].astype(BF16),
                         preferred_element_type=F32) + b_ref[...]


def _ada(c8, w, b, tn=1024):
    d, n = w.shape
    tn = math.gcd(tn, n)
    return pl.pallas_call(
        _ada_kernel,
        out_shape=jax.ShapeDtypeStruct((c8.shape[0], n), F32),
        grid=(n // tn,),
        in_specs=[pl.BlockSpec(c8.shape, lambda j: (0, 0)),
                  pl.BlockSpec((d, tn), lambda j: (0, j)),
                  pl.BlockSpec((1, tn), lambda j: (0, j))],
        out_specs=pl.BlockSpec((c8.shape[0], tn), lambda j: (0, j)),
        compiler_params=_cparams(("arbitrary",)),
        name="ada",
    )(c8, w, b.reshape(1, n))


def _inproj_kernel(x_ref, g_ref, sh_ref, sc_ref, w_ref, z_ref, *, ncol):
    h = _rms_mod(x_ref[0], g_ref[...], sh_ref[0], sc_ref[0]).astype(BF16)
    n = w_ref.shape[1]
    for c in range(n // ncol):
        z_ref[0, :, c * ncol:(c + 1) * ncol] = jnp.dot(
            h, w_ref[:, c * ncol:(c + 1) * ncol], preferred_element_type=F32).astype(BF16)


def _inproj(x, g, sh, sc, w, tm=512, ncol=1024):
    b, s, d = x.shape
    n = w.shape[1]
    ncol = math.gcd(ncol, n)
    return pl.pallas_call(
        functools.partial(_inproj_kernel, ncol=ncol),
        out_shape=jax.ShapeDtypeStruct((b, s, n), BF16),
        grid=(b, s // tm),
        in_specs=[pl.BlockSpec((1, tm, d), lambda bi, i: (bi, i, 0)),
                  _const_spec((1, d)),
                  pl.BlockSpec((1, 1, d), lambda bi, i: (bi, 0, 0)),
                  pl.BlockSpec((1, 1, d), lambda bi, i: (bi, 0, 0)),
                  _const_spec((d, n))],
        out_specs=pl.BlockSpec((1, tm, n), lambda bi, i: (bi, i, 0)),
        compiler_params=_cparams(("arbitrary", "arbitrary")),
        name="inproj",
    )(x, g.reshape(1, d), sh.reshape(b, 1, d), sc.reshape(b, 1, d), w)


def _s5_weights(lam_re, lam_im, log_dt, b_re, b_im, c_re, c_im):
    hp = lax.Precision.HIGHEST
    g_all, p = lam_re.shape
    h = b_re.shape[-1]
    no = g_all // OCT
    dt = jnp.exp(log_dt.astype(F32))[:, None]
    lre = jnp.minimum(lam_re.astype(F32), LAMBDA_RE_MAX)
    lim = lam_im.astype(F32)
    mag = jnp.exp(lre * dt)
    ang = lim * dt
    ab_re = mag * jnp.cos(ang)
    ab_im = mag * jnp.sin(ang)
    den = lre * lre + lim * lim
    nr = ab_re - 1.0
    ni = ab_im
    f_re = (nr * lre + ni * lim) / den
    f_im = (ni * lre - nr * lim) / den
    br_, bi_ = b_re.astype(F32), b_im.astype(F32)
    bb_re = f_re[..., None] * br_ - f_im[..., None] * bi_
    bb_im = f_re[..., None] * bi_ + f_im[..., None] * br_
    cr, ci = c_re.astype(F32), c_im.astype(F32)
    j = jnp.arange(CHUNK + 1, dtype=F32)[:, None, None]
    pw_mag = jnp.exp(j * (lre * dt)[None])
    pw_re = pw_mag * jnp.cos(j * ang[None])
    pw_im = pw_mag * jnp.sin(j * ang[None])
    cp_re = cr[None] * pw_re[:, :, None, :] - ci[None] * pw_im[:, :, None, :]
    cp_im = cr[None] * pw_im[:, :, None, :] + ci[None] * pw_re[:, :, None, :]
    klag = (jnp.einsum('jghp,gpi->gjhi', cp_re[:CHUNK], bb_re, precision=hp)
            - jnp.einsum('jghp,gpi->gjhi', cp_im[:CHUNK], bb_im, precision=hp))
    assert 2 * p == LANES and h * OCT == LANES
    eye = jnp.eye(OCT, dtype=F32)
    tk = jnp.arange(CHUNK)
    lag_t = jnp.einsum('ogjhi,gq->ojgiqh', klag.reshape(no, OCT, CHUNK, h, h), eye)
    lag_t = lag_t.reshape(no, CHUNK, LANES, LANES)
    pwr = pw_re[CHUNK - 1 - tk]
    pwi = pw_im[CHUNK - 1 - tk]
    s_re = pwr[..., None] * bb_re[None] - pwi[..., None] * bb_im[None]
    s_im = pwr[..., None] * bb_im[None] + pwi[..., None] * bb_re[None]
    s_ri = jnp.stack([s_re, s_im], axis=0).reshape(2, CHUNK, no, OCT, p, h)
    in_t = jnp.transpose(s_ri, (2, 1, 3, 5, 0, 4)).reshape(no, CHUNK, LANES, LANES)
    c_ri = jnp.stack([cp_re[1:], -cp_im[1:]], axis=0).reshape(2, CHUNK, no, OCT, h, p)
    out_t = jnp.transpose(c_ri, (2, 1, 0, 5, 3, 4)).reshape(no, CHUNK, LANES, LANES)
    are = pw_re[CHUNK].reshape(no, OCT, 1, p)
    aim = pw_im[CHUNK].reshape(no, OCT, 1, p)
    a1 = jnp.concatenate([are, are], axis=2).reshape(no, 1, OCT * LANES)
    a2 = jnp.concatenate([-aim, aim], axis=2).reshape(no, 1, OCT * LANES)
    return lag_t.astype(BF16), in_t.astype(BF16), out_t.astype(BF16), a1, a2


def _gelu_tanh(x):
    return 0.5 * x * (1.0 + jnp.tanh(math.sqrt(2.0 / math.pi) * (x + 0.044715 * x * x * x)))


def _s5_kernel(u_ref, lag_ref, in_ref, out_ref, a1_ref, a2_ref, d_ref, y_ref,
               uscr, sscr, pscr, yscr, state, t_op, b_op, c_op):
    nb, tm, _ = u_ref.shape
    nc = tm // CHUNK
    nq = a1_ref.shape[-1] // LANES

    @pl.when(pl.program_id(1) == 0)
    def _():
        state[...] = jnp.zeros_like(state)
        rowg = lax.broadcasted_iota(I32, (LANES, LANES), 0) // SSM_GROUP_DIM
        colg = lax.broadcasted_iota(I32, (LANES, LANES), 1) // SSM_GROUP_DIM
        zero = jnp.zeros((LANES, LANES), BF16)
        for k in range(CHUNK):
            in_k = in_ref[0, k]
            out_k = out_ref[0, k]
            for q in range(nq):
                b_op[k * LANES:(k + 1) * LANES, q * LANES:(q + 1) * LANES] = jnp.where(rowg == q, in_k, zero)
                c_op[q * LANES:(q + 1) * LANES, k * LANES:(k + 1) * LANES] = jnp.where(colg == q, out_k, zero)
            for t in range(CHUNK):
                t_op[k * LANES:(k + 1) * LANES, t * LANES:(t + 1) * LANES] = (
                    lag_ref[0, t - k] if t >= k else zero)

    for bi in range(nb):
        uscr[bi * tm:(bi + 1) * tm, :] = u_ref[bi].astype(F32)
    xk = jnp.concatenate(
        [uscr[pl.ds(k, nb * nc, stride=CHUNK), :] for k in range(CHUNK)], axis=1).astype(BF16)
    rr = nb * nc
    sc_all = jnp.dot(xk, b_op[...], preferred_element_type=F32)
    for q in range(nq):
        sscr[q * rr:(q + 1) * rr, :] = sc_all[:, q * LANES:(q + 1) * LANES]

    a1 = [a1_ref[0, :, q * LANES:(q + 1) * LANES] for q in range(nq)]
    a2 = [a2_ref[0, :, q * LANES:(q + 1) * LANES] for q in range(nq)]
    st = [state[:, q * LANES:(q + 1) * LANES] for q in range(nq)]
    for c in range(nc):
        for q in range(nq):
            pscr[pl.ds(q * rr + c, nb, stride=nc), :] = st[q]
            sc = sscr[pl.ds(q * rr + c, nb, stride=nc), :]
            st[q] = a1[q] * st[q] + a2[q] * pltpu.roll(st[q], LANES // 2, axis=1) + sc
    for q in range(nq):
        state[:, q * LANES:(q + 1) * LANES] = st[q]

    xprev = jnp.concatenate([pscr[q * rr:(q + 1) * rr, :] for q in range(nq)], axis=1)
    y = (jnp.dot(xk, t_op[...], preferred_element_type=F32)
         + jnp.dot(xprev.astype(BF16), c_op[...], preferred_element_type=F32))
    for t in range(CHUNK):
        yscr[pl.ds(t, nb * nc, stride=CHUNK), :] = y[:, t * LANES:(t + 1) * LANES]
    out = _gelu_tanh(yscr[...] + d_ref[...] * uscr[...])
    for bi in range(nb):
        y_ref[bi] = out[bi * tm:(bi + 1) * tm, :].astype(BF16)


def _s5(z, col0, lag_t, in_t, out_t, a1, a2, d_skip, tm=1024):
    nb, s, _ = z.shape
    no = lag_t.shape[0]
    w = no * LANES
    ns = a1.shape[-1]
    tm = min(tm, s)
    rows = nb * tm
    cb0 = col0 // LANES
    tile_spec = pl.BlockSpec((1, CHUNK, LANES, LANES), lambda o, i: (o, 0, 0, 0))
    return pl.pallas_call(
        _s5_kernel,
        out_shape=jax.ShapeDtypeStruct((nb, s, w), BF16),
        grid=(no, s // tm),
        in_specs=[pl.BlockSpec((nb, tm, LANES), lambda o, i: (0, i, cb0 + o)),
                  tile_spec, tile_spec, tile_spec,
                  pl.BlockSpec((1, 1, ns), lambda o, i: (o, 0, 0)),
                  pl.BlockSpec((1, 1, ns), lambda o, i: (o, 0, 0)),
                  pl.BlockSpec((1, LANES), lambda o, i: (0, o))],
        out_specs=pl.BlockSpec((nb, tm, LANES), lambda o, i: (0, i, o)),
        scratch_shapes=[pltpu.VMEM((rows, LANES), F32),
                        pltpu.VMEM((rows // CHUNK * (ns // LANES), LANES), F32),
                        pltpu.VMEM((rows // CHUNK * (ns // LANES), LANES), F32),
                        pltpu.VMEM((rows, LANES), F32),
                        pltpu.VMEM((nb, ns), F32),
                        pltpu.VMEM((CHUNK * LANES, CHUNK * LANES), BF16),
                        pltpu.VMEM((CHUNK * LANES, ns), BF16),
                        pltpu.VMEM((ns, CHUNK * LANES), BF16)],
        compiler_params=_cparams(("arbitrary", "arbitrary")),
        name="s5",
    )(z, lag_t, in_t, out_t, a1, a2, d_skip.reshape(1, w))


def _mixout_kernel(up_ref, halo_ref, ys_ref, gp_ref, gs_ref, x_ref, gt_ref,
                   pw_ref, ps_ref, wpo_ref, wglu_ref, bglu_ref, wout_ref, o_ref, escr):
    tm = up_ref.shape[1]
    i = pl.program_id(1)
    gw = pw_ref.shape[1]
    hal = halo_ref.shape[1]
    d = o_ref.shape[2]
    escr[0:hal, :] = jnp.where(i > 0, halo_ref[0].astype(F32), 0.0)
    escr[hal:hal + tm, :] = up_ref[0].astype(F32)
    pos = (i * tm + lax.broadcasted_iota(I32, (tm, gw), 0) + 1).astype(F32)
    mixed = []
    for g, win in enumerate(POOL_WINDOWS):
        cols = slice(g * gw, (g + 1) * gw)
        cur = escr[hal:hal + tm, cols]
        acc = cur
        for j in range(1, win):
            acc = acc + escr[hal - j:hal - j + tm, cols]
        pooled = acc / jnp.minimum(pos, float(win)) - cur
        mixed.append(jnp.dot(pooled.astype(BF16), pw_ref[g], preferred_element_type=F32))
    mixed = jnp.concatenate(mixed, axis=1) * ps_ref[...]
    y_pool = jnp.dot(mixed.astype(BF16), wpo_ref[...], preferred_element_type=F32)
    glu = jnp.dot(ys_ref[0], wglu_ref[...], preferred_element_type=F32) + bglu_ref[...]
    y_ssm = glu[:, :d] * jax.nn.sigmoid(glu[:, d:])
    merged = (jax.nn.sigmoid(gp_ref[0].astype(F32)) * y_pool
              + jax.nn.sigmoid(gs_ref[0].astype(F32)) * y_ssm)
    mix = jnp.dot(merged.astype(BF16), wout_ref[...], preferred_element_type=F32)
    o_ref[0] = x_ref[0] + gt_ref[0] * mix


def _mixout(z, ys, x, gt, pool_w, pool_scale, w_pool_out, w_glu, b_glu, w_out, tm=256):
    b, s, d = x.shape
    pwid = w_pool_out.shape[0]
    hal = max(POOL_WINDOWS)
    tpb = tm // hal
    gcol = (pwid + ys.shape[2]) // d
    return pl.pallas_call(
        _mixout_kernel,
        out_shape=jax.ShapeDtypeStruct((b, s, d), F32),
        grid=(b, s // tm),
        in_specs=[pl.BlockSpec((1, tm, pwid), lambda bi, i: (bi, i, 0)),
                  pl.BlockSpec((1, hal, pwid), lambda bi, i: (bi, jnp.maximum(i * tpb - 1, 0), 0)),
                  pl.BlockSpec((1, tm, ys.shape[2]), lambda bi, i: (bi, i, 0)),
                  pl.BlockSpec((1, tm, d), lambda bi, i: (bi, i, gcol)),
                  pl.BlockSpec((1, tm, d), lambda bi, i: (bi, i, gcol + 1)),
                  pl.BlockSpec((1, tm, d), lambda bi, i: (bi, i, 0)),
                  pl.BlockSpec((1, 1, d), lambda bi, i: (bi, 0, 0)),
                  _const_spec(pool_w.shape),
                  _const_spec((1, pwid)),
                  _const_spec(w_pool_out.shape),
                  _const_spec(w_glu.shape),
                  _const_spec((1, w_glu.shape[1])),
                  _const_spec(w_out.shape)],
        out_specs=pl.BlockSpec((1, tm, d), lambda bi, i: (bi, i, 0)),
        scratch_shapes=[pltpu.VMEM((hal + tm, pwid), F32)],
        compiler_params=_cparams(("arbitrary", "arbitrary")),
        name="mixout",
    )(z, z, ys, z, z, x, gt.reshape(b, 1, d), pool_w, pool_scale.reshape(1, pwid),
      w_pool_out, w_glu, b_glu.reshape(1, -1), w_out)


def _router_kernel(x_ref, g_ref, sh_ref, sc_ref, wr_ref, br_ref,
                   hp_ref, idx_ref, wt_ref, rank_ref, cnt_ref, carry, rscr, *, n_exp):
    tm = x_ref.shape[1]
    first = (pl.program_id(0) == 0) & (pl.program_id(1) == 0)

    @pl.when(first)
    def _():
        carry[...] = jnp.zeros_like(carry)

    h2 = _rms_mod(x_ref[0], g_ref[...], sh_ref[0], sc_ref[0])
    hp_ref[...] = _to_row_tiles(h2, rscr)
    logits = jnp.dot(h2, wr_ref[...], preferred_element_type=F32,
                     precision=lax.Precision.HIGHEST) + br_ref[...]
    lt = logits.T[:n_exp, :]
    eid = lax.broadcasted_iota(I32, lt.shape, 0).astype(F32)
    vals, idxs, hots = [], [], []
    v = lt
    for _ in range(TOP_K):
        m = jnp.max(v, axis=0, keepdims=True)
        sel = jnp.min(jnp.where(v == m, eid, float(n_exp)), axis=0, keepdims=True)
        hot = eid == sel
        vals.append(m)
        idxs.append(sel)
        hots.append(hot)
        v = jnp.where(hot, -jnp.inf, v)
    ex = [jnp.exp(m - vals[0]) for m in vals]
    tot = ex[0] + ex[1] + ex[2] + ex[3]
    msum = sum(h.astype(F32) for h in hots)
    tri = (lax.broadcasted_iota(I32, (tm, tm), 0) < lax.broadcasted_iota(I32, (tm, tm), 1))
    before = jnp.dot(msum.astype(BF16), tri.astype(BF16), preferred_element_type=F32) + carry[:, 0:1]
    for k in range(TOP_K):
        idx_ref[k:k + 1, :] = idxs[k].astype(I32)
        wt_ref[k:k + 1, :] = ex[k] / tot
        rank_ref[k:k + 1, :] = jnp.sum(jnp.where(hots[k], before, 0.0), axis=0,
                                       keepdims=True).astype(I32)
    carry[...] = carry[...] + jnp.sum(msum, axis=1, keepdims=True)
    cnt_ref[...] = carry[...].astype(I32)


def _router(x1, g, sh, sc, w_router, b_router, tm=512):
    b, s, d = x1.shape
    t = b * s
    e = w_router.shape[1]
    nt = d // LANES
    wr = jnp.zeros((d, LANES), F32).at[:, :e].set(w_router.astype(F32))
    br = jnp.full((1, LANES), -1e30, F32).at[0, :e].set(b_router.astype(F32))
    spb = s // tm
    return pl.pallas_call(
        functools.partial(_router_kernel, n_exp=e),
        out_shape=[jax.ShapeDtypeStruct((t, nt, LANES), BF16),
                   jax.ShapeDtypeStruct((TOP_K, t), I32),
                   jax.ShapeDtypeStruct((TOP_K, t), F32),
                   jax.ShapeDtypeStruct((TOP_K, t), I32),
                   jax.ShapeDtypeStruct((e, LANES), I32)],
        grid=(b, spb),
        in_specs=[pl.BlockSpec((1, tm, d), lambda bi, i: (bi, i, 0)),
                  _const_spec((1, d)),
                  pl.BlockSpec((1, 1, d), lambda bi, i: (bi, 0, 0)),
                  pl.BlockSpec((1, 1, d), lambda bi, i: (bi, 0, 0)),
                  _const_spec((d, LANES)),
                  _const_spec((1, LANES))],
        out_specs=[pl.BlockSpec((tm, nt, LANES), lambda bi, i: (bi * spb + i, 0, 0)),
                   pl.BlockSpec((TOP_K, tm), lambda bi, i: (0, bi * spb + i)),
                   pl.BlockSpec((TOP_K, tm), lambda bi, i: (0, bi * spb + i)),
                   pl.BlockSpec((TOP_K, tm), lambda bi, i: (0, bi * spb + i)),
                   pl.BlockSpec((e, LANES), lambda bi, i: (0, 0))],
        scratch_shapes=[pltpu.VMEM((e, LANES), F32),
                        pltpu.VMEM((tm * nt, LANES), F32)],
        compiler_params=_cparams(("arbitrary", "arbitrary")),
        name="router",
    )(x1, g.reshape(1, d), sh.reshape(b, 1, d), sc.reshape(b, 1, d), wr, br)


def _dispatch_kernel(dest_ref, zflag_ref, hp_ref, xs_ref, zbuf, sem, zsem, *, n_tok):
    tm = hp_ref.shape[0]
    i = pl.program_id(0)

    @pl.when(i == 0)
    def _():
        zbuf[...] = jnp.zeros_like(zbuf)

        def zcopy(q):
            r0 = pl.multiple_of(q * SUB, SUB)
            return pltpu.make_async_copy(zbuf, xs_ref.at[pl.ds(r0, SUB)], zsem)

        def zstart(q, carry):
            pl.when(zflag_ref[q] != 0)(lambda: zcopy(q).start())
            return carry

        def zwait(q, carry):
            pl.when(zflag_ref[q] != 0)(lambda: zcopy(q).wait())
            return carry

        lax.fori_loop(0, zflag_ref.shape[0], zstart, 0)
        lax.fori_loop(0, zflag_ref.shape[0], zwait, 0)

    def issue(t, carry):
        for k in range(TOP_K):
            dst = dest_ref[k * n_tok + i * tm + t]
            pltpu.make_async_copy(hp_ref.at[pl.ds(t, 1)], xs_ref.at[pl.ds(dst, 1)], sem).start()
        return carry

    lax.fori_loop(0, tm, issue, 0)
    for k in range(TOP_K):
        pltpu.make_async_copy(hp_ref, xs_ref.at[pl.ds(0, tm)], sem).wait()


def _dispatch(dest_flat, zflag, hp, n_rows, tm=256):
    t, nt, _ = hp.shape
    return pl.pallas_call(
        functools.partial(_dispatch_kernel, n_tok=t),
        out_shape=jax.ShapeDtypeStruct((n_rows, nt, LANES), BF16),
        grid_spec=pltpu.PrefetchScalarGridSpec(
            num_scalar_prefetch=2,
            grid=(t // tm,),
            in_specs=[pl.BlockSpec((tm, nt, LANES), lambda i, *_: (i, 0, 0))],
            out_specs=pl.BlockSpec(memory_space=pl.ANY),
            scratch_shapes=[pltpu.VMEM((SUB, nt, LANES), BF16),
                            pltpu.SemaphoreType.DMA,
                            pltpu.SemaphoreType.DMA]),
        compiler_params=_cparams(("arbitrary",)),
        name="dispatch",
    )(dest_flat, zflag, hp)


def _experts_kernel(sbe_ref, nsub_ref, nused_ref, x_ref, w1g_ref, w1l_ref, b1g_ref, b1l_ref,
                    w2_ref, b2_ref, o_ref, xb, acc, wg, wl, w2b, rscr):
    s = pl.program_id(0)
    j = pl.program_id(1)
    nj = pl.num_programs(1)
    n = nsub_ref[s]

    @pl.when(n > 0)
    def _():
        wg[...] = w1g_ref[0].astype(BF16)
        wl[...] = w1l_ref[0].astype(BF16)
        w2b[...] = w2_ref[0].astype(BF16)

    for r in range(SUPER // SUB):
        rows = slice(r * SUB, (r + 1) * SUB)

        @pl.when(r < n)
        def _():
            @pl.when(j == 0)
            def _():
                xb[rows, :] = _from_row_tiles(x_ref[rows], rscr).astype(BF16)

            xr = xb[rows, :]
            g = jnp.dot(xr, wg[...], preferred_element_type=F32) + b1g_ref[0]
            l = jnp.dot(xr, wl[...], preferred_element_type=F32) + b1l_ref[0]
            xg = jnp.minimum(g, SWIGLU_LIMIT)
            xl = jnp.clip(l, -SWIGLU_LIMIT, SWIGLU_LIMIT)
            act = xg * jax.nn.sigmoid(SWIGLU_ALPHA * xg) * (xl + 1.0)
            contrib = jnp.dot(act.astype(BF16), w2b[...], preferred_element_type=F32)

            @pl.when(j == 0)
            def _():
                acc[rows, :] = contrib + b2_ref[0]

            @pl.when(j > 0)
            def _():
                acc[rows, :] = acc[rows, :] + contrib

            @pl.when(j == nj - 1)
            def _():
                o_ref[rows] = _to_row_tiles(acc[rows, :], rscr)

        @pl.when((r >= n) & (j == nj - 1))
        def _():
            o_ref[rows] = jnp.zeros((SUB,) + o_ref.shape[1:], BF16)


def _experts(sbe, nsub, nused, xs, w1, b1, w2, b2, n_super, tf=256):
    n_exp, d, f2 = w1.shape
    f = f2 // 2
    tf = min(tf, f)
    nj = f // tf
    nt = xs.shape[1]

    def x_map(s, j, sbe, nsub, nused):
        return (jnp.minimum(s, nused[0] - 1), 0, 0)

    def jj(s, j, nused):
        return jnp.where(s < nused[0], j, nj - 1)

    return pl.pallas_call(
        _experts_kernel,
        out_shape=jax.ShapeDtypeStruct((n_super * SUPER, nt, LANES), BF16),
        grid_spec=pltpu.PrefetchScalarGridSpec(
            num_scalar_prefetch=3,
            grid=(n_super, nj),
            in_specs=[pl.BlockSpec((SUPER, nt, LANES), x_map),
                      pl.BlockSpec((1, d, tf), lambda s, j, sbe, nsub, nu: (sbe[s], 0, jj(s, j, nu))),
                      pl.BlockSpec((1, d, tf), lambda s, j, sbe, nsub, nu: (sbe[s], 0, nj + jj(s, j, nu))),
                      pl.BlockSpec((1, 1, tf), lambda s, j, sbe, nsub, nu: (sbe[s], 0, jj(s, j, nu))),
                      pl.BlockSpec((1, 1, tf), lambda s, j, sbe, nsub, nu: (sbe[s], 0, nj + jj(s, j, nu))),
                      pl.BlockSpec((1, tf, d), lambda s, j, sbe, nsub, nu: (sbe[s], jj(s, j, nu), 0)),
                      pl.BlockSpec((1, 1, d), lambda s, j, sbe, nsub, nu: (sbe[s], 0, 0))],
            out_specs=pl.BlockSpec((SUPER, nt, LANES), lambda s, j, *_: (s, 0, 0)),
            scratch_shapes=[pltpu.VMEM((SUPER, d), BF16),
                            pltpu.VMEM((SUPER, d), F32),
                            pltpu.VMEM((d, tf), BF16),
                            pltpu.VMEM((d, tf), BF16),
                            pltpu.VMEM((tf, d), BF16),
                            pltpu.VMEM((SUB * nt, LANES), F32)]),
        compiler_params=_cparams(("arbitrary", "arbitrary")),
        name="experts",
    )(sbe, nsub, nused, xs, w1, w1, b1.reshape(n_exp, 1, f2), b1.reshape(n_exp, 1, f2),
      w2, b2.reshape(n_exp, 1, d))


def _combine_kernel(dest_ref, x_ref, wt_ref, gt_ref, g_ref, sh_ref, sc_ref, ys_ref, o_ref,
                    gbuf, rscr, sem, *, n_tok, tiles_per_batch, apply_norm):
    tm = x_ref.shape[1]
    i = pl.program_id(0) * tiles_per_batch + pl.program_id(1)

    def issue(t, carry):
        for k in range(TOP_K):
            src = dest_ref[k * n_tok + i * tm + t]
            pltpu.make_async_copy(ys_ref.at[pl.ds(src, 1)], gbuf.at[k, pl.ds(t, 1)], sem).start()
        return carry

    lax.fori_loop(0, tm, issue, 0)
    for k in range(TOP_K):
        pltpu.make_async_copy(ys_ref.at[pl.ds(0, tm)], gbuf.at[k], sem).wait()
    wt = wt_ref[...]
    ffn = wt[:, 0:1] * _from_row_tiles(gbuf[0], rscr)
    for k in range(1, TOP_K):
        ffn = ffn + wt[:, k:k + 1] * _from_row_tiles(gbuf[k], rscr)
    x2 = x_ref[0] + gt_ref[0] * ffn
    o_ref[0] = _rms_mod(x2, g_ref[...], sh_ref[0], sc_ref[0]) if apply_norm else x2


def _combine(dest_flat, x1, wt_t, gt, g, sh, sc, ys, apply_norm, tm=256):
    b, s, d = x1.shape
    t = b * s
    spb = s // tm
    nt = ys.shape[1]
    return pl.pallas_call(
        functools.partial(_combine_kernel, n_tok=t, tiles_per_batch=spb, apply_norm=apply_norm),
        out_shape=jax.ShapeDtypeStruct((b, s, d), F32),
        grid_spec=pltpu.PrefetchScalarGridSpec(
            num_scalar_prefetch=1,
            grid=(b, spb),
            in_specs=[pl.BlockSpec((1, tm, d), lambda bi, i, *_: (bi, i, 0)),
                      pl.BlockSpec((tm, TOP_K), lambda bi, i, *_: (bi * spb + i, 0)),
                      pl.BlockSpec((1, 1, d), lambda bi, i, *_: (bi, 0, 0)),
                      pl.BlockSpec((1, d), lambda bi, i, *_: (0, 0)),
                      pl.BlockSpec((1, 1, d), lambda bi, i, *_: (bi, 0, 0)),
                      pl.BlockSpec((1, 1, d), lambda bi, i, *_: (bi, 0, 0)),
                      pl.BlockSpec(memory_space=pl.ANY)],
            out_specs=pl.BlockSpec((1, tm, d), lambda bi, i, *_: (bi, i, 0)),
            scratch_shapes=[pltpu.VMEM((TOP_K, tm, nt, LANES), BF16),
                            pltpu.VMEM((tm * nt, LANES), F32),
                            pltpu.SemaphoreType.DMA]),
        compiler_params=_cparams(("arbitrary", "arbitrary")),
        name="combine",
    )(dest_flat, x1, wt_t, gt.reshape(b, 1, d), g.reshape(1, d), sh.reshape(b, 1, d),
      sc.reshape(b, 1, d), ys)


def _plan(idx, rank, counts, n_super):
    n_exp = counts.shape[0]
    nsb = (counts + SUPER - 1) // SUPER
    sb_end = jnp.cumsum(nsb)
    sb_start = sb_end - nsb
    dest = sb_start[idx] * SUPER + rank
    s_ids = jnp.arange(n_super, dtype=I32)
    nused = sb_end[-1]
    sbe = jnp.minimum(jnp.searchsorted(sb_end, jnp.minimum(s_ids, nused - 1), side='right'),
                      n_exp - 1).astype(I32)
    per = SUPER // SUB
    q_ids = jnp.arange(n_super * per, dtype=I32)
    q_s = q_ids // per
    q_e = sbe[q_s]
    left = counts[q_e] - ((q_s - sb_start[q_e]) * SUPER + (q_ids % per) * SUB)
    vrows = jnp.where(q_s < nused, jnp.clip(left, 0, SUB), 0)
    nsub = jnp.sum((vrows > 0).reshape(n_super, per), axis=1).astype(I32)
    zflag = (vrows < SUB).astype(I32)
    return dest.reshape(-1).astype(I32), sbe, nsub, nused.reshape(1).astype(I32), zflag


def kernel(x, c, ada_w, ada_b, norm1_g, w_in, pool_w, pool_scale, w_pool_out, ssm_lam_re, ssm_lam_im,
           ssm_log_dt, ssm_b_re, ssm_b_im, ssm_c_re, ssm_c_im, ssm_d, w_glu, b_glu, w_out, norm2_g,
           w_router, b_router, w1, b1, w2, b2, final_ada_w, final_ada_b, final_norm_g):
    b, s, d = x.shape
    t = b * s
    depth = ada_w.shape[0]
    n_exp = w_router.shape[-1]
    pwid = w_pool_out.shape[1]
    n_super = (t * TOP_K) // SUPER + n_exp

    c8 = jnp.zeros((8, d), F32).at[:b].set(c.astype(F32))
    fmod = _ada(c8, final_ada_w, final_ada_b)[:b]
    sh_o, sc_o = jnp.split(fmod, 2, axis=-1)
    for l in range(depth):
        mod = _ada(c8, ada_w[l], ada_b[l])[:b]
        sh_m, sc_m, gt_m, sh_f, sc_f, gt_f = jnp.split(mod, 6, axis=-1)
        z = _inproj(x, norm1_g[l], sh_m, sc_m, w_in[l].astype(BF16))
        ops = _s5_weights(ssm_lam_re[l], ssm_lam_im[l], ssm_log_dt[l], ssm_b_re[l], ssm_b_im[l],
                          ssm_c_re[l], ssm_c_im[l])
        ys = _s5(z, pwid, *ops, ssm_d[l])
        x = _mixout(z, ys, x, gt_m, pool_w[l].astype(BF16), pool_scale[l], w_pool_out[l].astype(BF16),
                    w_glu[l].astype(BF16), b_glu[l], w_out[l].astype(BF16))
        hp, idx, wt, rank, cnt = _router(x, norm2_g[l], sh_f, sc_f, w_router[l], b_router[l])
        dest, sbe, nsub, nused, zflag = _plan(idx, rank, cnt[:, 0], n_super)
        xs = _dispatch(dest, zflag, hp, n_super * SUPER)
        ye = _experts(sbe, nsub, nused, xs, w1[l], b1[l], w2[l], b2[l], n_super)
        last = l == depth - 1
        x = _combine(dest, x, wt.T, gt_f, final_norm_g, sh_o, sc_o, ye, apply_norm=last)
    return x
```

```python
import functools
import math

import jax
import jax.numpy as jnp
from jax import lax
from jax.experimental import pallas as pl
from jax.experimental.pallas import tpu as pltpu

F32 = jnp.float32
BF16 = jnp.bfloat16
I32 = jnp.int32
U32 = jnp.uint32

RMS_EPS = 1e-6
POOL_WINDOWS = (2, 4, 8, 16)
SSM_GROUP_DIM = 16
SSM_STATE = 64
LAMBDA_RE_MAX = -1e-4
TOP_K = 4
SWIGLU_ALPHA = 1.702
SWIGLU_LIMIT = 7.0

LANES = 128
V7X_VMEM_LIMIT = 56 * 1024 * 1024

CHUNK = 16
OCT = LANES // SSM_GROUP_DIM
SUB = 256
SUPER = 4 * SUB


def _cparams(sem, vmem=V7X_VMEM_LIMIT):
    return pltpu.CompilerParams(dimension_semantics=sem, vmem_limit_bytes=vmem)


def _const_spec(shape):
    nd = len(shape)
    return pl.BlockSpec(shape, lambda *_: (0,) * nd, pipeline_mode=pl.Buffered(1))


def _rms_mod(x, g, sh, sc):
    y = x * lax.rsqrt(jnp.mean(x * x, axis=-1, keepdims=True) + RMS_EPS) * g
    return y * (1.0 + sc) + sh


def _pair_width(d):
    return min(2 * LANES, d // 2)


def _pack_rows(h):
    d = h.shape[1]
    pair = _pair_width(d)
    hb = lax.bitcast_convert_type(h.astype(BF16).astype(F32), U32)
    words = []
    for n in range(d // (2 * pair)):
        lo = hb[:, 2 * pair * n:2 * pair * n + pair]
        hi = hb[:, 2 * pair * n + pair:2 * pair * (n + 1)]
        words.append((lo >> 16) | (hi & jnp.uint32(0xFFFF0000)))
    return jnp.concatenate(words, axis=1)


def _unpack_rows(w):
    d = 2 * w.shape[1]
    pair = _pair_width(d)
    cols = []
    for n in range(d // (2 * pair)):
        wn = w[:, pair * n:pair * (n + 1)]
        cols.append(lax.bitcast_convert_type(wn << 16, F32))
        cols.append(lax.bitcast_convert_type(wn & jnp.uint32(0xFFFF0000), F32))
    return jnp.concatenate(cols, axis=1)


def _ada_kernel(c_ref, w_ref, b_ref, o_ref):
    c = c_ref[...]
    ca = c * jax.nn.sigmoid(c)
    o_ref[...] = jnp.dot(ca.astype(BF16), w_ref[...].astype(BF16),
                         preferred_element_type=F32) + b_ref[...]


def _ada(c8, w, b, tn=1024):
    d, n = w.shape
    tn = math.gcd(tn, n)
    return pl.pallas_call(
        _ada_kernel,
        out_shape=jax.ShapeDtypeStruct((c8.shape[0], n), F32),
        grid=(n // tn,),
        in_specs=[pl.BlockSpec(c8.shape, lambda j: (0, 0)),
                  pl.BlockSpec((d, tn), lambda j: (0, j)),
                  pl.BlockSpec((1, tn), lambda j: (0, j))],
        out_specs=pl.BlockSpec((c8.shape[0], tn), lambda j: (0, j)),
        compiler_params=_cparams(("arbitrary",)),
        name="ada",
    )(c8, w, b.reshape(1, n))


def _inproj_kernel(x_ref, g_ref, sh_ref, sc_ref, w_ref, z_ref, *, ncol):
    h = _rms_mod(x_ref[0], g_ref[...], sh_ref[0], sc_ref[0]).astype(BF16)
    n = w_ref.shape[1]
    for c in range(n // ncol):
        z_ref[0, :, c * ncol:(c + 1) * ncol] = jnp.dot(
            h, w_ref[:, c * ncol:(c + 1) * ncol], preferred_element_type=F32).astype(BF16)


def _inproj(x, g, sh, sc, w, tm=512, ncol=1024):
    b, s, d = x.shape
    n = w.shape[1]
    ncol = math.gcd(ncol, n)
    return pl.pallas_call(
        functools.partial(_inproj_kernel, ncol=ncol),
        out_shape=jax.ShapeDtypeStruct((b, s, n), BF16),
        grid=(b, s // tm),
        in_specs=[pl.BlockSpec((1, tm, d), lambda bi, i: (bi, i, 0)),
                  _const_spec((1, d)),
                  pl.BlockSpec((1, 1, d), lambda bi, i: (bi, 0, 0)),
                  pl.BlockSpec((1, 1, d), lambda bi, i: (bi, 0, 0)),
                  _const_spec((d, n))],
        out_specs=pl.BlockSpec((1, tm, n), lambda bi, i: (bi, i, 0)),
        compiler_params=_cparams(("arbitrary", "arbitrary")),
        name="inproj",
    )(x, g.reshape(1, d), sh.reshape(b, 1, d), sc.reshape(b, 1, d), w)


def _s5_weights(lam_re, lam_im, log_dt, b_re, b_im, c_re, c_im):
    hp = lax.Precision.HIGHEST
    g_all, p = lam_re.shape
    h = b_re.shape[-1]
    no = g_all // OCT
    dt = jnp.exp(log_dt.astype(F32))[:, None]
    lre = jnp.minimum(lam_re.astype(F32), LAMBDA_RE_MAX)
    lim = lam_im.astype(F32)
    mag = jnp.exp(lre * dt)
    ang = lim * dt
    ab_re = mag * jnp.cos(ang)
    ab_im = mag * jnp.sin(ang)
    den = lre * lre + lim * lim
    nr = ab_re - 1.0
    ni = ab_im
    f_re = (nr * lre + ni * lim) / den
    f_im = (ni * lre - nr * lim) / den
    br_, bi_ = b_re.astype(F32), b_im.astype(F32)
    bb_re = f_re[..., None] * br_ - f_im[..., None] * bi_
    bb_im = f_re[..., None] * bi_ + f_im[..., None] * br_
    cr, ci = c_re.astype(F32), c_im.astype(F32)
    j = jnp.arange(CHUNK + 1, dtype=F32)[:, None, None]
    pw_mag = jnp.exp(j * (lre * dt)[None])
    pw_re = pw_mag * jnp.cos(j * ang[None])
    pw_im = pw_mag * jnp.sin(j * ang[None])
    cp_re = cr[None] * pw_re[:, :, None, :] - ci[None] * pw_im[:, :, None, :]
    cp_im = cr[None] * pw_im[:, :, None, :] + ci[None] * pw_re[:, :, None, :]
    klag = (jnp.einsum('jghp,gpi->gjhi', cp_re[:CHUNK], bb_re, precision=hp)
            - jnp.einsum('jghp,gpi->gjhi', cp_im[:CHUNK], bb_im, precision=hp))
    assert 2 * p == LANES and h * OCT == LANES
    eye = jnp.eye(OCT, dtype=F32)
    tk = jnp.arange(CHUNK)
    lag_t = jnp.einsum('ogjhi,gq->ojgiqh', klag.reshape(no, OCT, CHUNK, h, h), eye)
    lag_t = lag_t.reshape(no, CHUNK, LANES, LANES)
    pwr = pw_re[CHUNK - 1 - tk]
    pwi = pw_im[CHUNK - 1 - tk]
    s_re = pwr[..., None] * bb_re[None] - pwi[..., None] * bb_im[None]
    s_im = pwr[..., None] * bb_im[None] + pwi[..., None] * bb_re[None]
    s_ri = jnp.stack([s_re, s_im], axis=0).reshape(2, CHUNK, no, OCT, p, h)
    in_t = jnp.transpose(s_ri, (2, 1, 3, 5, 0, 4)).reshape(no, CHUNK, LANES, LANES)
    c_ri = jnp.stack([cp_re[1:], -cp_im[1:]], axis=0).reshape(2, CHUNK, no, OCT, h, p)
    out_t = jnp.transpose(c_ri, (2, 1, 0, 5, 3, 4)).reshape(no, CHUNK, LANES, LANES)
    are = pw_re[CHUNK].reshape(no, OCT, 1, p)
    aim = pw_im[CHUNK].reshape(no, OCT, 1, p)
    a1 = jnp.concatenate([are, are], axis=2).reshape(no, 1, OCT * LANES)
    a2 = jnp.concatenate([-aim, aim], axis=2).reshape(no, 1, OCT * LANES)
    return lag_t.astype(BF16), in_t.astype(BF16), out_t.astype(BF16), a1, a2


def _gelu_tanh(x):
    return 0.5 * x * (1.0 + jnp.tanh(math.sqrt(2.0 / math.pi) * (x + 0.044715 * x * x * x)))


def _s5_kernel(u_ref, lag_ref, in_ref, out_ref, a1_ref, a2_ref, d_ref, y_ref,
               uscr, sscr, pscr, yscr, state, t_op, b_op, c_op):
    nb, tm, _ = u_ref.shape
    nc = tm // CHUNK
    nq = a1_ref.shape[-1] // LANES

    @pl.when(pl.program_id(1) == 0)
    def _():
        state[...] = jnp.zeros_like(state)
        rowg = lax.broadcasted_iota(I32, (LANES, LANES), 0) // SSM_GROUP_DIM
        colg = lax.broadcasted_iota(I32, (LANES, LANES), 1) // SSM_GROUP_DIM
        zero = jnp.zeros((LANES, LANES), BF16)
        for k in range(CHUNK):
            in_k = in_ref[0, k]
            out_k = out_ref[0, k]
            for q in range(nq):
                b_op[k * LANES:(k + 1) * LANES, q * LANES:(q + 1) * LANES] = jnp.where(rowg == q, in_k, zero)
                c_op[q * LANES:(q + 1) * LANES, k * LANES:(k + 1) * LANES] = jnp.where(colg == q, out_k, zero)
            for t in range(CHUNK):
                t_op[k * LANES:(k + 1) * LANES, t * LANES:(t + 1) * LANES] = (
                    lag_ref[0, t - k] if t >= k else zero)

    for bi in range(nb):
        uscr[bi * tm:(bi + 1) * tm, :] = u_ref[bi].astype(F32)
    xk = jnp.concatenate(
        [uscr[pl.ds(k, nb * nc, stride=CHUNK), :] for k in range(CHUNK)], axis=1).astype(BF16)
    rr = nb * nc
    sc_all = jnp.dot(xk, b_op[...], preferred_element_type=F32)
    for q in range(nq):
        sscr[q * rr:(q + 1) * rr, :] = sc_all[:, q * LANES:(q + 1) * LANES]

    a1 = [a1_ref[0, :, q * LANES:(q + 1) * LANES] for q in range(nq)]
    a2 = [a2_ref[0, :, q * LANES:(q + 1) * LANES] for q in range(nq)]
    st = [state[:, q * LANES:(q + 1) * LANES] for q in range(nq)]
    for c in range(nc):
        for q in range(nq):
            pscr[pl.ds(q * rr + c, nb, stride=nc), :] = st[q]
            sc = sscr[pl.ds(q * rr + c, nb, stride=nc), :]
            st[q] = a1[q] * st[q] + a2[q] * pltpu.roll(st[q], LANES // 2, axis=1) + sc
    for q in range(nq):
        state[:, q * LANES:(q + 1) * LANES] = st[q]

    xprev = jnp.concatenate([pscr[q * rr:(q + 1) * rr, :] for q in range(nq)], axis=1)
    y = (jnp.dot(xk, t_op[...], preferred_element_type=F32)
         + jnp.dot(xprev.astype(BF16), c_op[...], preferred_element_type=F32))
    for t in range(CHUNK):
        yscr[pl.ds(t, nb * nc, stride=CHUNK), :] = y[:, t * LANES:(t + 1) * LANES]
    out = _gelu_tanh(yscr[...] + d_ref[...] * uscr[...])
    for bi in range(nb):
        y_ref[bi] = out[bi * tm:(bi + 1) * tm, :].astype(BF16)


def _s5(z, col0, lag_t, in_t, out_t, a1, a2, d_skip, tm=1024):
    nb, s, _ = z.shape
    no = lag_t.shape[0]
    w = no * LANES
    ns = a1.shape[-1]
    tm = min(tm, s)
    rows = nb * tm
    cb0 = col0 // LANES
    tile_spec = pl.BlockSpec((1, CHUNK, LANES, LANES), lambda o, i: (o, 0, 0, 0))
    return pl.pallas_call(
        _s5_kernel,
        out_shape=jax.ShapeDtypeStruct((nb, s, w), BF16),
        grid=(no, s // tm),
        in_specs=[pl.BlockSpec((nb, tm, LANES), lambda o, i: (0, i, cb0 + o)),
                  tile_spec, tile_spec, tile_spec,
                  pl.BlockSpec((1, 1, ns), lambda o, i: (o, 0, 0)),
                  pl.BlockSpec((1, 1, ns), lambda o, i: (o, 0, 0)),
                  pl.BlockSpec((1, LANES), lambda o, i: (0, o))],
        out_specs=pl.BlockSpec((nb, tm, LANES), lambda o, i: (0, i, o)),
        scratch_shapes=[pltpu.VMEM((rows, LANES), F32),
                        pltpu.VMEM((rows // CHUNK * (ns // LANES), LANES), F32),
                        pltpu.VMEM((rows // CHUNK * (ns // LANES), LANES), F32),
                        pltpu.VMEM((rows, LANES), F32),
                        pltpu.VMEM((nb, ns), F32),
                        pltpu.VMEM((CHUNK * LANES, CHUNK * LANES), BF16),
                        pltpu.VMEM((CHUNK * LANES, ns), BF16),
                        pltpu.VMEM((ns, CHUNK * LANES), BF16)],
        compiler_params=_cparams(("arbitrary", "arbitrary")),
        name="s5",
    )(z, lag_t, in_t, out_t, a1, a2, d_skip.reshape(1, w))


def _mixout_kernel(up_ref, halo_ref, ys_ref, gp_ref, gs_ref, x_ref, gt_ref,
                   pw_ref, ps_ref, wpo_ref, wglu_ref, bglu_ref, wout_ref, o_ref, escr):
    tm = up_ref.shape[1]
    i = pl.program_id(1)
    gw = pw_ref.shape[1]
    hal = halo_ref.shape[1]
    d = o_ref.shape[2]
    escr[0:hal, :] = jnp.where(i > 0, halo_ref[0].astype(F32), 0.0)
    escr[hal:hal + tm, :] = up_ref[0].astype(F32)
    pos = (i * tm + lax.broadcasted_iota(I32, (tm, gw), 0) + 1).astype(F32)
    mixed = []
    for g, win in enumerate(POOL_WINDOWS):
        cols = slice(g * gw, (g + 1) * gw)
        cur = escr[hal:hal + tm, cols]
        acc = cur
        for j in range(1, win):
            acc = acc + escr[hal - j:hal - j + tm, cols]
        pooled = acc / jnp.minimum(pos, float(win)) - cur
        mixed.append(jnp.dot(pooled.astype(BF16), pw_ref[g], preferred_element_type=F32))
    mixed = jnp.concatenate(mixed, axis=1) * ps_ref[...]
    y_pool = jnp.dot(mixed.astype(BF16), wpo_ref[...], preferred_element_type=F32)
    glu = jnp.dot(ys_ref[0], wglu_ref[...], preferred_element_type=F32) + bglu_ref[...]
    y_ssm = glu[:, :d] * jax.nn.sigmoid(glu[:, d:])
    merged = (jax.nn.sigmoid(gp_ref[0].astype(F32)) * y_pool
              + jax.nn.sigmoid(gs_ref[0].astype(F32)) * y_ssm)
    mix = jnp.dot(merged.astype(BF16), wout_ref[...], preferred_element_type=F32)
    o_ref[0] = x_ref[0] + gt_ref[0] * mix


def _mixout(z, ys, x, gt, pool_w, pool_scale, w_pool_out, w_glu, b_glu, w_out, tm=256):
    b, s, d = x.shape
    pwid = w_pool_out.shape[0]
    hal = max(POOL_WINDOWS)
    tpb = tm // hal
    gcol = (pwid + ys.shape[2]) // d
    return pl.pallas_call(
        _mixout_kernel,
        out_shape=jax.ShapeDtypeStruct((b, s, d), F32),
        grid=(b, s // tm),
        in_specs=[pl.BlockSpec((1, tm, pwid), lambda bi, i: (bi, i, 0)),
                  pl.BlockSpec((1, hal, pwid), lambda bi, i: (bi, jnp.maximum(i * tpb - 1, 0), 0)),
                  pl.BlockSpec((1, tm, ys.shape[2]), lambda bi, i: (bi, i, 0)),
                  pl.BlockSpec((1, tm, d), lambda bi, i: (bi, i, gcol)),
                  pl.BlockSpec((1, tm, d), lambda bi, i: (bi, i, gcol + 1)),
                  pl.BlockSpec((1, tm, d), lambda bi, i: (bi, i, 0)),
                  pl.BlockSpec((1, 1, d), lambda bi, i: (bi, 0, 0)),
                  _const_spec(pool_w.shape),
                  _const_spec((1, pwid)),
                  _const_spec(w_pool_out.shape),
                  _const_spec(w_glu.shape),
                  _const_spec((1, w_glu.shape[1])),
                  _const_spec(w_out.shape)],
        out_specs=pl.BlockSpec((1, tm, d), lambda bi, i: (bi, i, 0)),
        scratch_shapes=[pltpu.VMEM((hal + tm, pwid), F32)],
        compiler_params=_cparams(("arbitrary", "arbitrary")),
        name="mixout",
    )(z, z, ys, z, z, x, gt.reshape(b, 1, d), pool_w, pool_scale.reshape(1, pwid),
      w_pool_out, w_glu, b_glu.reshape(1, -1), w_out)


def _router_kernel(x_ref, g_ref, sh_ref, sc_ref, wr_ref, br_ref,
                   hp_ref, idx_ref, wt_ref, rank_ref, cnt_ref, carry, *, n_exp):
    tm = x_ref.shape[1]
    first = (pl.program_id(0) == 0) & (pl.program_id(1) == 0)

    @pl.when(first)
    def _():
        carry[...] = jnp.zeros_like(carry)

    h2 = _rms_mod(x_ref[0], g_ref[...], sh_ref[0], sc_ref[0])
    hp_ref[...] = _pack_rows(h2)
    logits = jnp.dot(h2, wr_ref[...], preferred_element_type=F32,
                     precision=lax.Precision.HIGHEST) + br_ref[...]
    lt = logits.T[:n_exp, :]
    eid = lax.broadcasted_iota(I32, lt.shape, 0).astype(F32)
    vals, idxs, hots = [], [], []
    v = lt
    for _ in range(TOP_K):
        m = jnp.max(v, axis=0, keepdims=True)
        sel = jnp.min(jnp.where(v == m, eid, float(n_exp)), axis=0, keepdims=True)
        hot = eid == sel
        vals.append(m)
        idxs.append(sel)
        hots.append(hot)
        v = jnp.where(hot, -jnp.inf, v)
    ex = [jnp.exp(m - vals[0]) for m in vals]
    tot = ex[0] + ex[1] + ex[2] + ex[3]
    msum = sum(h.astype(F32) for h in hots)
    tri = (lax.broadcasted_iota(I32, (tm, tm), 0) < lax.broadcasted_iota(I32, (tm, tm), 1))
    before = jnp.dot(msum.astype(BF16), tri.astype(BF16), preferred_element_type=F32) + carry[:, 0:1]
    for k in range(TOP_K):
        idx_ref[k:k + 1, :] = idxs[k].astype(I32)
        wt_ref[k:k + 1, :] = ex[k] / tot
        rank_ref[k:k + 1, :] = jnp.sum(jnp.where(hots[k], before, 0.0), axis=0,
                                       keepdims=True).astype(I32)
    carry[...] = carry[...] + jnp.sum(msum, axis=1, keepdims=True)
    cnt_ref[...] = carry[...].astype(I32)


def _router(x1, g, sh, sc, w_router, b_router, tm=512):
    b, s, d = x1.shape
    t = b * s
    e = w_router.shape[1]
    wd = d // 2
    wr = jnp.zeros((d, LANES), F32).at[:, :e].set(w_router.astype(F32))
    br = jnp.full((1, LANES), -1e30, F32).at[0, :e].set(b_router.astype(F32))
    spb = s // tm
    return pl.pallas_call(
        functools.partial(_router_kernel, n_exp=e),
        out_shape=[jax.ShapeDtypeStruct((t, wd), U32),
                   jax.ShapeDtypeStruct((TOP_K, t), I32),
                   jax.ShapeDtypeStruct((TOP_K, t), F32),
                   jax.ShapeDtypeStruct((TOP_K, t), I32),
                   jax.ShapeDtypeStruct((e, LANES), I32)],
        grid=(b, spb),
        in_specs=[pl.BlockSpec((1, tm, d), lambda bi, i: (bi, i, 0)),
                  _const_spec((1, d)),
                  pl.BlockSpec((1, 1, d), lambda bi, i: (bi, 0, 0)),
                  pl.BlockSpec((1, 1, d), lambda bi, i: (bi, 0, 0)),
                  _const_spec((d, LANES)),
                  _const_spec((1, LANES))],
        out_specs=[pl.BlockSpec((tm, wd), lambda bi, i: (bi * spb + i, 0)),
                   pl.BlockSpec((TOP_K, tm), lambda bi, i: (0, bi * spb + i)),
                   pl.BlockSpec((TOP_K, tm), lambda bi, i: (0, bi * spb + i)),
                   pl.BlockSpec((TOP_K, tm), lambda bi, i: (0, bi * spb + i)),
                   pl.BlockSpec((e, LANES), lambda bi, i: (0, 0))],
        scratch_shapes=[pltpu.VMEM((e, LANES), F32)],
        compiler_params=_cparams(("arbitrary", "arbitrary")),
        name="router",
    )(x1, g.reshape(1, d), sh.reshape(b, 1, d), sc.reshape(b, 1, d), wr, br)


def _dispatch_kernel(dest_ref, zflag_ref, hp_ref, xs_ref, zbuf, sem, zsem, *, n_tok):
    tm = hp_ref.shape[0]
    i = pl.program_id(0)

    @pl.when(i == 0)
    def _():
        zbuf[...] = jnp.zeros_like(zbuf)

        def zcopy(q):
            r0 = pl.multiple_of(q * SUB, SUB)
            return pltpu.make_async_copy(zbuf, xs_ref.at[pl.ds(r0, SUB)], zsem)

        def zstart(q, carry):
            pl.when(zflag_ref[q] != 0)(lambda: zcopy(q).start())
            return carry

        def zwait(q, carry):
            pl.when(zflag_ref[q] != 0)(lambda: zcopy(q).wait())
            return carry

        lax.fori_loop(0, zflag_ref.shape[0], zstart, 0)
        lax.fori_loop(0, zflag_ref.shape[0], zwait, 0)

    def issue(t, carry):
        for k in range(TOP_K):
            dst = dest_ref[k * n_tok + i * tm + t]
            pltpu.make_async_copy(hp_ref.at[pl.ds(t, 1)], xs_ref.at[pl.ds(dst, 1)], sem).start()
        return carry

    lax.fori_loop(0, tm, issue, 0)
    for k in range(TOP_K):
        pltpu.make_async_copy(hp_ref, xs_ref.at[pl.ds(0, tm)], sem).wait()


def _dispatch(dest_flat, zflag, hp, n_rows, tm=256):
    t, wd = hp.shape
    return pl.pallas_call(
        functools.partial(_dispatch_kernel, n_tok=t),
        out_shape=jax.ShapeDtypeStruct((n_rows, wd), U32),
        grid_spec=pltpu.PrefetchScalarGridSpec(
            num_scalar_prefetch=2,
            grid=(t // tm,),
            in_specs=[pl.BlockSpec((tm, wd), lambda i, *_: (i, 0))],
            out_specs=pl.BlockSpec(memory_space=pl.ANY),
            scratch_shapes=[pltpu.VMEM((SUB, wd), U32),
                            pltpu.SemaphoreType.DMA,
                            pltpu.SemaphoreType.DMA]),
        compiler_params=_cparams(("arbitrary",)),
        name="dispatch",
    )(dest_flat, zflag, hp)


def _experts_kernel(sbe_ref, nsub_ref, nused_ref, x_ref, b1_ref, b2_ref, w1_hbm, w2_hbm, o_ref,
                    xb, actb, yb, wa, wb, sema, semb):
    s = pl.program_id(0)
    n = nsub_ref[s]
    nused = nused_ref[0]
    nj, _, tf = actb.shape
    nn = yb.shape[0]
    tn = 2 * yb.shape[2]
    f = nj * tf
    per = SUPER // SUB

    def a_copies(e, j, slot):
        return [pltpu.make_async_copy(
            w1_hbm.at[e, :, pl.ds(pl.multiple_of(part * f + j * tf, tf), tf)],
            wa.at[slot, part], sema.at[slot]) for part in range(2)]

    def b_copy(e, c, slot):
        return pltpu.make_async_copy(
            w2_hbm.at[e, :, pl.ds(pl.multiple_of(c * tn, tn), tn)], wb.at[slot], semb.at[slot])

    @pl.when((s == 0) & (nused > 0))
    def _():
        for cp in a_copies(sbe_ref[0], 0, 0):
            cp.start()

    def run(v):
        m = v * SUB
        e = sbe_ref[s]
        b_copy(e, 0, 0).start()
        for r in range(v):
            rows = slice(r * SUB, (r + 1) * SUB)
            xb[rows, :] = _unpack_rows(x_ref[rows, :]).astype(BF16)

        def a_body(j, carry):
            slot = lax.rem(j, 2)
            for cp in a_copies(e, j, slot):
                cp.wait()

            @pl.when(j + 1 < nj)
            def _():
                for cp in a_copies(e, j + 1, 1 - slot):
                    cp.start()

            x = xb[0:m, :]
            g = jnp.dot(x, wa[slot, 0].astype(BF16), preferred_element_type=F32) + b1_ref[0, j]
            l = jnp.dot(x, wa[slot, 1].astype(BF16), preferred_element_type=F32) + b1_ref[0, nj + j]
            xg = jnp.minimum(g, SWIGLU_LIMIT)
            xl = jnp.clip(l, -SWIGLU_LIMIT, SWIGLU_LIMIT)
            act = xg * jax.nn.sigmoid(SWIGLU_ALPHA * xg) * (xl + 1.0)
            actb[j, 0:m, :] = act.astype(BF16)
            return carry

        lax.fori_loop(0, nj, a_body, 0)

        @pl.when(s + 1 < nused)
        def _():
            for cp in a_copies(sbe_ref[s + 1], 0, 0):
                cp.start()

        def b_body(c, carry):
            slot = lax.rem(c, 2)
            b_copy(e, c, slot).wait()

            @pl.when(c + 1 < nn)
            def _():
                b_copy(e, c + 1, 1 - slot).start()

            w = wb[slot].astype(BF16)
            y = b2_ref[0, c] + jnp.dot(actb[0, 0:m, :], w[0:tf, :], preferred_element_type=F32)
            for j in range(1, nj):
                y = y + jnp.dot(actb[j, 0:m, :], w[j * tf:(j + 1) * tf, :], preferred_element_type=F32)
            yb[c, 0:m, :] = _pack_rows(y)
            return carry

        lax.fori_loop(0, nn, b_body, 0)

        pw = yb.shape[2]
        for c in range(nn):
            o_ref[0:m, c * pw:(c + 1) * pw] = yb[c, 0:m, :]
        if m < SUPER:
            o_ref[m:SUPER, :] = jnp.zeros((SUPER - m, o_ref.shape[1]), U32)

    for v in range(1, per + 1):
        pl.when((s < nused) & (n == v))(functools.partial(run, v))

    @pl.when(s >= nused)
    def _():
        o_ref[...] = jnp.zeros(o_ref.shape, U32)


def _experts(sbe, nsub, nused, xs, w1, b1, w2, b2, n_super, tf=256):
    n_exp, d, f2 = w1.shape
    f = f2 // 2
    tf = min(tf, f)
    tn = 2 * _pair_width(d)
    nj = f // tf
    nn = d // tn
    wd = xs.shape[1]

    def x_map(s, sbe, nsub, nused):
        return (jnp.minimum(s, nused[0] - 1), 0)

    return pl.pallas_call(
        _experts_kernel,
        out_shape=jax.ShapeDtypeStruct((n_super * SUPER, wd), U32),
        grid_spec=pltpu.PrefetchScalarGridSpec(
            num_scalar_prefetch=3,
            grid=(n_super,),
            in_specs=[pl.BlockSpec((SUPER, wd), x_map),
                      pl.BlockSpec((1, 2 * nj, 1, tf), lambda s, sbe, nsub, nu: (sbe[s], 0, 0, 0)),
                      pl.BlockSpec((1, nn, 1, tn), lambda s, sbe, nsub, nu: (sbe[s], 0, 0, 0)),
                      pl.BlockSpec(memory_space=pl.ANY),
                      pl.BlockSpec(memory_space=pl.ANY)],
            out_specs=pl.BlockSpec((SUPER, wd), lambda s, *_: (s, 0)),
            scratch_shapes=[pltpu.VMEM((SUPER, d), BF16),
                            pltpu.VMEM((nj, SUPER, tf), BF16),
                            pltpu.VMEM((nn, SUPER, tn // 2), U32),
                            pltpu.VMEM((2, 2, d, tf), F32),
                            pltpu.VMEM((2, f, tn), F32),
                            pltpu.SemaphoreType.DMA((2,)),
                            pltpu.SemaphoreType.DMA((2,))]),
        compiler_params=_cparams(("arbitrary",)),
        name="experts",
    )(sbe, nsub, nused, xs, b1.reshape(n_exp, 2 * nj, 1, tf), b2.reshape(n_exp, nn, 1, tn), w1, w2)


def _combine_kernel(dest_ref, x_ref, wt_ref, gt_ref, g_ref, sh_ref, sc_ref, ys_ref, o_ref,
                    gbuf, sem, *, n_tok, tiles_per_batch, apply_norm):
    tm = x_ref.shape[1]
    i = pl.program_id(0) * tiles_per_batch + pl.program_id(1)

    def issue(t, carry):
        for k in range(TOP_K):
            src = dest_ref[k * n_tok + i * tm + t]
            pltpu.make_async_copy(ys_ref.at[pl.ds(src, 1)], gbuf.at[k, pl.ds(t, 1)], sem).start()
        return carry

    lax.fori_loop(0, tm, issue, 0)
    for k in range(TOP_K):
        pltpu.make_async_copy(ys_ref.at[pl.ds(0, tm)], gbuf.at[k], sem).wait()
    wt = wt_ref[...]
    ffn = wt[:, 0:1] * _unpack_rows(gbuf[0])
    for k in range(1, TOP_K):
        ffn = ffn + wt[:, k:k + 1] * _unpack_rows(gbuf[k])
    x2 = x_ref[0] + gt_ref[0] * ffn
    o_ref[0] = _rms_mod(x2, g_ref[...], sh_ref[0], sc_ref[0]) if apply_norm else x2


def _combine(dest_flat, x1, wt_t, gt, g, sh, sc, ys, apply_norm, tm=256):
    b, s, d = x1.shape
    t = b * s
    spb = s // tm
    wd = ys.shape[1]
    return pl.pallas_call(
        functools.partial(_combine_kernel, n_tok=t, tiles_per_batch=spb, apply_norm=apply_norm),
        out_shape=jax.ShapeDtypeStruct((b, s, d), F32),
        grid_spec=pltpu.PrefetchScalarGridSpec(
            num_scalar_prefetch=1,
            grid=(b, spb),
            in_specs=[pl.BlockSpec((1, tm, d), lambda bi, i, *_: (bi, i, 0)),
                      pl.BlockSpec((tm, TOP_K), lambda bi, i, *_: (bi * spb + i, 0)),
                      pl.BlockSpec((1, 1, d), lambda bi, i, *_: (bi, 0, 0)),
                      pl.BlockSpec((1, d), lambda bi, i, *_: (0, 0)),
                      pl.BlockSpec((1, 1, d), lambda bi, i, *_: (bi, 0, 0)),
                      pl.BlockSpec((1, 1, d), lambda bi, i, *_: (bi, 0, 0)),
                      pl.BlockSpec(memory_space=pl.ANY)],
            out_specs=pl.BlockSpec((1, tm, d), lambda bi, i, *_: (bi, i, 0)),
            scratch_shapes=[pltpu.VMEM((TOP_K, tm, wd), U32),
                            pltpu.SemaphoreType.DMA]),
        compiler_params=_cparams(("arbitrary", "arbitrary")),
        name="combine",
    )(dest_flat, x1, wt_t, gt.reshape(b, 1, d), g.reshape(1, d), sh.reshape(b, 1, d),
      sc.reshape(b, 1, d), ys)


def _plan(idx, rank, counts, n_super):
    n_exp = counts.shape[0]
    nsb = (counts + SUPER - 1) // SUPER
    sb_end = jnp.cumsum(nsb)
    sb_start = sb_end - nsb
    hot = idx[..., None] == jnp.arange(n_exp, dtype=I32)
    dest = jnp.sum(jnp.where(hot, sb_start.astype(I32), 0), axis=-1) * SUPER + rank
    s_ids = jnp.arange(n_super, dtype=I32)
    nused = sb_end[-1]
    sbe = jnp.minimum(jnp.searchsorted(sb_end, jnp.minimum(s_ids, nused - 1), side='right'),
                      n_exp - 1).astype(I32)
    per = SUPER // SUB
    q_ids = jnp.arange(n_super * per, dtype=I32)
    q_s = q_ids // per
    q_e = sbe[q_s]
    left = counts[q_e] - ((q_s - sb_start[q_e]) * SUPER + (q_ids % per) * SUB)
    vrows = jnp.where(q_s < nused, jnp.clip(left, 0, SUB), 0)
    nsub = jnp.sum((vrows > 0).reshape(n_super, per), axis=1).astype(I32)
    zflag = (vrows < SUB).astype(I32)
    return dest.reshape(-1).astype(I32), sbe, nsub, nused.reshape(1).astype(I32), zflag


def kernel(x, c, ada_w, ada_b, norm1_g, w_in, pool_w, pool_scale, w_pool_out, ssm_lam_re, ssm_lam_im,
           ssm_log_dt, ssm_b_re, ssm_b_im, ssm_c_re, ssm_c_im, ssm_d, w_glu, b_glu, w_out, norm2_g,
           w_router, b_router, w1, b1, w2, b2, final_ada_w, final_ada_b, final_norm_g):
    b, s, d = x.shape
    t = b * s
    depth = ada_w.shape[0]
    n_exp = w_router.shape[-1]
    pwid = w_pool_out.shape[1]
    n_super = (t * TOP_K) // SUPER + n_exp

    c8 = jnp.zeros((8, d), F32).at[:b].set(c.astype(F32))
    fmod = _ada(c8, final_ada_w, final_ada_b)[:b]
    sh_o, sc_o = jnp.split(fmod, 2, axis=-1)
    for l in range(depth):
        mod = _ada(c8, ada_w[l], ada_b[l])[:b]
        sh_m, sc_m, gt_m, sh_f, sc_f, gt_f = jnp.split(mod, 6, axis=-1)
        z = _inproj(x, norm1_g[l], sh_m, sc_m, w_in[l].astype(BF16))
        ops = _s5_weights(ssm_lam_re[l], ssm_lam_im[l], ssm_log_dt[l], ssm_b_re[l], ssm_b_im[l],
                          ssm_c_re[l], ssm_c_im[l])
        ys = _s5(z, pwid, *ops, ssm_d[l])
        x = _mixout(z, ys, x, gt_m, pool_w[l].astype(BF16), pool_scale[l], w_pool_out[l].astype(BF16),
                    w_glu[l].astype(BF16), b_glu[l], w_out[l].astype(BF16))
        hp, idx, wt, rank, cnt = _router(x, norm2_g[l], sh_f, sc_f, w_router[l], b_router[l])
        dest, sbe, nsub, nused, zflag = _plan(idx, rank, cnt[:, 0], n_super)
        xs = _dispatch(dest, zflag, hp, n_super * SUPER)
        ye = _experts(sbe, nsub, nused, xs, w1[l], b1[l], w2[l], b2[l], n_super)
        last = l == depth - 1
        x = _combine(dest, x, wt.T, gt_f, final_norm_g, sh_o, sc_o, ye, apply_norm=last)
    return x
```

```python
import functools
import math

import jax
import jax.numpy as jnp
from jax import lax
from jax.experimental import pallas as pl
from jax.experimental.pallas import tpu as pltpu

F32 = jnp.float32
BF16 = jnp.bfloat16
I32 = jnp.int32
U32 = jnp.uint32

RMS_EPS = 1e-6
POOL_WINDOWS = (2, 4, 8, 16)
SSM_GROUP_DIM = 16
SSM_STATE = 64
LAMBDA_RE_MAX = -1e-4
TOP_K = 4
SWIGLU_ALPHA = 1.702
SWIGLU_LIMIT = 7.0

LANES = 128
V7X_VMEM_LIMIT = 56 * 1024 * 1024

CHUNK = 16
OCT = LANES // SSM_GROUP_DIM
SUB = 256
SUPER = 4 * SUB


def _cparams(sem, vmem=V7X_VMEM_LIMIT):
    return pltpu.CompilerParams(dimension_semantics=sem, vmem_limit_bytes=vmem)


def _const_spec(shape):
    nd = len(shape)
    return pl.BlockSpec(shape, lambda *_: (0,) * nd, pipeline_mode=pl.Buffered(1))


def _rms_mod(x, g, sh, sc):
    y = x * lax.rsqrt(jnp.mean(x * x, axis=-1, keepdims=True) + RMS_EPS) * g
    return y * (1.0 + sc) + sh


def _pair_width(d):
    return min(2 * LANES, d // 2)


def _pack_rows(h):
    d = h.shape[1]
    pair = _pair_width(d)
    hb = lax.bitcast_convert_type(h.astype(BF16).astype(F32), U32)
    words = []
    for n in range(d // (2 * pair)):
        lo = hb[:, 2 * pair * n:2 * pair * n + pair]
        hi = hb[:, 2 * pair * n + pair:2 * pair * (n + 1)]
        words.append((lo >> 16) | (hi & jnp.uint32(0xFFFF0000)))
    return jnp.concatenate(words, axis=1)


def _unpack_rows(w):
    d = 2 * w.shape[1]
    pair = _pair_width(d)
    cols = []
    for n in range(d // (2 * pair)):
        wn = w[:, pair * n:pair * (n + 1)]
        cols.append(lax.bitcast_convert_type(wn << 16, F32))
        cols.append(lax.bitcast_convert_type(wn & jnp.uint32(0xFFFF0000), F32))
    return jnp.concatenate(cols, axis=1)


def _ada_kernel(c_ref, w_ref, b_ref, o_ref):
    c = c_ref[...]
    ca = c * jax.nn.sigmoid(c)
    o_ref[...] = jnp.dot(ca.astype(BF16), w_ref[...].astype(BF16),
                         preferred_element_type=F32) + b_ref[...]


def _ada(c8, w, b, tn=1024):
    d, n = w.shape
    tn = math.gcd(tn, n)
    return pl.pallas_call(
        _ada_kernel,
        out_shape=jax.ShapeDtypeStruct((c8.shape[0], n), F32),
        grid=(n // tn,),
        in_specs=[pl.BlockSpec(c8.shape, lambda j: (0, 0)),
                  pl.BlockSpec((d, tn), lambda j: (0, j)),
                  pl.BlockSpec((1, tn), lambda j: (0, j))],
        out_specs=pl.BlockSpec((c8.shape[0], tn), lambda j: (0, j)),
        compiler_params=_cparams(("arbitrary",)),
        name="ada",
    )(c8, w, b.reshape(1, n))


def _inproj_kernel(x_ref, g_ref, sh_ref, sc_ref, w_ref, z_ref, *, ncol):
    h = _rms_mod(x_ref[0], g_ref[...], sh_ref[0], sc_ref[0]).astype(BF16)
    n = w_ref.shape[1]
    for c in range(n // ncol):
        z_ref[0, :, c * ncol:(c + 1) * ncol] = jnp.dot(
            h, w_ref[:, c * ncol:(c + 1) * ncol], preferred_element_type=F32).astype(BF16)


def _inproj(x, g, sh, sc, w, tm=512, ncol=1024):
    b, s, d = x.shape
    n = w.shape[1]
    ncol = math.gcd(ncol, n)
    return pl.pallas_call(
        functools.partial(_inproj_kernel, ncol=ncol),
        out_shape=jax.ShapeDtypeStruct((b, s, n), BF16),
        grid=(b, s // tm),
        in_specs=[pl.BlockSpec((1, tm, d), lambda bi, i: (bi, i, 0)),
                  _const_spec((1, d)),
                  pl.BlockSpec((1, 1, d), lambda bi, i: (bi, 0, 0)),
                  pl.BlockSpec((1, 1, d), lambda bi, i: (bi, 0, 0)),
                  _const_spec((d, n))],
        out_specs=pl.BlockSpec((1, tm, n), lambda bi, i: (bi, i, 0)),
        compiler_params=_cparams(("arbitrary", "arbitrary")),
        name="inproj",
    )(x, g.reshape(1, d), sh.reshape(b, 1, d), sc.reshape(b, 1, d), w)


def _s5_weights(lam_re, lam_im, log_dt, b_re, b_im, c_re, c_im):
    hp = lax.Precision.HIGHEST
    g_all, p = lam_re.shape
    h = b_re.shape[-1]
    no = g_all // OCT
    dt = jnp.exp(log_dt.astype(F32))[:, None]
    lre = jnp.minimum(lam_re.astype(F32), LAMBDA_RE_MAX)
    lim = lam_im.astype(F32)
    mag = jnp.exp(lre * dt)
    ang = lim * dt
    ab_re = mag * jnp.cos(ang)
    ab_im = mag * jnp.sin(ang)
    den = lre * lre + lim * lim
    nr = ab_re - 1.0
    ni = ab_im
    f_re = (nr * lre + ni * lim) / den
    f_im = (ni * lre - nr * lim) / den
    br_, bi_ = b_re.astype(F32), b_im.astype(F32)
    bb_re = f_re[..., None] * br_ - f_im[..., None] * bi_
    bb_im = f_re[..., None] * bi_ + f_im[..., None] * br_
    cr, ci = c_re.astype(F32), c_im.astype(F32)
    j = jnp.arange(CHUNK + 1, dtype=F32)[:, None, None]
    pw_mag = jnp.exp(j * (lre * dt)[None])
    pw_re = pw_mag * jnp.cos(j * ang[None])
    pw_im = pw_mag * jnp.sin(j * ang[None])
    cp_re = cr[None] * pw_re[:, :, None, :] - ci[None] * pw_im[:, :, None, :]
    cp_im = cr[None] * pw_im[:, :, None, :] + ci[None] * pw_re[:, :, None, :]
    klag = (jnp.einsum('jghp,gpi->gjhi', cp_re[:CHUNK], bb_re, precision=hp)
            - jnp.einsum('jghp,gpi->gjhi', cp_im[:CHUNK], bb_im, precision=hp))
    assert 2 * p == LANES and h * OCT == LANES
    eye = jnp.eye(OCT, dtype=F32)
    tk = jnp.arange(CHUNK)
    lag_t = jnp.einsum('ogjhi,gq->ojgiqh', klag.reshape(no, OCT, CHUNK, h, h), eye)
    lag_t = lag_t.reshape(no, CHUNK, LANES, LANES)
    pwr = pw_re[CHUNK - 1 - tk]
    pwi = pw_im[CHUNK - 1 - tk]
    s_re = pwr[..., None] * bb_re[None] - pwi[..., None] * bb_im[None]
    s_im = pwr[..., None] * bb_im[None] + pwi[..., None] * bb_re[None]
    s_ri = jnp.stack([s_re, s_im], axis=0).reshape(2, CHUNK, no, OCT, p, h)
    in_t = jnp.transpose(s_ri, (2, 1, 3, 5, 0, 4)).reshape(no, CHUNK, LANES, LANES)
    c_ri = jnp.stack([cp_re[1:], -cp_im[1:]], axis=0).reshape(2, CHUNK, no, OCT, h, p)
    out_t = jnp.transpose(c_ri, (2, 1, 0, 5, 3, 4)).reshape(no, CHUNK, LANES, LANES)
    are = pw_re[CHUNK].reshape(no, OCT, 1, p)
    aim = pw_im[CHUNK].reshape(no, OCT, 1, p)
    a1 = jnp.concatenate([are, are], axis=2).reshape(no, 1, OCT * LANES)
    a2 = jnp.concatenate([-aim, aim], axis=2).reshape(no, 1, OCT * LANES)
    return lag_t.astype(BF16), in_t.astype(BF16), out_t.astype(BF16), a1, a2


def _gelu_tanh(x):
    return 0.5 * x * (1.0 + jnp.tanh(math.sqrt(2.0 / math.pi) * (x + 0.044715 * x * x * x)))


def _s5_kernel(u_ref, lag_ref, in_ref, out_ref, a1_ref, a2_ref, d_ref, y_ref,
               uscr, sscr, pscr, yscr, state, t_op, b_op, c_op):
    nb, tm, _ = u_ref.shape
    nc = tm // CHUNK
    nq = a1_ref.shape[-1] // LANES

    @pl.when(pl.program_id(1) == 0)
    def _():
        state[...] = jnp.zeros_like(state)
        rowg = lax.broadcasted_iota(I32, (LANES, LANES), 0) // SSM_GROUP_DIM
        colg = lax.broadcasted_iota(I32, (LANES, LANES), 1) // SSM_GROUP_DIM
        zero = jnp.zeros((LANES, LANES), BF16)
        for k in range(CHUNK):
            in_k = in_ref[0, k]
            out_k = out_ref[0, k]
            for q in range(nq):
                b_op[k * LANES:(k + 1) * LANES, q * LANES:(q + 1) * LANES] = jnp.where(rowg == q, in_k, zero)
                c_op[q * LANES:(q + 1) * LANES, k * LANES:(k + 1) * LANES] = jnp.where(colg == q, out_k, zero)
            for t in range(CHUNK):
                t_op[k * LANES:(k + 1) * LANES, t * LANES:(t + 1) * LANES] = (
                    lag_ref[0, t - k] if t >= k else zero)

    for bi in range(nb):
        uscr[bi * tm:(bi + 1) * tm, :] = u_ref[bi].astype(F32)
    xk = jnp.concatenate(
        [uscr[pl.ds(k, nb * nc, stride=CHUNK), :] for k in range(CHUNK)], axis=1).astype(BF16)
    rr = nb * nc
    sc_all = jnp.dot(xk, b_op[...], preferred_element_type=F32)
    for q in range(nq):
        sscr[q * rr:(q + 1) * rr, :] = sc_all[:, q * LANES:(q + 1) * LANES]

    a1 = [a1_ref[0, :, q * LANES:(q + 1) * LANES] for q in range(nq)]
    a2 = [a2_ref[0, :, q * LANES:(q + 1) * LANES] for q in range(nq)]
    st = [state[:, q * LANES:(q + 1) * LANES] for q in range(nq)]
    for c in range(nc):
        for q in range(nq):
            pscr[pl.ds(q * rr + c, nb, stride=nc), :] = st[q]
            sc = sscr[pl.ds(q * rr + c, nb, stride=nc), :]
            st[q] = a1[q] * st[q] + a2[q] * pltpu.roll(st[q], LANES // 2, axis=1) + sc
    for q in range(nq):
        state[:, q * LANES:(q + 1) * LANES] = st[q]

    xprev = jnp.concatenate([pscr[q * rr:(q + 1) * rr, :] for q in range(nq)], axis=1)
    y = (jnp.dot(xk, t_op[...], preferred_element_type=F32)
         + jnp.dot(xprev.astype(BF16), c_op[...], preferred_element_type=F32))
    for t in range(CHUNK):
        yscr[pl.ds(t, nb * nc, stride=CHUNK), :] = y[:, t * LANES:(t + 1) * LANES]
    out = _gelu_tanh(yscr[...] + d_ref[...] * uscr[...])
    for bi in range(nb):
        y_ref[bi] = out[bi * tm:(bi + 1) * tm, :].astype(BF16)


def _s5(z, col0, lag_t, in_t, out_t, a1, a2, d_skip, tm=1024):
    nb, s, _ = z.shape
    no = lag_t.shape[0]
    w = no * LANES
    ns = a1.shape[-1]
    tm = min(tm, s)
    rows = nb * tm
    cb0 = col0 // LANES
    tile_spec = pl.BlockSpec((1, CHUNK, LANES, LANES), lambda o, i: (o, 0, 0, 0))
    return pl.pallas_call(
        _s5_kernel,
        out_shape=jax.ShapeDtypeStruct((nb, s, w), BF16),
        grid=(no, s // tm),
        in_specs=[pl.BlockSpec((nb, tm, LANES), lambda o, i: (0, i, cb0 + o)),
                  tile_spec, tile_spec, tile_spec,
                  pl.BlockSpec((1, 1, ns), lambda o, i: (o, 0, 0)),
                  pl.BlockSpec((1, 1, ns), lambda o, i: (o, 0, 0)),
                  pl.BlockSpec((1, LANES), lambda o, i: (0, o))],
        out_specs=pl.BlockSpec((nb, tm, LANES), lambda o, i: (0, i, o)),
        scratch_shapes=[pltpu.VMEM((rows, LANES), F32),
                        pltpu.VMEM((rows // CHUNK * (ns // LANES), LANES), F32),
                        pltpu.VMEM((rows // CHUNK * (ns // LANES), LANES), F32),
                        pltpu.VMEM((rows, LANES), F32),
                        pltpu.VMEM((nb, ns), F32),
                        pltpu.VMEM((CHUNK * LANES, CHUNK * LANES), BF16),
                        pltpu.VMEM((CHUNK * LANES, ns), BF16),
                        pltpu.VMEM((ns, CHUNK * LANES), BF16)],
        compiler_params=_cparams(("arbitrary", "arbitrary")),
        name="s5",
    )(z, lag_t, in_t, out_t, a1, a2, d_skip.reshape(1, w))


def _mixout_kernel(up_ref, halo_ref, ys_ref, gp_ref, gs_ref, x_ref, gt_ref,
                   pw_ref, ps_ref, wpo_ref, wglu_ref, bglu_ref, wout_ref, o_ref, escr):
    tm = up_ref.shape[1]
    i = pl.program_id(1)
    gw = pw_ref.shape[1]
    hal = halo_ref.shape[1]
    d = o_ref.shape[2]
    escr[0:hal, :] = jnp.where(i > 0, halo_ref[0].astype(F32), 0.0)
    escr[hal:hal + tm, :] = up_ref[0].astype(F32)
    pos = (i * tm + lax.broadcasted_iota(I32, (tm, gw), 0) + 1).astype(F32)
    mixed = []
    for g, win in enumerate(POOL_WINDOWS):
        cols = slice(g * gw, (g + 1) * gw)
        cur = escr[hal:hal + tm, cols]
        acc = cur
        for j in range(1, win):
            acc = acc + escr[hal - j:hal - j + tm, cols]
        pooled = acc / jnp.minimum(pos, float(win)) - cur
        mixed.append(jnp.dot(pooled.astype(BF16), pw_ref[g], preferred_element_type=F32))
    mixed = jnp.concatenate(mixed, axis=1) * ps_ref[...]
    y_pool = jnp.dot(mixed.astype(BF16), wpo_ref[...], preferred_element_type=F32)
    glu = jnp.dot(ys_ref[0], wglu_ref[...], preferred_element_type=F32) + bglu_ref[...]
    y_ssm = glu[:, :d] * jax.nn.sigmoid(glu[:, d:])
    merged = (jax.nn.sigmoid(gp_ref[0].astype(F32)) * y_pool
              + jax.nn.sigmoid(gs_ref[0].astype(F32)) * y_ssm)
    mix = jnp.dot(merged.astype(BF16), wout_ref[...], preferred_element_type=F32)
    o_ref[0] = x_ref[0] + gt_ref[0] * mix


def _mixout(z, ys, x, gt, pool_w, pool_scale, w_pool_out, w_glu, b_glu, w_out, tm=256):
    b, s, d = x.shape
    pwid = w_pool_out.shape[0]
    hal = max(POOL_WINDOWS)
    tpb = tm // hal
    gcol = (pwid + ys.shape[2]) // d
    return pl.pallas_call(
        _mixout_kernel,
        out_shape=jax.ShapeDtypeStruct((b, s, d), F32),
        grid=(b, s // tm),
        in_specs=[pl.BlockSpec((1, tm, pwid), lambda bi, i: (bi, i, 0)),
                  pl.BlockSpec((1, hal, pwid), lambda bi, i: (bi, jnp.maximum(i * tpb - 1, 0), 0)),
                  pl.BlockSpec((1, tm, ys.shape[2]), lambda bi, i: (bi, i, 0)),
                  pl.BlockSpec((1, tm, d), lambda bi, i: (bi, i, gcol)),
                  pl.BlockSpec((1, tm, d), lambda bi, i: (bi, i, gcol + 1)),
                  pl.BlockSpec((1, tm, d), lambda bi, i: (bi, i, 0)),
                  pl.BlockSpec((1, 1, d), lambda bi, i: (bi, 0, 0)),
                  _const_spec(pool_w.shape),
                  _const_spec((1, pwid)),
                  _const_spec(w_pool_out.shape),
                  _const_spec(w_glu.shape),
                  _const_spec((1, w_glu.shape[1])),
                  _const_spec(w_out.shape)],
        out_specs=pl.BlockSpec((1, tm, d), lambda bi, i: (bi, i, 0)),
        scratch_shapes=[pltpu.VMEM((hal + tm, pwid), F32)],
        compiler_params=_cparams(("arbitrary", "arbitrary")),
        name="mixout",
    )(z, z, ys, z, z, x, gt.reshape(b, 1, d), pool_w, pool_scale.reshape(1, pwid),
      w_pool_out, w_glu, b_glu.reshape(1, -1), w_out)


def _router_kernel(x_ref, g_ref, sh_ref, sc_ref, wr_ref, br_ref,
                   hp_ref, idx_ref, wt_ref, rank_ref, cnt_ref, carry, *, n_exp):
    tm = x_ref.shape[1]
    first = (pl.program_id(0) == 0) & (pl.program_id(1) == 0)

    @pl.when(first)
    def _():
        carry[...] = jnp.zeros_like(carry)

    h2 = _rms_mod(x_ref[0], g_ref[...], sh_ref[0], sc_ref[0])
    hp_ref[...] = _pack_rows(h2)
    logits = jnp.dot(h2, wr_ref[...], preferred_element_type=F32,
                     precision=lax.Precision.HIGHEST) + br_ref[...]
    lt = logits.T[:n_exp, :]
    eid = lax.broadcasted_iota(I32, lt.shape, 0).astype(F32)
    vals, idxs, hots = [], [], []
    v = lt
    for _ in range(TOP_K):
        m = jnp.max(v, axis=0, keepdims=True)
        sel = jnp.min(jnp.where(v == m, eid, float(n_exp)), axis=0, keepdims=True)
        hot = eid == sel
        vals.append(m)
        idxs.append(sel)
        hots.append(hot)
        v = jnp.where(hot, -jnp.inf, v)
    ex = [jnp.exp(m - vals[0]) for m in vals]
    tot = ex[0] + ex[1] + ex[2] + ex[3]
    msum = sum(h.astype(F32) for h in hots)
    tri = (lax.broadcasted_iota(I32, (tm, tm), 0) < lax.broadcasted_iota(I32, (tm, tm), 1))
    before = jnp.dot(msum.astype(BF16), tri.astype(BF16), preferred_element_type=F32) + carry[:, 0:1]
    for k in range(TOP_K):
        idx_ref[k:k + 1, :] = idxs[k].astype(I32)
        wt_ref[k:k + 1, :] = ex[k] / tot
        rank_ref[k:k + 1, :] = jnp.sum(jnp.where(hots[k], before, 0.0), axis=0,
                                       keepdims=True).astype(I32)
    carry[...] = carry[...] + jnp.sum(msum, axis=1, keepdims=True)
    cnt_ref[...] = carry[...].astype(I32)


def _router(x1, g, sh, sc, w_router, b_router, tm=512):
    b, s, d = x1.shape
    t = b * s
    e = w_router.shape[1]
    wd = d // 2
    wr = jnp.zeros((d, LANES), F32).at[:, :e].set(w_router.astype(F32))
    br = jnp.full((1, LANES), -1e30, F32).at[0, :e].set(b_router.astype(F32))
    spb = s // tm
    return pl.pallas_call(
        functools.partial(_router_kernel, n_exp=e),
        out_shape=[jax.ShapeDtypeStruct((t, wd), U32),
                   jax.ShapeDtypeStruct((TOP_K, t), I32),
                   jax.ShapeDtypeStruct((TOP_K, t), F32),
                   jax.ShapeDtypeStruct((TOP_K, t), I32),
                   jax.ShapeDtypeStruct((e, LANES), I32)],
        grid=(b, spb),
        in_specs=[pl.BlockSpec((1, tm, d), lambda bi, i: (bi, i, 0)),
                  _const_spec((1, d)),
                  pl.BlockSpec((1, 1, d), lambda bi, i: (bi, 0, 0)),
                  pl.BlockSpec((1, 1, d), lambda bi, i: (bi, 0, 0)),
                  _const_spec((d, LANES)),
                  _const_spec((1, LANES))],
        out_specs=[pl.BlockSpec((tm, wd), lambda bi, i: (bi * spb + i, 0)),
                   pl.BlockSpec((TOP_K, tm), lambda bi, i: (0, bi * spb + i)),
                   pl.BlockSpec((TOP_K, tm), lambda bi, i: (0, bi * spb + i)),
                   pl.BlockSpec((TOP_K, tm), lambda bi, i: (0, bi * spb + i)),
                   pl.BlockSpec((e, LANES), lambda bi, i: (0, 0))],
        scratch_shapes=[pltpu.VMEM((e, LANES), F32)],
        compiler_params=_cparams(("arbitrary", "arbitrary")),
        name="router",
    )(x1, g.reshape(1, d), sh.reshape(b, 1, d), sc.reshape(b, 1, d), wr, br)


def _dispatch_kernel(dest_ref, zflag_ref, hp_ref, xs_ref, zbuf, sem, zsem, *, n_tok):
    tm = hp_ref.shape[0]
    i = pl.program_id(0)

    @pl.when(i == 0)
    def _():
        zbuf[...] = jnp.zeros_like(zbuf)

        def zcopy(q):
            r0 = pl.multiple_of(q * SUB, SUB)
            return pltpu.make_async_copy(zbuf, xs_ref.at[pl.ds(r0, SUB)], zsem)

        def zstart(q, carry):
            pl.when(zflag_ref[q] != 0)(lambda: zcopy(q).start())
            return carry

        def zwait(q, carry):
            pl.when(zflag_ref[q] != 0)(lambda: zcopy(q).wait())
            return carry

        lax.fori_loop(0, zflag_ref.shape[0], zstart, 0)
        lax.fori_loop(0, zflag_ref.shape[0], zwait, 0)

    def issue(t, carry):
        for k in range(TOP_K):
            dst = dest_ref[k * n_tok + i * tm + t]
            pltpu.make_async_copy(hp_ref.at[pl.ds(t, 1)], xs_ref.at[pl.ds(dst, 1)], sem).start()
        return carry

    lax.fori_loop(0, tm, issue, 0)
    for k in range(TOP_K):
        pltpu.make_async_copy(hp_ref, xs_ref.at[pl.ds(0, tm)], sem).wait()


def _dispatch(dest_flat, zflag, hp, n_rows, tm=256):
    t, wd = hp.shape
    return pl.pallas_call(
        functools.partial(_dispatch_kernel, n_tok=t),
        out_shape=jax.ShapeDtypeStruct((n_rows, wd), U32),
        grid_spec=pltpu.PrefetchScalarGridSpec(
            num_scalar_prefetch=2,
            grid=(t // tm,),
            in_specs=[pl.BlockSpec((tm, wd), lambda i, *_: (i, 0))],
            out_specs=pl.BlockSpec(memory_space=pl.ANY),
            scratch_shapes=[pltpu.VMEM((SUB, wd), U32),
                            pltpu.SemaphoreType.DMA,
                            pltpu.SemaphoreType.DMA]),
        compiler_params=_cparams(("arbitrary",)),
        name="dispatch",
    )(dest_flat, zflag, hp)


def _experts_kernel(sbe_ref, nsub_ref, nused_ref, x_ref, b1_ref, b2_ref, w1_hbm, w2_hbm, o_ref,
                    xb, actb, yb, wt, sem):
    s = pl.program_id(0)
    n = nsub_ref[s]
    nused = nused_ref[0]
    nj, _, tf = actb.shape
    nn, _, pair = yb.shape
    d = xb.shape[1]
    f = nj * tf
    nslot = wt.shape[0]
    ntile = nj + nn
    per = SUPER // SUB

    def a_copies(e, j, slot):
        return [pltpu.make_async_copy(
            w1_hbm.at[e, :, pl.ds(pl.multiple_of(part * f + j * tf, tf), tf)],
            wt.at[slot, part, 0:d, 0:tf], sem.at[slot]) for part in range(2)]

    def b_copies(e, c, slot):
        return [pltpu.make_async_copy(
            w2_hbm.at[e, :, pl.ds(pl.multiple_of((2 * c + part) * pair, pair), pair)],
            wt.at[slot, part, 0:f, 0:pair], sem.at[slot]) for part in range(2)]

    def slot_of(sb, q):
        return lax.rem(sb * ntile + q, nslot)

    def start_tile(sb, q):
        e = sbe_ref[sb]
        slot = slot_of(sb, q)

        @pl.when(q < nj)
        def _():
            for cp in a_copies(e, q, slot):
                cp.start()

        @pl.when(q >= nj)
        def _():
            for cp in b_copies(e, q - nj, slot):
                cp.start()

    def prefetch(q):
        qq = q + nslot - 1

        @pl.when(qq < ntile)
        def _():
            start_tile(s, qq)

        @pl.when((qq >= ntile) & (s + 1 < nused))
        def _():
            start_tile(s + 1, qq - ntile)

    @pl.when((s == 0) & (nused > 0))
    def _():
        for q in range(nslot - 1):
            start_tile(0, jnp.int32(q))

    def run(v):
        m = v * SUB
        e = sbe_ref[s]
        for r in range(v):
            rows = slice(r * SUB, (r + 1) * SUB)
            xb[rows, :] = _unpack_rows(x_ref[rows, :]).astype(BF16)

        def a_body(j, carry):
            slot = slot_of(s, j)
            for cp in a_copies(e, j, slot):
                cp.wait()
            prefetch(j)
            x = xb[0:m, :]
            g = jnp.dot(x, wt[slot, 0, 0:d, 0:tf].astype(BF16), preferred_element_type=F32) + b1_ref[0, j]
            l = jnp.dot(x, wt[slot, 1, 0:d, 0:tf].astype(BF16), preferred_element_type=F32) + b1_ref[0, nj + j]
            xg = jnp.minimum(g, SWIGLU_LIMIT)
            xl = jnp.clip(l, -SWIGLU_LIMIT, SWIGLU_LIMIT)
            act = xg * jax.nn.sigmoid(SWIGLU_ALPHA * xg) * (xl + 1.0)
            actb[j, 0:m, :] = act.astype(BF16)
            return carry

        lax.fori_loop(0, nj, a_body, 0)

        def b_body(c, carry):
            slot = slot_of(s, nj + c)
            for cp in b_copies(e, c, slot):
                cp.wait()
            prefetch(nj + c)
            halves = []
            for part in range(2):
                w = wt[slot, part, 0:f, 0:pair].astype(BF16)
                y = b2_ref[0, 2 * c + part] + jnp.dot(actb[0, 0:m, :], w[0:tf, :],
                                                      preferred_element_type=F32)
                for j in range(1, nj):
                    y = y + jnp.dot(actb[j, 0:m, :], w[j * tf:(j + 1) * tf, :],
                                    preferred_element_type=F32)
                halves.append(y)
            yb[c, 0:m, :] = _pack_rows(jnp.concatenate(halves, axis=1))
            return carry

        lax.fori_loop(0, nn, b_body, 0)

        for c in range(nn):
            o_ref[0:m, c * pair:(c + 1) * pair] = yb[c, 0:m, :]
        if m < SUPER:
            o_ref[m:SUPER, :] = jnp.zeros((SUPER - m, o_ref.shape[1]), U32)

    for v in range(1, per + 1):
        pl.when((s < nused) & (n == v))(functools.partial(run, v))

    @pl.when(s >= nused)
    def _():
        o_ref[...] = jnp.zeros(o_ref.shape, U32)


def _experts(sbe, nsub, nused, xs, w1, b1, w2, b2, n_super, tf=256, nslot=3):
    n_exp, d, f2 = w1.shape
    f = f2 // 2
    tf = min(tf, f)
    pair = _pair_width(d)
    nj = f // tf
    nn = d // (2 * pair)
    wd = xs.shape[1]

    def x_map(s, sbe, nsub, nused):
        return (jnp.minimum(s, nused[0] - 1), 0)

    return pl.pallas_call(
        _experts_kernel,
        out_shape=jax.ShapeDtypeStruct((n_super * SUPER, wd), U32),
        grid_spec=pltpu.PrefetchScalarGridSpec(
            num_scalar_prefetch=3,
            grid=(n_super,),
            in_specs=[pl.BlockSpec((SUPER, wd), x_map),
                      pl.BlockSpec((1, 2 * nj, 1, tf), lambda s, sbe, nsub, nu: (sbe[s], 0, 0, 0)),
                      pl.BlockSpec((1, 2 * nn, 1, pair), lambda s, sbe, nsub, nu: (sbe[s], 0, 0, 0)),
                      pl.BlockSpec(memory_space=pl.ANY),
                      pl.BlockSpec(memory_space=pl.ANY)],
            out_specs=pl.BlockSpec((SUPER, wd), lambda s, *_: (s, 0)),
            scratch_shapes=[pltpu.VMEM((SUPER, d), BF16),
                            pltpu.VMEM((nj, SUPER, tf), BF16),
                            pltpu.VMEM((nn, SUPER, pair), U32),
                            pltpu.VMEM((nslot, 2, max(d, f), max(tf, pair)), F32),
                            pltpu.SemaphoreType.DMA((nslot,))]),
        compiler_params=_cparams(("arbitrary",)),
        name="experts",
    )(sbe, nsub, nused, xs, b1.reshape(n_exp, 2 * nj, 1, tf), b2.reshape(n_exp, 2 * nn, 1, pair), w1, w2)


def _combine_kernel(dest_ref, x_ref, wt_ref, gt_ref, g_ref, sh_ref, sc_ref, ys_ref, o_ref,
                    gbuf, sem, *, n_tok, tiles_per_batch, apply_norm):
    tm = x_ref.shape[1]
    i = pl.program_id(0) * tiles_per_batch + pl.program_id(1)

    def issue(t, carry):
        for k in range(TOP_K):
            src = dest_ref[k * n_tok + i * tm + t]
            pltpu.make_async_copy(ys_ref.at[pl.ds(src, 1)], gbuf.at[k, pl.ds(t, 1)], sem).start()
        return carry

    lax.fori_loop(0, tm, issue, 0)
    for k in range(TOP_K):
        pltpu.make_async_copy(ys_ref.at[pl.ds(0, tm)], gbuf.at[k], sem).wait()
    wt = wt_ref[...]
    ffn = wt[:, 0:1] * _unpack_rows(gbuf[0])
    for k in range(1, TOP_K):
        ffn = ffn + wt[:, k:k + 1] * _unpack_rows(gbuf[k])
    x2 = x_ref[0] + gt_ref[0] * ffn
    o_ref[0] = _rms_mod(x2, g_ref[...], sh_ref[0], sc_ref[0]) if apply_norm else x2


def _combine(dest_flat, x1, wt_t, gt, g, sh, sc, ys, apply_norm, tm=256):
    b, s, d = x1.shape
    t = b * s
    spb = s // tm
    wd = ys.shape[1]
    return pl.pallas_call(
        functools.partial(_combine_kernel, n_tok=t, tiles_per_batch=spb, apply_norm=apply_norm),
        out_shape=jax.ShapeDtypeStruct((b, s, d), F32),
        grid_spec=pltpu.PrefetchScalarGridSpec(
            num_scalar_prefetch=1,
            grid=(b, spb),
            in_specs=[pl.BlockSpec((1, tm, d), lambda bi, i, *_: (bi, i, 0)),
                      pl.BlockSpec((tm, TOP_K), lambda bi, i, *_: (bi * spb + i, 0)),
                      pl.BlockSpec((1, 1, d), lambda bi, i, *_: (bi, 0, 0)),
                      pl.BlockSpec((1, d), lambda bi, i, *_: (0, 0)),
                      pl.BlockSpec((1, 1, d), lambda bi, i, *_: (bi, 0, 0)),
                      pl.BlockSpec((1, 1, d), lambda bi, i, *_: (bi, 0, 0)),
                      pl.BlockSpec(memory_space=pl.ANY)],
            out_specs=pl.BlockSpec((1, tm, d), lambda bi, i, *_: (bi, i, 0)),
            scratch_shapes=[pltpu.VMEM((TOP_K, tm, wd), U32),
                            pltpu.SemaphoreType.DMA]),
        compiler_params=_cparams(("arbitrary", "arbitrary")),
        name="combine",
    )(dest_flat, x1, wt_t, gt.reshape(b, 1, d), g.reshape(1, d), sh.reshape(b, 1, d),
      sc.reshape(b, 1, d), ys)


def _plan(idx, rank, counts, n_super):
    n_exp = counts.shape[0]
    nsb = (counts + SUPER - 1) // SUPER
    rps = (counts + jnp.maximum(nsb, 1) - 1) // jnp.maximum(nsb, 1)
    rps = jnp.maximum(rps, 1).astype(I32)
    sb_end = jnp.cumsum(nsb)
    sb_start = sb_end - nsb
    hot = idx[..., None] == jnp.arange(n_exp, dtype=I32)
    t_start = jnp.sum(jnp.where(hot, sb_start.astype(I32), 0), axis=-1)
    t_rps = jnp.sum(jnp.where(hot, rps, 0), axis=-1)
    dest = (t_start + rank // t_rps) * SUPER + rank % t_rps
    s_ids = jnp.arange(n_super, dtype=I32)
    nused = sb_end[-1]
    sbe = jnp.minimum(jnp.searchsorted(sb_end, jnp.minimum(s_ids, nused - 1), side='right'),
                      n_exp - 1).astype(I32)
    per = SUPER // SUB
    q_ids = jnp.arange(n_super * per, dtype=I32)
    q_s = q_ids // per
    q_e = sbe[q_s]
    in_sb = jnp.minimum(rps[q_e], counts[q_e] - (q_s - sb_start[q_e]) * rps[q_e])
    left = in_sb - (q_ids % per) * SUB
    vrows = jnp.where(q_s < nused, jnp.clip(left, 0, SUB), 0)
    nsub = jnp.sum((vrows > 0).reshape(n_super, per), axis=1).astype(I32)
    zflag = (vrows < SUB).astype(I32)
    return dest.reshape(-1).astype(I32), sbe, nsub, nused.reshape(1).astype(I32), zflag


def kernel(x, c, ada_w, ada_b, norm1_g, w_in, pool_w, pool_scale, w_pool_out, ssm_lam_re, ssm_lam_im,
           ssm_log_dt, ssm_b_re, ssm_b_im, ssm_c_re, ssm_c_im, ssm_d, w_glu, b_glu, w_out, norm2_g,
           w_router, b_router, w1, b1, w2, b2, final_ada_w, final_ada_b, final_norm_g):
    b, s, d = x.shape
    t = b * s
    depth = ada_w.shape[0]
    n_exp = w_router.shape[-1]
    pwid = w_pool_out.shape[1]
    n_super = (t * TOP_K) // SUPER + n_exp

    c8 = jnp.zeros((8, d), F32).at[:b].set(c.astype(F32))
    fmod = _ada(c8, final_ada_w, final_ada_b)[:b]
    sh_o, sc_o = jnp.split(fmod, 2, axis=-1)
    for l in range(depth):
        mod = _ada(c8, ada_w[l], ada_b[l])[:b]
        sh_m, sc_m, gt_m, sh_f, sc_f, gt_f = jnp.split(mod, 6, axis=-1)
        z = _inproj(x, norm1_g[l], sh_m, sc_m, w_in[l].astype(BF16))
        ops = _s5_weights(ssm_lam_re[l], ssm_lam_im[l], ssm_log_dt[l], ssm_b_re[l], ssm_b_im[l],
                          ssm_c_re[l], ssm_c_im[l])
        ys = _s5(z, pwid, *ops, ssm_d[l])
        x = _mixout(z, ys, x, gt_m, pool_w[l].astype(BF16), pool_scale[l], w_pool_out[l].astype(BF16),
                    w_glu[l].astype(BF16), b_glu[l], w_out[l].astype(BF16))
        hp, idx, wt, rank, cnt = _router(x, norm2_g[l], sh_f, sc_f, w_router[l], b_router[l])
        dest, sbe, nsub, nused, zflag = _plan(idx, rank, cnt[:, 0], n_super)
        xs = _dispatch(dest, zflag, hp, n_super * SUPER)
        ye = _experts(sbe, nsub, nused, xs, w1[l], b1[l], w2[l], b2[l], n_super)
        last = l == depth - 1
        x = _combine(dest, x, wt.T, gt_f, final_norm_g, sh_o, sc_o, ye, apply_norm=last)
    return x
```

```python
import functools
import math

import jax
import jax.numpy as jnp
from jax import lax
from jax.experimental import pallas as pl
from jax.experimental.pallas import tpu as pltpu

F32 = jnp.float32
BF16 = jnp.bfloat16
I32 = jnp.int32
U32 = jnp.uint32

RMS_EPS = 1e-6
POOL_WINDOWS = (2, 4, 8, 16)
SSM_GROUP_DIM = 16
SSM_STATE = 64
LAMBDA_RE_MAX = -1e-4
TOP_K = 4
SWIGLU_ALPHA = 1.702
SWIGLU_LIMIT = 7.0

LANES = 128
V7X_VMEM_LIMIT = 56 * 1024 * 1024

CHUNK = 16
OCT = LANES // SSM_GROUP_DIM
SUB = 256
SUPER = 4 * SUB


def _cparams(sem, vmem=V7X_VMEM_LIMIT):
    return pltpu.CompilerParams(dimension_semantics=sem, vmem_limit_bytes=vmem)


def _const_spec(shape):
    nd = len(shape)
    return pl.BlockSpec(shape, lambda *_: (0,) * nd, pipeline_mode=pl.Buffered(1))


def _rms_mod(x, g, sh, sc):
    y = x * lax.rsqrt(jnp.mean(x * x, axis=-1, keepdims=True) + RMS_EPS) * g
    return y * (1.0 + sc) + sh


def _pair_width(d):
    return min(2 * LANES, d // 2)


def _pack_rows(h):
    d = h.shape[1]
    pair = _pair_width(d)
    hb = lax.bitcast_convert_type(h.astype(BF16).astype(F32), U32)
    words = []
    for n in range(d // (2 * pair)):
        lo = hb[:, 2 * pair * n:2 * pair * n + pair]
        hi = hb[:, 2 * pair * n + pair:2 * pair * (n + 1)]
        words.append((lo >> 16) | (hi & jnp.uint32(0xFFFF0000)))
    return jnp.concatenate(words, axis=1)


def _unpack_rows(w):
    d = 2 * w.shape[1]
    pair = _pair_width(d)
    cols = []
    for n in range(d // (2 * pair)):
        wn = w[:, pair * n:pair * (n + 1)]
        cols.append(lax.bitcast_convert_type(wn << 16, F32))
        cols.append(lax.bitcast_convert_type(wn & jnp.uint32(0xFFFF0000), F32))
    return jnp.concatenate(cols, axis=1)


def _ada_kernel(c_ref, w_ref, b_ref, o_ref):
    c = c_ref[...]
    ca = c * jax.nn.sigmoid(c)
    o_ref[...] = jnp.dot(ca.astype(BF16), w_ref[...].astype(BF16),
                         preferred_element_type=F32) + b_ref[...]


def _ada(c8, w, b, tn=1024):
    d, n = w.shape
    tn = math.gcd(tn, n)
    return pl.pallas_call(
        _ada_kernel,
        out_shape=jax.ShapeDtypeStruct((c8.shape[0], n), F32),
        grid=(n // tn,),
        in_specs=[pl.BlockSpec(c8.shape, lambda j: (0, 0)),
                  pl.BlockSpec((d, tn), lambda j: (0, j)),
                  pl.BlockSpec((1, tn), lambda j: (0, j))],
        out_specs=pl.BlockSpec((c8.shape[0], tn), lambda j: (0, j)),
        compiler_params=_cparams(("arbitrary",)),
        name="ada",
    )(c8, w, b.reshape(1, n))


def _inproj_kernel(x_ref, g_ref, sh_ref, sc_ref, w_ref, z_ref, *, ncol):
    h = _rms_mod(x_ref[0], g_ref[...], sh_ref[0], sc_ref[0]).astype(BF16)
    n = w_ref.shape[1]
    for c in range(n // ncol):
        z_ref[0, :, c * ncol:(c + 1) * ncol] = jnp.dot(
            h, w_ref[:, c * ncol:(c + 1) * ncol], preferred_element_type=F32).astype(BF16)


def _inproj(x, g, sh, sc, w, tm=512, ncol=1024):
    b, s, d = x.shape
    n = w.shape[1]
    ncol = math.gcd(ncol, n)
    return pl.pallas_call(
        functools.partial(_inproj_kernel, ncol=ncol),
        out_shape=jax.ShapeDtypeStruct((b, s, n), BF16),
        grid=(b, s // tm),
        in_specs=[pl.BlockSpec((1, tm, d), lambda bi, i: (bi, i, 0)),
                  _const_spec((1, d)),
                  pl.BlockSpec((1, 1, d), lambda bi, i: (bi, 0, 0)),
                  pl.BlockSpec((1, 1, d), lambda bi, i: (bi, 0, 0)),
                  _const_spec((d, n))],
        out_specs=pl.BlockSpec((1, tm, n), lambda bi, i: (bi, i, 0)),
        compiler_params=_cparams(("arbitrary", "arbitrary")),
        name="inproj",
    )(x, g.reshape(1, d), sh.reshape(b, 1, d), sc.reshape(b, 1, d), w)


def _s5_weights(lam_re, lam_im, log_dt, b_re, b_im, c_re, c_im):
    hp = lax.Precision.HIGHEST
    g_all, p = lam_re.shape
    h = b_re.shape[-1]
    no = g_all // OCT
    dt = jnp.exp(log_dt.astype(F32))[:, None]
    lre = jnp.minimum(lam_re.astype(F32), LAMBDA_RE_MAX)
    lim = lam_im.astype(F32)
    mag = jnp.exp(lre * dt)
    ang = lim * dt
    ab_re = mag * jnp.cos(ang)
    ab_im = mag * jnp.sin(ang)
    den = lre * lre + lim * lim
    nr = ab_re - 1.0
    ni = ab_im
    f_re = (nr * lre + ni * lim) / den
    f_im = (ni * lre - nr * lim) / den
    br_, bi_ = b_re.astype(F32), b_im.astype(F32)
    bb_re = f_re[..., None] * br_ - f_im[..., None] * bi_
    bb_im = f_re[..., None] * bi_ + f_im[..., None] * br_
    cr, ci = c_re.astype(F32), c_im.astype(F32)
    j = jnp.arange(CHUNK + 1, dtype=F32)[:, None, None]
    pw_mag = jnp.exp(j * (lre * dt)[None])
    pw_re = pw_mag * jnp.cos(j * ang[None])
    pw_im = pw_mag * jnp.sin(j * ang[None])
    cp_re = cr[None] * pw_re[:, :, None, :] - ci[None] * pw_im[:, :, None, :]
    cp_im = cr[None] * pw_im[:, :, None, :] + ci[None] * pw_re[:, :, None, :]
    klag = (jnp.einsum('jghp,gpi->gjhi', cp_re[:CHUNK], bb_re, precision=hp)
            - jnp.einsum('jghp,gpi->gjhi', cp_im[:CHUNK], bb_im, precision=hp))
    assert 2 * p == LANES and h * OCT == LANES
    eye = jnp.eye(OCT, dtype=F32)
    tk = jnp.arange(CHUNK)
    lag_t = jnp.einsum('ogjhi,gq->ojgiqh', klag.reshape(no, OCT, CHUNK, h, h), eye)
    lag_t = lag_t.reshape(no, CHUNK, LANES, LANES)
    pwr = pw_re[CHUNK - 1 - tk]
    pwi = pw_im[CHUNK - 1 - tk]
    s_re = pwr[..., None] * bb_re[None] - pwi[..., None] * bb_im[None]
    s_im = pwr[..., None] * bb_im[None] + pwi[..., None] * bb_re[None]
    s_ri = jnp.stack([s_re, s_im], axis=0).reshape(2, CHUNK, no, OCT, p, h)
    in_t = jnp.transpose(s_ri, (2, 1, 3, 5, 0, 4)).reshape(no, CHUNK, LANES, LANES)
    c_ri = jnp.stack([cp_re[1:], -cp_im[1:]], axis=0).reshape(2, CHUNK, no, OCT, h, p)
    out_t = jnp.transpose(c_ri, (2, 1, 0, 5, 3, 4)).reshape(no, CHUNK, LANES, LANES)
    are = pw_re[CHUNK].reshape(no, OCT, 1, p)
    aim = pw_im[CHUNK].reshape(no, OCT, 1, p)
    a1 = jnp.concatenate([are, are], axis=2).reshape(no, 1, OCT * LANES)
    a2 = jnp.concatenate([-aim, aim], axis=2).reshape(no, 1, OCT * LANES)
    return lag_t.astype(BF16), in_t.astype(BF16), out_t.astype(BF16), a1, a2


def _gelu_tanh(x):
    return 0.5 * x * (1.0 + jnp.tanh(math.sqrt(2.0 / math.pi) * (x + 0.044715 * x * x * x)))


def _s5_kernel(u_ref, lag_ref, in_ref, out_ref, a1_ref, a2_ref, d_ref, y_ref,
               uscr, sscr, pscr, yscr, state, t_op, b_op, c_op):
    nb, tm, _ = u_ref.shape
    nc = tm // CHUNK
    nq = a1_ref.shape[-1] // LANES

    @pl.when(pl.program_id(1) == 0)
    def _():
        state[...] = jnp.zeros_like(state)
        rowg = lax.broadcasted_iota(I32, (LANES, LANES), 0) // SSM_GROUP_DIM
        colg = lax.broadcasted_iota(I32, (LANES, LANES), 1) // SSM_GROUP_DIM
        zero = jnp.zeros((LANES, LANES), BF16)
        for k in range(CHUNK):
            in_k = in_ref[0, k]
            out_k = out_ref[0, k]
            for q in range(nq):
                b_op[k * LANES:(k + 1) * LANES, q * LANES:(q + 1) * LANES] = jnp.where(rowg == q, in_k, zero)
                c_op[q * LANES:(q + 1) * LANES, k * LANES:(k + 1) * LANES] = jnp.where(colg == q, out_k, zero)
            for t in range(CHUNK):
                t_op[k * LANES:(k + 1) * LANES, t * LANES:(t + 1) * LANES] = (
                    lag_ref[0, t - k] if t >= k else zero)

    for bi in range(nb):
        uscr[bi * tm:(bi + 1) * tm, :] = u_ref[bi].astype(F32)
    xk = jnp.concatenate(
        [uscr[pl.ds(k, nb * nc, stride=CHUNK), :] for k in range(CHUNK)], axis=1).astype(BF16)
    rr = nb * nc
    sc_all = jnp.dot(xk, b_op[...], preferred_element_type=F32)
    for q in range(nq):
        sscr[q * rr:(q + 1) * rr, :] = sc_all[:, q * LANES:(q + 1) * LANES]

    a1 = [a1_ref[0, :, q * LANES:(q + 1) * LANES] for q in range(nq)]
    a2 = [a2_ref[0, :, q * LANES:(q + 1) * LANES] for q in range(nq)]
    st = [state[:, q * LANES:(q + 1) * LANES] for q in range(nq)]
    for c in range(nc):
        for q in range(nq):
            pscr[pl.ds(q * rr + c, nb, stride=nc), :] = st[q]
            sc = sscr[pl.ds(q * rr + c, nb, stride=nc), :]
            st[q] = a1[q] * st[q] + a2[q] * pltpu.roll(st[q], LANES // 2, axis=1) + sc
    for q in range(nq):
        state[:, q * LANES:(q + 1) * LANES] = st[q]

    xprev = jnp.concatenate([pscr[q * rr:(q + 1) * rr, :] for q in range(nq)], axis=1)
    y = (jnp.dot(xk, t_op[...], preferred_element_type=F32)
         + jnp.dot(xprev.astype(BF16), c_op[...], preferred_element_type=F32))
    for t in range(CHUNK):
        yscr[pl.ds(t, nb * nc, stride=CHUNK), :] = y[:, t * LANES:(t + 1) * LANES]
    out = _gelu_tanh(yscr[...] + d_ref[...] * uscr[...])
    for bi in range(nb):
        y_ref[bi] = out[bi * tm:(bi + 1) * tm, :].astype(BF16)


def _s5(z, col0, lag_t, in_t, out_t, a1, a2, d_skip, tm=1024):
    nb, s, _ = z.shape
    no = lag_t.shape[0]
    w = no * LANES
    ns = a1.shape[-1]
    tm = min(tm, s)
    rows = nb * tm
    cb0 = col0 // LANES
    tile_spec = pl.BlockSpec((1, CHUNK, LANES, LANES), lambda o, i: (o, 0, 0, 0))
    return pl.pallas_call(
        _s5_kernel,
        out_shape=jax.ShapeDtypeStruct((nb, s, w), BF16),
        grid=(no, s // tm),
        in_specs=[pl.BlockSpec((nb, tm, LANES), lambda o, i: (0, i, cb0 + o)),
                  tile_spec, tile_spec, tile_spec,
                  pl.BlockSpec((1, 1, ns), lambda o, i: (o, 0, 0)),
                  pl.BlockSpec((1, 1, ns), lambda o, i: (o, 0, 0)),
                  pl.BlockSpec((1, LANES), lambda o, i: (0, o))],
        out_specs=pl.BlockSpec((nb, tm, LANES), lambda o, i: (0, i, o)),
        scratch_shapes=[pltpu.VMEM((rows, LANES), F32),
                        pltpu.VMEM((rows // CHUNK * (ns // LANES), LANES), F32),
                        pltpu.VMEM((rows // CHUNK * (ns // LANES), LANES), F32),
                        pltpu.VMEM((rows, LANES), F32),
                        pltpu.VMEM((nb, ns), F32),
                        pltpu.VMEM((CHUNK * LANES, CHUNK * LANES), BF16),
                        pltpu.VMEM((CHUNK * LANES, ns), BF16),
                        pltpu.VMEM((ns, CHUNK * LANES), BF16)],
        compiler_params=_cparams(("arbitrary", "arbitrary")),
        name="s5",
    )(z, lag_t, in_t, out_t, a1, a2, d_skip.reshape(1, w))


def _mixout_kernel(up_ref, halo_ref, ys_ref, gp_ref, gs_ref, x_ref, gt_ref,
                   pw_ref, ps_ref, wpo_ref, wglu_ref, bglu_ref, wout_ref, o_ref, escr):
    tm = up_ref.shape[1]
    i = pl.program_id(1)
    gw = pw_ref.shape[1]
    hal = halo_ref.shape[1]
    d = o_ref.shape[2]
    escr[0:hal, :] = jnp.where(i > 0, halo_ref[0].astype(F32), 0.0)
    escr[hal:hal + tm, :] = up_ref[0].astype(F32)
    pos = (i * tm + lax.broadcasted_iota(I32, (tm, gw), 0) + 1).astype(F32)
    mixed = []
    for g, win in enumerate(POOL_WINDOWS):
        cols = slice(g * gw, (g + 1) * gw)
        cur = escr[hal:hal + tm, cols]
        acc = cur
        for j in range(1, win):
            acc = acc + escr[hal - j:hal - j + tm, cols]
        pooled = acc / jnp.minimum(pos, float(win)) - cur
        mixed.append(jnp.dot(pooled.astype(BF16), pw_ref[g], preferred_element_type=F32))
    mixed = jnp.concatenate(mixed, axis=1) * ps_ref[...]
    y_pool = jnp.dot(mixed.astype(BF16), wpo_ref[...], preferred_element_type=F32)
    glu = jnp.dot(ys_ref[0], wglu_ref[...], preferred_element_type=F32) + bglu_ref[...]
    y_ssm = glu[:, :d] * jax.nn.sigmoid(glu[:, d:])
    merged = (jax.nn.sigmoid(gp_ref[0].astype(F32)) * y_pool
              + jax.nn.sigmoid(gs_ref[0].astype(F32)) * y_ssm)
    mix = jnp.dot(merged.astype(BF16), wout_ref[...], preferred_element_type=F32)
    o_ref[0] = x_ref[0] + gt_ref[0] * mix


def _mixout(z, ys, x, gt, pool_w, pool_scale, w_pool_out, w_glu, b_glu, w_out, tm=256):
    b, s, d = x.shape
    pwid = w_pool_out.shape[0]
    hal = max(POOL_WINDOWS)
    tpb = tm // hal
    gcol = (pwid + ys.shape[2]) // d
    return pl.pallas_call(
        _mixout_kernel,
        out_shape=jax.ShapeDtypeStruct((b, s, d), F32),
        grid=(b, s // tm),
        in_specs=[pl.BlockSpec((1, tm, pwid), lambda bi, i: (bi, i, 0)),
                  pl.BlockSpec((1, hal, pwid), lambda bi, i: (bi, jnp.maximum(i * tpb - 1, 0), 0)),
                  pl.BlockSpec((1, tm, ys.shape[2]), lambda bi, i: (bi, i, 0)),
                  pl.BlockSpec((1, tm, d), lambda bi, i: (bi, i, gcol)),
                  pl.BlockSpec((1, tm, d), lambda bi, i: (bi, i, gcol + 1)),
                  pl.BlockSpec((1, tm, d), lambda bi, i: (bi, i, 0)),
                  pl.BlockSpec((1, 1, d), lambda bi, i: (bi, 0, 0)),
                  _const_spec(pool_w.shape),
                  _const_spec((1, pwid)),
                  _const_spec(w_pool_out.shape),
                  _const_spec(w_glu.shape),
                  _const_spec((1, w_glu.shape[1])),
                  _const_spec(w_out.shape)],
        out_specs=pl.BlockSpec((1, tm, d), lambda bi, i: (bi, i, 0)),
        scratch_shapes=[pltpu.VMEM((hal + tm, pwid), F32)],
        compiler_params=_cparams(("arbitrary", "arbitrary")),
        name="mixout",
    )(z, z, ys, z, z, x, gt.reshape(b, 1, d), pool_w, pool_scale.reshape(1, pwid),
      w_pool_out, w_glu, b_glu.reshape(1, -1), w_out)


def _router_kernel(x_ref, g_ref, sh_ref, sc_ref, wr_ref, br_ref,
                   hp_ref, idx_ref, wt_ref, rank_ref, cnt_ref, carry, *, n_exp):
    tm = x_ref.shape[1]
    first = (pl.program_id(0) == 0) & (pl.program_id(1) == 0)

    @pl.when(first)
    def _():
        carry[...] = jnp.zeros_like(carry)

    h2 = _rms_mod(x_ref[0], g_ref[...], sh_ref[0], sc_ref[0])
    hp_ref[...] = _pack_rows(h2)
    logits = jnp.dot(h2, wr_ref[...], preferred_element_type=F32,
                     precision=lax.Precision.HIGHEST) + br_ref[...]
    lt = logits.T[:n_exp, :]
    eid = lax.broadcasted_iota(I32, lt.shape, 0).astype(F32)
    vals, idxs, hots = [], [], []
    v = lt
    for _ in range(TOP_K):
        m = jnp.max(v, axis=0, keepdims=True)
        sel = jnp.min(jnp.where(v == m, eid, float(n_exp)), axis=0, keepdims=True)
        hot = eid == sel
        vals.append(m)
        idxs.append(sel)
        hots.append(hot)
        v = jnp.where(hot, -jnp.inf, v)
    ex = [jnp.exp(m - vals[0]) for m in vals]
    tot = ex[0] + ex[1] + ex[2] + ex[3]
    msum = sum(h.astype(F32) for h in hots)
    tri = (lax.broadcasted_iota(I32, (tm, tm), 0) < lax.broadcasted_iota(I32, (tm, tm), 1))
    before = jnp.dot(msum.astype(BF16), tri.astype(BF16), preferred_element_type=F32) + carry[:, 0:1]
    for k in range(TOP_K):
        idx_ref[k:k + 1, :] = idxs[k].astype(I32)
        wt_ref[k:k + 1, :] = ex[k] / tot
        rank_ref[k:k + 1, :] = jnp.sum(jnp.where(hots[k], before, 0.0), axis=0,
                                       keepdims=True).astype(I32)
    carry[...] = carry[...] + jnp.sum(msum, axis=1, keepdims=True)
    cnt_ref[...] = carry[...].astype(I32)


def _router(x1, g, sh, sc, w_router, b_router, tm=512):
    b, s, d = x1.shape
    t = b * s
    e = w_router.shape[1]
    wd = d // 2
    wr = jnp.zeros((d, LANES), F32).at[:, :e].set(w_router.astype(F32))
    br = jnp.full((1, LANES), -1e30, F32).at[0, :e].set(b_router.astype(F32))
    spb = s // tm
    return pl.pallas_call(
        functools.partial(_router_kernel, n_exp=e),
        out_shape=[jax.ShapeDtypeStruct((t, wd), U32),
                   jax.ShapeDtypeStruct((TOP_K, t), I32),
                   jax.ShapeDtypeStruct((TOP_K, t), F32),
                   jax.ShapeDtypeStruct((TOP_K, t), I32),
                   jax.ShapeDtypeStruct((e, LANES), I32)],
        grid=(b, spb),
        in_specs=[pl.BlockSpec((1, tm, d), lambda bi, i: (bi, i, 0)),
                  _const_spec((1, d)),
                  pl.BlockSpec((1, 1, d), lambda bi, i: (bi, 0, 0)),
                  pl.BlockSpec((1, 1, d), lambda bi, i: (bi, 0, 0)),
                  _const_spec((d, LANES)),
                  _const_spec((1, LANES))],
        out_specs=[pl.BlockSpec((tm, wd), lambda bi, i: (bi * spb + i, 0)),
                   pl.BlockSpec((TOP_K, tm), lambda bi, i: (0, bi * spb + i)),
                   pl.BlockSpec((TOP_K, tm), lambda bi, i: (0, bi * spb + i)),
                   pl.BlockSpec((TOP_K, tm), lambda bi, i: (0, bi * spb + i)),
                   pl.BlockSpec((e, LANES), lambda bi, i: (0, 0))],
        scratch_shapes=[pltpu.VMEM((e, LANES), F32)],
        compiler_params=_cparams(("arbitrary", "arbitrary")),
        name="router",
    )(x1, g.reshape(1, d), sh.reshape(b, 1, d), sc.reshape(b, 1, d), wr, br)


def _dispatch_kernel(dest_ref, zflag_ref, hp_ref, xs_ref, zbuf, sem, zsem, *, n_tok):
    tm = hp_ref.shape[0]
    i = pl.program_id(0)

    @pl.when(i == 0)
    def _():
        zbuf[...] = jnp.zeros_like(zbuf)

        def zcopy(q):
            r0 = pl.multiple_of(q * SUB, SUB)
            return pltpu.make_async_copy(zbuf, xs_ref.at[pl.ds(r0, SUB)], zsem)

        def zstart(q, carry):
            pl.when(zflag_ref[q] != 0)(lambda: zcopy(q).start())
            return carry

        def zwait(q, carry):
            pl.when(zflag_ref[q] != 0)(lambda: zcopy(q).wait())
            return carry

        lax.fori_loop(0, zflag_ref.shape[0], zstart, 0)
        lax.fori_loop(0, zflag_ref.shape[0], zwait, 0)

    def issue(t, carry):
        for k in range(TOP_K):
            dst = dest_ref[k * n_tok + i * tm + t]
            pltpu.make_async_copy(hp_ref.at[pl.ds(t, 1)], xs_ref.at[pl.ds(dst, 1)],
                                  sem).start(priority=k % 2)
        return carry

    lax.fori_loop(0, tm, issue, 0, unroll=4)
    for k in range(TOP_K):
        pltpu.make_async_copy(hp_ref, xs_ref.at[pl.ds(0, tm)], sem).wait()


def _dispatch(dest_flat, zflag, hp, n_rows, tm=256):
    t, wd = hp.shape
    return pl.pallas_call(
        functools.partial(_dispatch_kernel, n_tok=t),
        out_shape=jax.ShapeDtypeStruct((n_rows, wd), U32),
        grid_spec=pltpu.PrefetchScalarGridSpec(
            num_scalar_prefetch=2,
            grid=(t // tm,),
            in_specs=[pl.BlockSpec((tm, wd), lambda i, *_: (i, 0))],
            out_specs=pl.BlockSpec(memory_space=pl.ANY),
            scratch_shapes=[pltpu.VMEM((SUB, wd), U32),
                            pltpu.SemaphoreType.DMA,
                            pltpu.SemaphoreType.DMA]),
        compiler_params=_cparams(("arbitrary",)),
        name="dispatch",
    )(dest_flat, zflag, hp)


def _experts_kernel(sbe_ref, nsub_ref, nused_ref, x_ref, b1_ref, b2_ref, w1_hbm, w2_hbm, o_ref,
                    xb, actb, yb, wt, sem):
    s = pl.program_id(0)
    n = nsub_ref[s]
    nused = nused_ref[0]
    nj, _, tf = actb.shape
    nn, _, pair = yb.shape
    d = xb.shape[1]
    f = nj * tf
    nslot = wt.shape[0]
    ntile = nj + nn
    per = SUPER // SUB

    def a_copies(e, j, slot):
        return [pltpu.make_async_copy(
            w1_hbm.at[e, :, pl.ds(pl.multiple_of(part * f + j * tf, tf), tf)],
            wt.at[slot, part, 0:d, 0:tf], sem.at[slot]) for part in range(2)]

    def b_copies(e, c, slot):
        return [pltpu.make_async_copy(
            w2_hbm.at[e, :, pl.ds(pl.multiple_of((2 * c + part) * pair, pair), pair)],
            wt.at[slot, part, 0:f, 0:pair], sem.at[slot]) for part in range(2)]

    def slot_of(sb, q):
        return lax.rem(sb * ntile + q, nslot)

    def start_tile(sb, q):
        e = sbe_ref[sb]
        slot = slot_of(sb, q)

        @pl.when(q < nj)
        def _():
            for cp in a_copies(e, q, slot):
                cp.start()

        @pl.when(q >= nj)
        def _():
            for cp in b_copies(e, q - nj, slot):
                cp.start()

    def prefetch(q):
        qq = q + nslot - 1

        @pl.when(qq < ntile)
        def _():
            start_tile(s, qq)

        @pl.when((qq >= ntile) & (s + 1 < nused))
        def _():
            start_tile(s + 1, qq - ntile)

    @pl.when((s == 0) & (nused > 0))
    def _():
        for q in range(nslot - 1):
            start_tile(0, jnp.int32(q))

    def run(v):
        m = v * SUB
        e = sbe_ref[s]
        for r in range(v):
            rows = slice(r * SUB, (r + 1) * SUB)
            xb[rows, :] = _unpack_rows(x_ref[rows, :]).astype(BF16)

        def a_body(j, carry):
            slot = slot_of(s, j)
            for cp in a_copies(e, j, slot):
                cp.wait()
            prefetch(j)
            x = xb[0:m, :]
            g = jnp.dot(x, wt[slot, 0, 0:d, 0:tf].astype(BF16), preferred_element_type=F32) + b1_ref[0, j]
            l = jnp.dot(x, wt[slot, 1, 0:d, 0:tf].astype(BF16), preferred_element_type=F32) + b1_ref[0, nj + j]
            xg = jnp.minimum(g, SWIGLU_LIMIT)
            xl = jnp.clip(l, -SWIGLU_LIMIT, SWIGLU_LIMIT)
            act = xg * jax.nn.sigmoid(SWIGLU_ALPHA * xg) * (xl + 1.0)
            actb[j, 0:m, :] = act.astype(BF16)
            return carry

        lax.fori_loop(0, nj, a_body, 0)

        def b_body(c, carry):
            slot = slot_of(s, nj + c)
            for cp in b_copies(e, c, slot):
                cp.wait()
            prefetch(nj + c)
            halves = []
            for part in range(2):
                w = wt[slot, part, 0:f, 0:pair].astype(BF16)
                y = b2_ref[0, 2 * c + part] + jnp.dot(actb[0, 0:m, :], w[0:tf, :],
                                                      preferred_element_type=F32)
                for j in range(1, nj):
                    y = y + jnp.dot(actb[j, 0:m, :], w[j * tf:(j + 1) * tf, :],
                                    preferred_element_type=F32)
                halves.append(y)
            yb[c, 0:m, :] = _pack_rows(jnp.concatenate(halves, axis=1))
            return carry

        lax.fori_loop(0, nn, b_body, 0)

        for c in range(nn):
            o_ref[0:m, c * pair:(c + 1) * pair] = yb[c, 0:m, :]
        if m < SUPER:
            o_ref[m:SUPER, :] = jnp.zeros((SUPER - m, o_ref.shape[1]), U32)

    for v in range(1, per + 1):
        pl.when((s < nused) & (n == v))(functools.partial(run, v))

    @pl.when(s >= nused)
    def _():
        o_ref[...] = jnp.zeros(o_ref.shape, U32)


def _experts(sbe, nsub, nused, xs, w1, b1, w2, b2, n_super, tf=256, nslot=3):
    n_exp, d, f2 = w1.shape
    f = f2 // 2
    tf = min(tf, f)
    pair = _pair_width(d)
    nj = f // tf
    nn = d // (2 * pair)
    wd = xs.shape[1]

    def x_map(s, sbe, nsub, nused):
        return (jnp.minimum(s, nused[0] - 1), 0)

    return pl.pallas_call(
        _experts_kernel,
        out_shape=jax.ShapeDtypeStruct((n_super * SUPER, wd), U32),
        grid_spec=pltpu.PrefetchScalarGridSpec(
            num_scalar_prefetch=3,
            grid=(n_super,),
            in_specs=[pl.BlockSpec((SUPER, wd), x_map),
                      pl.BlockSpec((1, 2 * nj, 1, tf), lambda s, sbe, nsub, nu: (sbe[s], 0, 0, 0)),
                      pl.BlockSpec((1, 2 * nn, 1, pair), lambda s, sbe, nsub, nu: (sbe[s], 0, 0, 0)),
                      pl.BlockSpec(memory_space=pl.ANY),
                      pl.BlockSpec(memory_space=pl.ANY)],
            out_specs=pl.BlockSpec((SUPER, wd), lambda s, *_: (s, 0)),
            scratch_shapes=[pltpu.VMEM((SUPER, d), BF16),
                            pltpu.VMEM((nj, SUPER, tf), BF16),
                            pltpu.VMEM((nn, SUPER, pair), U32),
                            pltpu.VMEM((nslot, 2, max(d, f), max(tf, pair)), F32),
                            pltpu.SemaphoreType.DMA((nslot,))]),
        compiler_params=_cparams(("arbitrary",)),
        name="experts",
    )(sbe, nsub, nused, xs, b1.reshape(n_exp, 2 * nj, 1, tf), b2.reshape(n_exp, 2 * nn, 1, pair), w1, w2)


def _combine_kernel(dest_ref, x_ref, wt_ref, gt_ref, g_ref, sh_ref, sc_ref, ys_ref, o_ref,
                    gbuf, sem, *, n_tok, tiles_per_batch, apply_norm):
    tm = x_ref.shape[1]
    i = pl.program_id(0) * tiles_per_batch + pl.program_id(1)
    n_tiles = pl.num_programs(0) * tiles_per_batch
    slot = lax.rem(i, 2)

    def gather(tile, buf):
        def issue(t, carry):
            for k in range(TOP_K):
                src = dest_ref[k * n_tok + tile * tm + t]
                pltpu.make_async_copy(ys_ref.at[pl.ds(src, 1)], gbuf.at[buf, k, pl.ds(t, 1)],
                                      sem.at[buf]).start(priority=k % 2)
            return carry

        lax.fori_loop(0, tm, issue, 0, unroll=4)

    @pl.when(i == 0)
    def _():
        gather(i, slot)

    @pl.when(i + 1 < n_tiles)
    def _():
        gather(i + 1, 1 - slot)

    for k in range(TOP_K):
        pltpu.make_async_copy(ys_ref.at[pl.ds(0, tm)], gbuf.at[slot, k], sem.at[slot]).wait()
    wt = wt_ref[...]
    ffn = wt[:, 0:1] * _unpack_rows(gbuf[slot, 0])
    for k in range(1, TOP_K):
        ffn = ffn + wt[:, k:k + 1] * _unpack_rows(gbuf[slot, k])
    x2 = x_ref[0] + gt_ref[0] * ffn
    o_ref[0] = _rms_mod(x2, g_ref[...], sh_ref[0], sc_ref[0]) if apply_norm else x2


def _combine(dest_flat, x1, wt_t, gt, g, sh, sc, ys, apply_norm, tm=256):
    b, s, d = x1.shape
    t = b * s
    spb = s // tm
    wd = ys.shape[1]
    return pl.pallas_call(
        functools.partial(_combine_kernel, n_tok=t, tiles_per_batch=spb, apply_norm=apply_norm),
        out_shape=jax.ShapeDtypeStruct((b, s, d), F32),
        grid_spec=pltpu.PrefetchScalarGridSpec(
            num_scalar_prefetch=1,
            grid=(b, spb),
            in_specs=[pl.BlockSpec((1, tm, d), lambda bi, i, *_: (bi, i, 0)),
                      pl.BlockSpec((tm, TOP_K), lambda bi, i, *_: (bi * spb + i, 0)),
                      pl.BlockSpec((1, 1, d), lambda bi, i, *_: (bi, 0, 0)),
                      pl.BlockSpec((1, d), lambda bi, i, *_: (0, 0)),
                      pl.BlockSpec((1, 1, d), lambda bi, i, *_: (bi, 0, 0)),
                      pl.BlockSpec((1, 1, d), lambda bi, i, *_: (bi, 0, 0)),
                      pl.BlockSpec(memory_space=pl.ANY)],
            out_specs=pl.BlockSpec((1, tm, d), lambda bi, i, *_: (bi, i, 0)),
            scratch_shapes=[pltpu.VMEM((2, TOP_K, tm, wd), U32),
                            pltpu.SemaphoreType.DMA((2,))]),
        compiler_params=_cparams(("arbitrary", "arbitrary")),
        name="combine",
    )(dest_flat, x1, wt_t, gt.reshape(b, 1, d), g.reshape(1, d), sh.reshape(b, 1, d),
      sc.reshape(b, 1, d), ys)


def _plan(idx, rank, counts, n_super):
    n_exp = counts.shape[0]
    nsb = (counts + SUPER - 1) // SUPER
    rps = (counts + jnp.maximum(nsb, 1) - 1) // jnp.maximum(nsb, 1)
    rps = jnp.maximum(rps, 1).astype(I32)
    sb_end = jnp.cumsum(nsb)
    sb_start = sb_end - nsb
    hot = idx[..., None] == jnp.arange(n_exp, dtype=I32)
    t_start = jnp.sum(jnp.where(hot, sb_start.astype(I32), 0), axis=-1)
    t_rps = jnp.sum(jnp.where(hot, rps, 0), axis=-1)
    dest = (t_start + rank // t_rps) * SUPER + rank % t_rps
    s_ids = jnp.arange(n_super, dtype=I32)
    nused = sb_end[-1]
    sbe = jnp.minimum(jnp.searchsorted(sb_end, jnp.minimum(s_ids, nused - 1), side='right'),
                      n_exp - 1).astype(I32)
    per = SUPER // SUB
    q_ids = jnp.arange(n_super * per, dtype=I32)
    q_s = q_ids // per
    q_e = sbe[q_s]
    in_sb = jnp.minimum(rps[q_e], counts[q_e] - (q_s - sb_start[q_e]) * rps[q_e])
    left = in_sb - (q_ids % per) * SUB
    vrows = jnp.where(q_s < nused, jnp.clip(left, 0, SUB), 0)
    nsub = jnp.sum((vrows > 0).reshape(n_super, per), axis=1).astype(I32)
    zflag = (vrows < SUB).astype(I32)
    return dest.reshape(-1).astype(I32), sbe, nsub, nused.reshape(1).astype(I32), zflag


def kernel(x, c, ada_w, ada_b, norm1_g, w_in, pool_w, pool_scale, w_pool_out, ssm_lam_re, ssm_lam_im,
           ssm_log_dt, ssm_b_re, ssm_b_im, ssm_c_re, ssm_c_im, ssm_d, w_glu, b_glu, w_out, norm2_g,
           w_router, b_router, w1, b1, w2, b2, final_ada_w, final_ada_b, final_norm_g):
    b, s, d = x.shape
    t = b * s
    depth = ada_w.shape[0]
    n_exp = w_router.shape[-1]
    pwid = w_pool_out.shape[1]
    n_super = (t * TOP_K) // SUPER + n_exp

    c8 = jnp.zeros((8, d), F32).at[:b].set(c.astype(F32))
    fmod = _ada(c8, final_ada_w, final_ada_b)[:b]
    sh_o, sc_o = jnp.split(fmod, 2, axis=-1)
    for l in range(depth):
        mod = _ada(c8, ada_w[l], ada_b[l])[:b]
        sh_m, sc_m, gt_m, sh_f, sc_f, gt_f = jnp.split(mod, 6, axis=-1)
        z = _inproj(x, norm1_g[l], sh_m, sc_m, w_in[l].astype(BF16))
        ops = _s5_weights(ssm_lam_re[l], ssm_lam_im[l], ssm_log_dt[l], ssm_b_re[l], ssm_b_im[l],
                          ssm_c_re[l], ssm_c_im[l])
        ys = _s5(z, pwid, *ops, ssm_d[l])
        x = _mixout(z, ys, x, gt_m, pool_w[l].astype(BF16), pool_scale[l], w_pool_out[l].astype(BF16),
                    w_glu[l].astype(BF16), b_glu[l], w_out[l].astype(BF16))
        hp, idx, wt, rank, cnt = _router(x, norm2_g[l], sh_f, sc_f, w_router[l], b_router[l])
        dest, sbe, nsub, nused, zflag = _plan(idx, rank, cnt[:, 0], n_super)
        xs = _dispatch(dest, zflag, hp, n_super * SUPER)
        ye = _experts(sbe, nsub, nused, xs, w1[l], b1[l], w2[l], b2[l], n_super)
        last = l == depth - 1
        x = _combine(dest, x, wt.T, gt_f, final_norm_g, sh_o, sc_o, ye, apply_norm=last)
    return x
```

```python
import functools
import math

import jax
import jax.numpy as jnp
from jax import lax
from jax.experimental import pallas as pl
from jax.experimental.pallas import tpu as pltpu

F32 = jnp.float32
BF16 = jnp.bfloat16
I32 = jnp.int32
U32 = jnp.uint32

RMS_EPS = 1e-6
POOL_WINDOWS = (2, 4, 8, 16)
SSM_GROUP_DIM = 16
SSM_STATE = 64
LAMBDA_RE_MAX = -1e-4
TOP_K = 4
SWIGLU_ALPHA = 1.702
SWIGLU_LIMIT = 7.0

LANES = 128
V7X_VMEM_LIMIT = 56 * 1024 * 1024

CHUNK = 16
OCT = LANES // SSM_GROUP_DIM
SUB = 256
SUPER = 4 * SUB


def _cparams(sem, vmem=V7X_VMEM_LIMIT):
    return pltpu.CompilerParams(dimension_semantics=sem, vmem_limit_bytes=vmem)


def _const_spec(shape):
    nd = len(shape)
    return pl.BlockSpec(shape, lambda *_: (0,) * nd, pipeline_mode=pl.Buffered(1))


def _rms_mod(x, g, sh, sc):
    y = x * lax.rsqrt(jnp.mean(x * x, axis=-1, keepdims=True) + RMS_EPS) * g
    return y * (1.0 + sc) + sh


def _pair_width(d):
    return min(2 * LANES, d // 2)


def _pack_rows(h):
    d = h.shape[1]
    pair = _pair_width(d)
    hb = lax.bitcast_convert_type(h.astype(BF16).astype(F32), U32)
    words = []
    for n in range(d // (2 * pair)):
        lo = hb[:, 2 * pair * n:2 * pair * n + pair]
        hi = hb[:, 2 * pair * n + pair:2 * pair * (n + 1)]
        words.append((lo >> 16) | (hi & jnp.uint32(0xFFFF0000)))
    return jnp.concatenate(words, axis=1)


def _unpack_rows(w):
    d = 2 * w.shape[1]
    pair = _pair_width(d)
    cols = []
    for n in range(d // (2 * pair)):
        wn = w[:, pair * n:pair * (n + 1)]
        cols.append(lax.bitcast_convert_type(wn << 16, F32))
        cols.append(lax.bitcast_convert_type(wn & jnp.uint32(0xFFFF0000), F32))
    return jnp.concatenate(cols, axis=1)


def _ada_kernel(c_ref, w_ref, b_ref, o_ref):
    c = c_ref[...]
    ca = c * jax.nn.sigmoid(c)
    o_ref[...] = jnp.dot(ca.astype(BF16), w_ref[...].astype(BF16),
                         preferred_element_type=F32) + b_ref[...]


def _ada(c8, w, b, tn=1024):
    d, n = w.shape
    tn = math.gcd(tn, n)
    return pl.pallas_call(
        _ada_kernel,
        out_shape=jax.ShapeDtypeStruct((c8.shape[0], n), F32),
        grid=(n // tn,),
        in_specs=[pl.BlockSpec(c8.shape, lambda j: (0, 0)),
                  pl.BlockSpec((d, tn), lambda j: (0, j)),
                  pl.BlockSpec((1, tn), lambda j: (0, j))],
        out_specs=pl.BlockSpec((c8.shape[0], tn), lambda j: (0, j)),
        compiler_params=_cparams(("arbitrary",)),
        name="ada",
    )(c8, w, b.reshape(1, n))


def _inproj_kernel(x_ref, g_ref, sh_ref, sc_ref, w_ref, z_ref, *, ncol):
    h = _rms_mod(x_ref[0], g_ref[...], sh_ref[0], sc_ref[0]).astype(BF16)
    n = w_ref.shape[1]
    for c in range(n // ncol):
        z_ref[0, :, c * ncol:(c + 1) * ncol] = jnp.dot(
            h, w_ref[:, c * ncol:(c + 1) * ncol], preferred_element_type=F32).astype(BF16)


def _inproj(x, g, sh, sc, w, tm=512, ncol=1024):
    b, s, d = x.shape
    n = w.shape[1]
    ncol = math.gcd(ncol, n)
    return pl.pallas_call(
        functools.partial(_inproj_kernel, ncol=ncol),
        out_shape=jax.ShapeDtypeStruct((b, s, n), BF16),
        grid=(b, s // tm),
        in_specs=[pl.BlockSpec((1, tm, d), lambda bi, i: (bi, i, 0)),
                  _const_spec((1, d)),
                  pl.BlockSpec((1, 1, d), lambda bi, i: (bi, 0, 0)),
                  pl.BlockSpec((1, 1, d), lambda bi, i: (bi, 0, 0)),
                  _const_spec((d, n))],
        out_specs=pl.BlockSpec((1, tm, n), lambda bi, i: (bi, i, 0)),
        compiler_params=_cparams(("arbitrary", "arbitrary")),
        name="inproj",
    )(x, g.reshape(1, d), sh.reshape(b, 1, d), sc.reshape(b, 1, d), w)


def _s5_weights(lam_re, lam_im, log_dt, b_re, b_im, c_re, c_im):
    hp = lax.Precision.HIGHEST
    g_all, p = lam_re.shape
    h = b_re.shape[-1]
    no = g_all // OCT
    dt = jnp.exp(log_dt.astype(F32))[:, None]
    lre = jnp.minimum(lam_re.astype(F32), LAMBDA_RE_MAX)
    lim = lam_im.astype(F32)
    mag = jnp.exp(lre * dt)
    ang = lim * dt
    ab_re = mag * jnp.cos(ang)
    ab_im = mag * jnp.sin(ang)
    den = lre * lre + lim * lim
    nr = ab_re - 1.0
    ni = ab_im
    f_re = (nr * lre + ni * lim) / den
    f_im = (ni * lre - nr * lim) / den
    br_, bi_ = b_re.astype(F32), b_im.astype(F32)
    bb_re = f_re[..., None] * br_ - f_im[..., None] * bi_
    bb_im = f_re[..., None] * bi_ + f_im[..., None] * br_
    cr, ci = c_re.astype(F32), c_im.astype(F32)
    j = jnp.arange(CHUNK + 1, dtype=F32)[:, None, None]
    pw_mag = jnp.exp(j * (lre * dt)[None])
    pw_re = pw_mag * jnp.cos(j * ang[None])
    pw_im = pw_mag * jnp.sin(j * ang[None])
    cp_re = cr[None] * pw_re[:, :, None, :] - ci[None] * pw_im[:, :, None, :]
    cp_im = cr[None] * pw_im[:, :, None, :] + ci[None] * pw_re[:, :, None, :]
    assert 2 * p == LANES and h * OCT == LANES
    tk = jnp.arange(CHUNK)

    def by_tile(a):
        a = a.reshape((a.shape[0], no, OCT) + a.shape[2:])
        return jnp.swapaxes(a, 0, 1)

    klag = (jnp.einsum('jghp,gpi->jgih', cp_re[:CHUNK], bb_re, precision=hp)
            - jnp.einsum('jghp,gpi->jgih', cp_im[:CHUNK], bb_im, precision=hp))
    lag_t = by_tile(klag).reshape(no, CHUNK, LANES, h)
    bt_re = jnp.swapaxes(bb_re, 1, 2)[None]
    bt_im = jnp.swapaxes(bb_im, 1, 2)[None]
    pwr = pw_re[CHUNK - 1 - tk][:, :, None, :]
    pwi = pw_im[CHUNK - 1 - tk][:, :, None, :]
    in_t = jnp.concatenate([pwr * bt_re - pwi * bt_im, pwr * bt_im + pwi * bt_re], axis=-1)
    in_t = by_tile(in_t).reshape(no, CHUNK, LANES, LANES)
    out_t = jnp.concatenate([cp_re[1:], -cp_im[1:]], axis=-1)
    out_t = by_tile(out_t).reshape(no, CHUNK, LANES, LANES)
    are = pw_re[CHUNK].reshape(no, OCT, 1, p)
    aim = pw_im[CHUNK].reshape(no, OCT, 1, p)
    a1 = jnp.concatenate([are, are], axis=2).reshape(no, 1, OCT * LANES)
    a2 = jnp.concatenate([-aim, aim], axis=2).reshape(no, 1, OCT * LANES)
    return lag_t, in_t.astype(BF16), out_t, a1, a2


def _gelu_tanh(x):
    return 0.5 * x * (1.0 + jnp.tanh(math.sqrt(2.0 / math.pi) * (x + 0.044715 * x * x * x)))


def _s5_kernel(u_ref, lag_ref, in_ref, out_ref, a1_ref, a2_ref, d_ref, y_ref,
               uscr, sscr, pscr, yscr, state, t_op, b_op, c_op):
    nb, tm, _ = u_ref.shape
    nc = tm // CHUNK
    nq = a1_ref.shape[-1] // LANES

    @pl.when(pl.program_id(1) == 0)
    def _():
        state[...] = jnp.zeros_like(state)
        rowg = lax.broadcasted_iota(I32, (LANES, LANES), 0) // SSM_GROUP_DIM
        colg = lax.broadcasted_iota(I32, (LANES, LANES), 1) // SSM_GROUP_DIM
        zero = jnp.zeros((LANES, LANES), BF16)
        rep = (lax.broadcasted_iota(I32, (SSM_GROUP_DIM, LANES), 1) % SSM_GROUP_DIM
               == lax.broadcasted_iota(I32, (SSM_GROUP_DIM, LANES), 0)).astype(BF16)
        lag = []
        for j in range(CHUNK):
            wide = jnp.dot(lag_ref[0, j].astype(BF16), rep, preferred_element_type=F32).astype(BF16)
            lag.append(jnp.where(rowg == colg, wide, zero))
        for k in range(CHUNK):
            in_k = in_ref[0, k]
            out_k = out_ref[0, k].T.astype(BF16)
            for q in range(nq):
                b_op[k * LANES:(k + 1) * LANES, q * LANES:(q + 1) * LANES] = jnp.where(rowg == q, in_k, zero)
                c_op[q * LANES:(q + 1) * LANES, k * LANES:(k + 1) * LANES] = jnp.where(colg == q, out_k, zero)
            for t in range(CHUNK):
                t_op[k * LANES:(k + 1) * LANES, t * LANES:(t + 1) * LANES] = lag[t - k] if t >= k else zero

    for bi in range(nb):
        uscr[bi * tm:(bi + 1) * tm, :] = u_ref[bi].astype(F32)
    xk = jnp.concatenate(
        [uscr[pl.ds(k, nb * nc, stride=CHUNK), :] for k in range(CHUNK)], axis=1).astype(BF16)
    rr = nb * nc
    sc_all = jnp.dot(xk, b_op[...], preferred_element_type=F32)
    for q in range(nq):
        sscr[q * rr:(q + 1) * rr, :] = sc_all[:, q * LANES:(q + 1) * LANES]

    a1 = [a1_ref[0, :, q * LANES:(q + 1) * LANES] for q in range(nq)]
    a2 = [a2_ref[0, :, q * LANES:(q + 1) * LANES] for q in range(nq)]
    st = [state[:, q * LANES:(q + 1) * LANES] for q in range(nq)]
    for c in range(nc):
        for q in range(nq):
            pscr[pl.ds(q * rr + c, nb, stride=nc), :] = st[q]
            sc = sscr[pl.ds(q * rr + c, nb, stride=nc), :]
            st[q] = a1[q] * st[q] + a2[q] * pltpu.roll(st[q], LANES // 2, axis=1) + sc
    for q in range(nq):
        state[:, q * LANES:(q + 1) * LANES] = st[q]

    xprev = jnp.concatenate([pscr[q * rr:(q + 1) * rr, :] for q in range(nq)], axis=1)
    y = (jnp.dot(xk, t_op[...], preferred_element_type=F32)
         + jnp.dot(xprev.astype(BF16), c_op[...], preferred_element_type=F32))
    for t in range(CHUNK):
        yscr[pl.ds(t, nb * nc, stride=CHUNK), :] = y[:, t * LANES:(t + 1) * LANES]
    out = _gelu_tanh(yscr[...] + d_ref[...] * uscr[...])
    for bi in range(nb):
        y_ref[bi] = out[bi * tm:(bi + 1) * tm, :].astype(BF16)


def _s5(z, col0, lag_t, in_t, out_t, a1, a2, d_skip, tm=1024):
    nb, s, _ = z.shape
    no = lag_t.shape[0]
    w = no * LANES
    ns = a1.shape[-1]
    tm = min(tm, s)
    rows = nb * tm
    cb0 = col0 // LANES
    tile_spec = pl.BlockSpec((1, CHUNK, LANES, LANES), lambda o, i: (o, 0, 0, 0))
    lag_spec = pl.BlockSpec((1, CHUNK, LANES, lag_t.shape[-1]), lambda o, i: (o, 0, 0, 0))
    return pl.pallas_call(
        _s5_kernel,
        out_shape=jax.ShapeDtypeStruct((nb, s, w), BF16),
        grid=(no, s // tm),
        in_specs=[pl.BlockSpec((nb, tm, LANES), lambda o, i: (0, i, cb0 + o)),
                  lag_spec, tile_spec, tile_spec,
                  pl.BlockSpec((1, 1, ns), lambda o, i: (o, 0, 0)),
                  pl.BlockSpec((1, 1, ns), lambda o, i: (o, 0, 0)),
                  pl.BlockSpec((1, LANES), lambda o, i: (0, o))],
        out_specs=pl.BlockSpec((nb, tm, LANES), lambda o, i: (0, i, o)),
        scratch_shapes=[pltpu.VMEM((rows, LANES), F32),
                        pltpu.VMEM((rows // CHUNK * (ns // LANES), LANES), F32),
                        pltpu.VMEM((rows // CHUNK * (ns // LANES), LANES), F32),
                        pltpu.VMEM((rows, LANES), F32),
                        pltpu.VMEM((nb, ns), F32),
                        pltpu.VMEM((CHUNK * LANES, CHUNK * LANES), BF16),
                        pltpu.VMEM((CHUNK * LANES, ns), BF16),
                        pltpu.VMEM((ns, CHUNK * LANES), BF16)],
        compiler_params=_cparams(("arbitrary", "arbitrary")),
        name="s5",
    )(z, lag_t, in_t, out_t, a1, a2, d_skip.reshape(1, w))


def _mixout_kernel(up_ref, halo_ref, ys_ref, gp_ref, gs_ref, x_ref, gt_ref,
                   pw_ref, ps_ref, wpo_ref, wglu_ref, bglu_ref, wout_ref, o_ref, escr):
    tm = up_ref.shape[1]
    i = pl.program_id(1)
    gw = pw_ref.shape[1]
    hal = halo_ref.shape[1]
    d = o_ref.shape[2]
    escr[0:hal, :] = jnp.where(i > 0, halo_ref[0].astype(F32), 0.0)
    escr[hal:hal + tm, :] = up_ref[0].astype(F32)
    pos = (i * tm + lax.broadcasted_iota(I32, (tm, gw), 0) + 1).astype(F32)
    mixed = []
    for g, win in enumerate(POOL_WINDOWS):
        cols = slice(g * gw, (g + 1) * gw)
        cur = escr[hal:hal + tm, cols]
        acc = cur
        for j in range(1, win):
            acc = acc + escr[hal - j:hal - j + tm, cols]
        pooled = acc / jnp.minimum(pos, float(win)) - cur
        mixed.append(jnp.dot(pooled.astype(BF16), pw_ref[g], preferred_element_type=F32))
    mixed = jnp.concatenate(mixed, axis=1) * ps_ref[...]
    y_pool = jnp.dot(mixed.astype(BF16), wpo_ref[...], preferred_element_type=F32)
    glu = jnp.dot(ys_ref[0], wglu_ref[...], preferred_element_type=F32) + bglu_ref[...]
    y_ssm = glu[:, :d] * jax.nn.sigmoid(glu[:, d:])
    merged = (jax.nn.sigmoid(gp_ref[0].astype(F32)) * y_pool
              + jax.nn.sigmoid(gs_ref[0].astype(F32)) * y_ssm)
    mix = jnp.dot(merged.astype(BF16), wout_ref[...], preferred_element_type=F32)
    o_ref[0] = x_ref[0] + gt_ref[0] * mix


def _mixout(z, ys, x, gt, pool_w, pool_scale, w_pool_out, w_glu, b_glu, w_out, tm=256):
    b, s, d = x.shape
    pwid = w_pool_out.shape[0]
    hal = max(POOL_WINDOWS)
    tpb = tm // hal
    gcol = (pwid + ys.shape[2]) // d
    return pl.pallas_call(
        _mixout_kernel,
        out_shape=jax.ShapeDtypeStruct((b, s, d), F32),
        grid=(b, s // tm),
        in_specs=[pl.BlockSpec((1, tm, pwid), lambda bi, i: (bi, i, 0)),
                  pl.BlockSpec((1, hal, pwid), lambda bi, i: (bi, jnp.maximum(i * tpb - 1, 0), 0)),
                  pl.BlockSpec((1, tm, ys.shape[2]), lambda bi, i: (bi, i, 0)),
                  pl.BlockSpec((1, tm, d), lambda bi, i: (bi, i, gcol)),
                  pl.BlockSpec((1, tm, d), lambda bi, i: (bi, i, gcol + 1)),
                  pl.BlockSpec((1, tm, d), lambda bi, i: (bi, i, 0)),
                  pl.BlockSpec((1, 1, d), lambda bi, i: (bi, 0, 0)),
                  _const_spec(pool_w.shape),
                  _const_spec((1, pwid)),
                  _const_spec(w_pool_out.shape),
                  _const_spec(w_glu.shape),
                  _const_spec((1, w_glu.shape[1])),
                  _const_spec(w_out.shape)],
        out_specs=pl.BlockSpec((1, tm, d), lambda bi, i: (bi, i, 0)),
        scratch_shapes=[pltpu.VMEM((hal + tm, pwid), F32)],
        compiler_params=_cparams(("arbitrary", "arbitrary")),
        name="mixout",
    )(z, z, ys, z, z, x, gt.reshape(b, 1, d), pool_w, pool_scale.reshape(1, pwid),
      w_pool_out, w_glu, b_glu.reshape(1, -1), w_out)


def _router_kernel(x_ref, g_ref, sh_ref, sc_ref, wr_ref, br_ref,
                   hp_ref, idx_ref, wt_ref, rank_ref, cnt_ref, carry, *, n_exp):
    tm = x_ref.shape[1]
    first = (pl.program_id(0) == 0) & (pl.program_id(1) == 0)

    @pl.when(first)
    def _():
        carry[...] = jnp.zeros_like(carry)

    h2 = _rms_mod(x_ref[0], g_ref[...], sh_ref[0], sc_ref[0])
    hp_ref[...] = _pack_rows(h2)
    logits = jnp.dot(h2, wr_ref[...], preferred_element_type=F32,
                     precision=lax.Precision.HIGHEST) + br_ref[...]
    lt = logits.T[:n_exp, :]
    eid = lax.broadcasted_iota(I32, lt.shape, 0).astype(F32)
    vals, idxs, hots = [], [], []
    v = lt
    for _ in range(TOP_K):
        m = jnp.max(v, axis=0, keepdims=True)
        sel = jnp.min(jnp.where(v == m, eid, float(n_exp)), axis=0, keepdims=True)
        hot = eid == sel
        vals.append(m)
        idxs.append(sel)
        hots.append(hot)
        v = jnp.where(hot, -jnp.inf, v)
    ex = [jnp.exp(m - vals[0]) for m in vals]
    tot = ex[0] + ex[1] + ex[2] + ex[3]
    msum = sum(h.astype(F32) for h in hots)
    tri = (lax.broadcasted_iota(I32, (tm, tm), 0) < lax.broadcasted_iota(I32, (tm, tm), 1))
    before = jnp.dot(msum.astype(BF16), tri.astype(BF16), preferred_element_type=F32) + carry[:, 0:1]
    for k in range(TOP_K):
        idx_ref[k:k + 1, :] = idxs[k].astype(I32)
        wt_ref[k:k + 1, :] = ex[k] / tot
        rank_ref[k:k + 1, :] = jnp.sum(jnp.where(hots[k], before, 0.0), axis=0,
                                       keepdims=True).astype(I32)
    carry[...] = carry[...] + jnp.sum(msum, axis=1, keepdims=True)
    cnt_ref[...] = carry[...].astype(I32)


def _router(x1, g, sh, sc, w_router, b_router, tm=512):
    b, s, d = x1.shape
    t = b * s
    e = w_router.shape[1]
    wd = d // 2
    wr = jnp.zeros((d, LANES), F32).at[:, :e].set(w_router.astype(F32))
    br = jnp.full((1, LANES), -1e30, F32).at[0, :e].set(b_router.astype(F32))
    spb = s // tm
    return pl.pallas_call(
        functools.partial(_router_kernel, n_exp=e),
        out_shape=[jax.ShapeDtypeStruct((t, wd), U32),
                   jax.ShapeDtypeStruct((TOP_K, t), I32),
                   jax.ShapeDtypeStruct((TOP_K, t), F32),
                   jax.ShapeDtypeStruct((TOP_K, t), I32),
                   jax.ShapeDtypeStruct((e, LANES), I32)],
        grid=(b, spb),
        in_specs=[pl.BlockSpec((1, tm, d), lambda bi, i: (bi, i, 0)),
                  _const_spec((1, d)),
                  pl.BlockSpec((1, 1, d), lambda bi, i: (bi, 0, 0)),
                  pl.BlockSpec((1, 1, d), lambda bi, i: (bi, 0, 0)),
                  _const_spec((d, LANES)),
                  _const_spec((1, LANES))],
        out_specs=[pl.BlockSpec((tm, wd), lambda bi, i: (bi * spb + i, 0)),
                   pl.BlockSpec((TOP_K, tm), lambda bi, i: (0, bi * spb + i)),
                   pl.BlockSpec((TOP_K, tm), lambda bi, i: (0, bi * spb + i)),
                   pl.BlockSpec((TOP_K, tm), lambda bi, i: (0, bi * spb + i)),
                   pl.BlockSpec((e, LANES), lambda bi, i: (0, 0))],
        scratch_shapes=[pltpu.VMEM((e, LANES), F32)],
        compiler_params=_cparams(("arbitrary", "arbitrary")),
        name="router",
    )(x1, g.reshape(1, d), sh.reshape(b, 1, d), sc.reshape(b, 1, d), wr, br)


def _dispatch_kernel(dest_ref, zflag_ref, hp_ref, xs_ref, zbuf, sem, zsem, *, n_tok):
    tm = hp_ref.shape[0]
    i = pl.program_id(0)

    @pl.when(i == 0)
    def _():
        zbuf[...] = jnp.zeros_like(zbuf)

        def zcopy(q):
            r0 = pl.multiple_of(q * SUB, SUB)
            return pltpu.make_async_copy(zbuf, xs_ref.at[pl.ds(r0, SUB)], zsem)

        def zstart(q, carry):
            pl.when(zflag_ref[q] != 0)(lambda: zcopy(q).start())
            return carry

        def zwait(q, carry):
            pl.when(zflag_ref[q] != 0)(lambda: zcopy(q).wait())
            return carry

        lax.fori_loop(0, zflag_ref.shape[0], zstart, 0)
        lax.fori_loop(0, zflag_ref.shape[0], zwait, 0)

    def issue(t, carry):
        for k in range(TOP_K):
            dst = dest_ref[k * n_tok + i * tm + t]
            pltpu.make_async_copy(hp_ref.at[pl.ds(t, 1)], xs_ref.at[pl.ds(dst, 1)],
                                  sem).start(priority=k % 2)
        return carry

    lax.fori_loop(0, tm, issue, 0, unroll=4)
    for k in range(TOP_K):
        pltpu.make_async_copy(hp_ref, xs_ref.at[pl.ds(0, tm)], sem).wait()


def _dispatch(dest_flat, zflag, hp, n_rows, tm=256):
    t, wd = hp.shape
    return pl.pallas_call(
        functools.partial(_dispatch_kernel, n_tok=t),
        out_shape=jax.ShapeDtypeStruct((n_rows, wd), U32),
        grid_spec=pltpu.PrefetchScalarGridSpec(
            num_scalar_prefetch=2,
            grid=(t // tm,),
            in_specs=[pl.BlockSpec((tm, wd), lambda i, *_: (i, 0))],
            out_specs=pl.BlockSpec(memory_space=pl.ANY),
            scratch_shapes=[pltpu.VMEM((SUB, wd), U32),
                            pltpu.SemaphoreType.DMA,
                            pltpu.SemaphoreType.DMA]),
        compiler_params=_cparams(("arbitrary",)),
        name="dispatch",
    )(dest_flat, zflag, hp)


def _experts_kernel(sbe_ref, nsub_ref, nused_ref, x_ref, b1_ref, b2_ref, w1_hbm, w2_hbm, o_ref,
                    xb, actb, yb, wt, sem):
    s = pl.program_id(0)
    n = nsub_ref[s]
    nused = nused_ref[0]
    nj, _, tf = actb.shape
    nn, _, pair = yb.shape
    d = xb.shape[1]
    f = nj * tf
    nslot = wt.shape[0]
    ntile = nj + nn
    per = SUPER // SUB

    def a_copies(e, j, slot):
        return [pltpu.make_async_copy(
            w1_hbm.at[e, :, pl.ds(pl.multiple_of(part * f + j * tf, tf), tf)],
            wt.at[slot, part, 0:d, 0:tf], sem.at[slot]) for part in range(2)]

    def b_copies(e, c, slot):
        return [pltpu.make_async_copy(
            w2_hbm.at[e, :, pl.ds(pl.multiple_of((2 * c + part) * pair, pair), pair)],
            wt.at[slot, part, 0:f, 0:pair], sem.at[slot]) for part in range(2)]

    def slot_of(sb, q):
        return lax.rem(sb * ntile + q, nslot)

    def start_tile(sb, q):
        e = sbe_ref[sb]
        slot = slot_of(sb, q)

        @pl.when(q < nj)
        def _():
            for cp in a_copies(e, q, slot):
                cp.start()

        @pl.when(q >= nj)
        def _():
            for cp in b_copies(e, q - nj, slot):
                cp.start()

    def prefetch(q):
        qq = q + nslot - 1

        @pl.when(qq < ntile)
        def _():
            start_tile(s, qq)

        @pl.when((qq >= ntile) & (s + 1 < nused))
        def _():
            start_tile(s + 1, qq - ntile)

    @pl.when((s == 0) & (nused > 0))
    def _():
        for q in range(nslot - 1):
            start_tile(0, jnp.int32(q))

    def run(v):
        m = v * SUB
        e = sbe_ref[s]
        for r in range(v):
            rows = slice(r * SUB, (r + 1) * SUB)
            xb[rows, :] = _unpack_rows(x_ref[rows, :]).astype(BF16)

        def a_body(j, carry):
            slot = slot_of(s, j)
            for cp in a_copies(e, j, slot):
                cp.wait()
            prefetch(j)
            x = xb[0:m, :]
            g = jnp.dot(x, wt[slot, 0, 0:d, 0:tf].astype(BF16), preferred_element_type=F32) + b1_ref[0, j]
            l = jnp.dot(x, wt[slot, 1, 0:d, 0:tf].astype(BF16), preferred_element_type=F32) + b1_ref[0, nj + j]
            xg = jnp.minimum(g, SWIGLU_LIMIT)
            xl = jnp.clip(l, -SWIGLU_LIMIT, SWIGLU_LIMIT)
            act = xg * jax.nn.sigmoid(SWIGLU_ALPHA * xg) * (xl + 1.0)
            actb[j, 0:m, :] = act.astype(BF16)
            return carry

        lax.fori_loop(0, nj, a_body, 0)

        def b_body(c, carry):
            slot = slot_of(s, nj + c)
            for cp in b_copies(e, c, slot):
                cp.wait()
            prefetch(nj + c)
            halves = []
            for part in range(2):
                w = wt[slot, part, 0:f, 0:pair].astype(BF16)
                y = b2_ref[0, 2 * c + part] + jnp.dot(actb[0, 0:m, :], w[0:tf, :],
                                                      preferred_element_type=F32)
                for j in range(1, nj):
                    y = y + jnp.dot(actb[j, 0:m, :], w[j * tf:(j + 1) * tf, :],
                                    preferred_element_type=F32)
                halves.append(y)
            yb[c, 0:m, :] = _pack_rows(jnp.concatenate(halves, axis=1))
            return carry

        lax.fori_loop(0, nn, b_body, 0)

        for c in range(nn):
            o_ref[0:m, c * pair:(c + 1) * pair] = yb[c, 0:m, :]
        if m < SUPER:
            o_ref[m:SUPER, :] = jnp.zeros((SUPER - m, o_ref.shape[1]), U32)

    for v in range(1, per + 1):
        pl.when((s < nused) & (n == v))(functools.partial(run, v))

    @pl.when(s >= nused)
    def _():
        o_ref[...] = jnp.zeros(o_ref.shape, U32)


def _experts(sbe, nsub, nused, xs, w1, b1, w2, b2, n_super, tf=256, nslot=3):
    n_exp, d, f2 = w1.shape
    f = f2 // 2
    tf = min(tf, f)
    pair = _pair_width(d)
    nj = f // tf
    nn = d // (2 * pair)
    wd = xs.shape[1]

    def x_map(s, sbe, nsub, nused):
        return (jnp.minimum(s, nused[0] - 1), 0)

    return pl.pallas_call(
        _experts_kernel,
        out_shape=jax.ShapeDtypeStruct((n_super * SUPER, wd), U32),
        grid_spec=pltpu.PrefetchScalarGridSpec(
            num_scalar_prefetch=3,
            grid=(n_super,),
            in_specs=[pl.BlockSpec((SUPER, wd), x_map),
                      pl.BlockSpec((1, 2 * nj, 1, tf), lambda s, sbe, nsub, nu: (sbe[s], 0, 0, 0)),
                      pl.BlockSpec((1, 2 * nn, 1, pair), lambda s, sbe, nsub, nu: (sbe[s], 0, 0, 0)),
                      pl.BlockSpec(memory_space=pl.ANY),
                      pl.BlockSpec(memory_space=pl.ANY)],
            out_specs=pl.BlockSpec((SUPER, wd), lambda s, *_: (s, 0)),
            scratch_shapes=[pltpu.VMEM((SUPER, d), BF16),
                            pltpu.VMEM((nj, SUPER, tf), BF16),
                            pltpu.VMEM((nn, SUPER, pair), U32),
                            pltpu.VMEM((nslot, 2, max(d, f), max(tf, pair)), F32),
                            pltpu.SemaphoreType.DMA((nslot,))]),
        compiler_params=_cparams(("arbitrary",)),
        name="experts",
    )(sbe, nsub, nused, xs, b1.reshape(n_exp, 2 * nj, 1, tf), b2.reshape(n_exp, 2 * nn, 1, pair), w1, w2)


def _combine_kernel(dest_ref, x_ref, wt_ref, gt_ref, g_ref, sh_ref, sc_ref, ys_ref, o_ref,
                    gbuf, sem, *, n_tok, tiles_per_batch, apply_norm):
    tm = x_ref.shape[1]
    i = pl.program_id(0) * tiles_per_batch + pl.program_id(1)
    n_tiles = pl.num_programs(0) * tiles_per_batch
    slot = lax.rem(i, 2)

    def gather(tile, buf):
        def issue(t, carry):
            for k in range(TOP_K):
                src = dest_ref[k * n_tok + tile * tm + t]
                pltpu.make_async_copy(ys_ref.at[pl.ds(src, 1)], gbuf.at[buf, k, pl.ds(t, 1)],
                                      sem.at[buf]).start(priority=k % 2)
            return carry

        lax.fori_loop(0, tm, issue, 0, unroll=4)

    @pl.when(i == 0)
    def _():
        gather(i, slot)

    @pl.when(i + 1 < n_tiles)
    def _():
        gather(i + 1, 1 - slot)

    for k in range(TOP_K):
        pltpu.make_async_copy(ys_ref.at[pl.ds(0, tm)], gbuf.at[slot, k], sem.at[slot]).wait()
    wt = wt_ref[...]
    ffn = wt[:, 0:1] * _unpack_rows(gbuf[slot, 0])
    for k in range(1, TOP_K):
        ffn = ffn + wt[:, k:k + 1] * _unpack_rows(gbuf[slot, k])
    x2 = x_ref[0] + gt_ref[0] * ffn
    o_ref[0] = _rms_mod(x2, g_ref[...], sh_ref[0], sc_ref[0]) if apply_norm else x2


def _combine(dest_flat, x1, wt_t, gt, g, sh, sc, ys, apply_norm, tm=256):
    b, s, d = x1.shape
    t = b * s
    spb = s // tm
    wd = ys.shape[1]
    return pl.pallas_call(
        functools.partial(_combine_kernel, n_tok=t, tiles_per_batch=spb, apply_norm=apply_norm),
        out_shape=jax.ShapeDtypeStruct((b, s, d), F32),
        grid_spec=pltpu.PrefetchScalarGridSpec(
            num_scalar_prefetch=1,
            grid=(b, spb),
            in_specs=[pl.BlockSpec((1, tm, d), lambda bi, i, *_: (bi, i, 0)),
                      pl.BlockSpec((tm, TOP_K), lambda bi, i, *_: (bi * spb + i, 0)),
                      pl.BlockSpec((1, 1, d), lambda bi, i, *_: (bi, 0, 0)),
                      pl.BlockSpec((1, d), lambda bi, i, *_: (0, 0)),
                      pl.BlockSpec((1, 1, d), lambda bi, i, *_: (bi, 0, 0)),
                      pl.BlockSpec((1, 1, d), lambda bi, i, *_: (bi, 0, 0)),
                      pl.BlockSpec(memory_space=pl.ANY)],
            out_specs=pl.BlockSpec((1, tm, d), lambda bi, i, *_: (bi, i, 0)),
            scratch_shapes=[pltpu.VMEM((2, TOP_K, tm, wd), U32),
                            pltpu.SemaphoreType.DMA((2,))]),
        compiler_params=_cparams(("arbitrary", "arbitrary")),
        name="combine",
    )(dest_flat, x1, wt_t, gt.reshape(b, 1, d), g.reshape(1, d), sh.reshape(b, 1, d),
      sc.reshape(b, 1, d), ys)


def _plan(idx, rank, counts, n_super):
    n_exp = counts.shape[0]
    nsb = (counts + SUPER - 1) // SUPER
    rps = (counts + jnp.maximum(nsb, 1) - 1) // jnp.maximum(nsb, 1)
    rps = jnp.maximum(rps, 1).astype(I32)
    sb_end = jnp.cumsum(nsb)
    sb_start = sb_end - nsb
    hot = idx[None] == jnp.arange(n_exp, dtype=I32)[:, None, None]
    t_start = jnp.sum(jnp.where(hot, sb_start.astype(I32)[:, None, None], 0), axis=0)
    t_rps = jnp.sum(jnp.where(hot, rps[:, None, None], 0), axis=0)
    dest = (t_start + rank // t_rps) * SUPER + rank % t_rps
    s_ids = jnp.arange(n_super, dtype=I32)
    nused = sb_end[-1]
    sbe = jnp.minimum(jnp.searchsorted(sb_end, jnp.minimum(s_ids, nused - 1), side='right'),
                      n_exp - 1).astype(I32)
    per = SUPER // SUB
    q_ids = jnp.arange(n_super * per, dtype=I32)
    q_s = q_ids // per
    q_e = sbe[q_s]
    in_sb = jnp.minimum(rps[q_e], counts[q_e] - (q_s - sb_start[q_e]) * rps[q_e])
    left = in_sb - (q_ids % per) * SUB
    vrows = jnp.where(q_s < nused, jnp.clip(left, 0, SUB), 0)
    nsub = jnp.sum((vrows > 0).reshape(n_super, per), axis=1).astype(I32)
    zflag = (vrows < SUB).astype(I32)
    return dest.reshape(-1).astype(I32), sbe, nsub, nused.reshape(1).astype(I32), zflag


def kernel(x, c, ada_w, ada_b, norm1_g, w_in, pool_w, pool_scale, w_pool_out, ssm_lam_re, ssm_lam_im,
           ssm_log_dt, ssm_b_re, ssm_b_im, ssm_c_re, ssm_c_im, ssm_d, w_glu, b_glu, w_out, norm2_g,
           w_router, b_router, w1, b1, w2, b2, final_ada_w, final_ada_b, final_norm_g):
    b, s, d = x.shape
    t = b * s
    depth = ada_w.shape[0]
    n_exp = w_router.shape[-1]
    pwid = w_pool_out.shape[1]
    n_super = (t * TOP_K) // SUPER + n_exp

    c8 = jnp.zeros((8, d), F32).at[:b].set(c.astype(F32))
    fmod = _ada(c8, final_ada_w, final_ada_b)[:b]
    sh_o, sc_o = jnp.split(fmod, 2, axis=-1)
    for l in range(depth):
        mod = _ada(c8, ada_w[l], ada_b[l])[:b]
        sh_m, sc_m, gt_m, sh_f, sc_f, gt_f = jnp.split(mod, 6, axis=-1)
        z = _inproj(x, norm1_g[l], sh_m, sc_m, w_in[l].astype(BF16))
        ops = _s5_weights(ssm_lam_re[l], ssm_lam_im[l], ssm_log_dt[l], ssm_b_re[l], ssm_b_im[l],
                          ssm_c_re[l], ssm_c_im[l])
        ys = _s5(z, pwid, *ops, ssm_d[l])
        x = _mixout(z, ys, x, gt_m, pool_w[l].astype(BF16), pool_scale[l], w_pool_out[l].astype(BF16),
                    w_glu[l].astype(BF16), b_glu[l], w_out[l].astype(BF16))
        hp, idx, wt, rank, cnt = _router(x, norm2_g[l], sh_f, sc_f, w_router[l], b_router[l])
        dest, sbe, nsub, nused, zflag = _plan(idx, rank, cnt[:, 0], n_super)
        xs = _dispatch(dest, zflag, hp, n_super * SUPER)
        ye = _experts(sbe, nsub, nused, xs, w1[l], b1[l], w2[l], b2[l], n_super)
        last = l == depth - 1
        x = _combine(dest, x, wt.T, gt_f, final_norm_g, sh_o, sc_o, ye, apply_norm=last)
    return x
```

```python
import functools
import math

import jax
import jax.numpy as jnp
from jax import lax
from jax.experimental import pallas as pl
from jax.experimental.pallas import tpu as pltpu

F32 = jnp.float32
BF16 = jnp.bfloat16
I32 = jnp.int32
U32 = jnp.uint32

RMS_EPS = 1e-6
POOL_WINDOWS = (2, 4, 8, 16)
SSM_GROUP_DIM = 16
SSM_STATE = 64
LAMBDA_RE_MAX = -1e-4
TOP_K = 4
SWIGLU_ALPHA = 1.702
SWIGLU_LIMIT = 7.0

LANES = 128
V7X_VMEM_LIMIT = 56 * 1024 * 1024

CHUNK = 16
OCT = LANES // SSM_GROUP_DIM
SUB = 256
SUPER = 4 * SUB


def _cparams(sem, vmem=V7X_VMEM_LIMIT):
    return pltpu.CompilerParams(dimension_semantics=sem, vmem_limit_bytes=vmem)


def _const_spec(shape):
    nd = len(shape)
    return pl.BlockSpec(shape, lambda *_: (0,) * nd, pipeline_mode=pl.Buffered(1))


def _rms_mod(x, g, sh, sc):
    y = x * lax.rsqrt(jnp.mean(x * x, axis=-1, keepdims=True) + RMS_EPS) * g
    return y * (1.0 + sc) + sh


def _pair_width(d):
    return min(2 * LANES, d // 2)


def _pack_rows(h):
    d = h.shape[1]
    pair = _pair_width(d)
    hb = lax.bitcast_convert_type(h.astype(BF16).astype(F32), U32)
    words = []
    for n in range(d // (2 * pair)):
        lo = hb[:, 2 * pair * n:2 * pair * n + pair]
        hi = hb[:, 2 * pair * n + pair:2 * pair * (n + 1)]
        words.append((lo >> 16) | (hi & jnp.uint32(0xFFFF0000)))
    return jnp.concatenate(words, axis=1)


def _unpack_rows(w):
    d = 2 * w.shape[1]
    pair = _pair_width(d)
    cols = []
    for n in range(d // (2 * pair)):
        wn = w[:, pair * n:pair * (n + 1)]
        cols.append(lax.bitcast_convert_type(wn << 16, F32))
        cols.append(lax.bitcast_convert_type(wn & jnp.uint32(0xFFFF0000), F32))
    return jnp.concatenate(cols, axis=1)


def _ada_kernel(c_ref, w_ref, b_ref, o_ref):
    c = c_ref[...]
    ca = c * jax.nn.sigmoid(c)
    o_ref[...] = jnp.dot(ca.astype(BF16), w_ref[...].astype(BF16),
                         preferred_element_type=F32) + b_ref[...]


def _ada(c8, w, b, tn=1024):
    d, n = w.shape
    tn = math.gcd(tn, n)
    return pl.pallas_call(
        _ada_kernel,
        out_shape=jax.ShapeDtypeStruct((c8.shape[0], n), F32),
        grid=(n // tn,),
        in_specs=[pl.BlockSpec(c8.shape, lambda j: (0, 0)),
                  pl.BlockSpec((d, tn), lambda j: (0, j)),
                  pl.BlockSpec((1, tn), lambda j: (0, j))],
        out_specs=pl.BlockSpec((c8.shape[0], tn), lambda j: (0, j)),
        compiler_params=_cparams(("arbitrary",)),
        name="ada",
    )(c8, w, b.reshape(1, n))


def _inproj_kernel(x_ref, g_ref, sh_ref, sc_ref, w_ref, z_ref, *, ncol):
    h = _rms_mod(x_ref[0], g_ref[...], sh_ref[0], sc_ref[0]).astype(BF16)
    n = w_ref.shape[1]
    for c in range(n // ncol):
        z_ref[0, :, c * ncol:(c + 1) * ncol] = jnp.dot(
            h, w_ref[:, c * ncol:(c + 1) * ncol], preferred_element_type=F32).astype(BF16)


def _inproj(x, g, sh, sc, w, tm=512, ncol=1024):
    b, s, d = x.shape
    n = w.shape[1]
    ncol = math.gcd(ncol, n)
    return pl.pallas_call(
        functools.partial(_inproj_kernel, ncol=ncol),
        out_shape=jax.ShapeDtypeStruct((b, s, n), BF16),
        grid=(b, s // tm),
        in_specs=[pl.BlockSpec((1, tm, d), lambda bi, i: (bi, i, 0)),
                  _const_spec((1, d)),
                  pl.BlockSpec((1, 1, d), lambda bi, i: (bi, 0, 0)),
                  pl.BlockSpec((1, 1, d), lambda bi, i: (bi, 0, 0)),
                  _const_spec((d, n))],
        out_specs=pl.BlockSpec((1, tm, n), lambda bi, i: (bi, i, 0)),
        compiler_params=_cparams(("arbitrary", "arbitrary")),
        name="inproj",
    )(x, g.reshape(1, d), sh.reshape(b, 1, d), sc.reshape(b, 1, d), w)


def _s5_weights(lam_re, lam_im, log_dt, b_re, b_im, c_re, c_im):
    hp = lax.Precision.HIGHEST
    g_all, p = lam_re.shape
    h = b_re.shape[-1]
    no = g_all // OCT
    dt = jnp.exp(log_dt.astype(F32))[:, None]
    lre = jnp.minimum(lam_re.astype(F32), LAMBDA_RE_MAX)
    lim = lam_im.astype(F32)
    mag = jnp.exp(lre * dt)
    ang = lim * dt
    ab_re = mag * jnp.cos(ang)
    ab_im = mag * jnp.sin(ang)
    den = lre * lre + lim * lim
    nr = ab_re - 1.0
    ni = ab_im
    f_re = (nr * lre + ni * lim) / den
    f_im = (ni * lre - nr * lim) / den
    br_, bi_ = b_re.astype(F32), b_im.astype(F32)
    bb_re = f_re[..., None] * br_ - f_im[..., None] * bi_
    bb_im = f_re[..., None] * bi_ + f_im[..., None] * br_
    cr, ci = c_re.astype(F32), c_im.astype(F32)
    j = jnp.arange(CHUNK + 1, dtype=F32)[:, None, None]
    pw_mag = jnp.exp(j * (lre * dt)[None])
    pw_re = pw_mag * jnp.cos(j * ang[None])
    pw_im = pw_mag * jnp.sin(j * ang[None])
    cp_re = cr[None] * pw_re[:, :, None, :] - ci[None] * pw_im[:, :, None, :]
    cp_im = cr[None] * pw_im[:, :, None, :] + ci[None] * pw_re[:, :, None, :]
    assert 2 * p == LANES and h * OCT == LANES
    tk = jnp.arange(CHUNK)

    klag = (jnp.einsum('jghp,gpi->jgih', cp_re[:CHUNK], bb_re, precision=hp)
            - jnp.einsum('jghp,gpi->jgih', cp_im[:CHUNK], bb_im, precision=hp))
    lag_t = klag.reshape(CHUNK, g_all * h, h)
    bt_re = jnp.swapaxes(bb_re, 1, 2)[None]
    bt_im = jnp.swapaxes(bb_im, 1, 2)[None]
    pwr = pw_re[CHUNK - 1 - tk][:, :, None, :]
    pwi = pw_im[CHUNK - 1 - tk][:, :, None, :]
    slabs = jnp.stack([pwr * bt_re - pwi * bt_im, pwr * bt_im + pwi * bt_re,
                       cp_re[1:], -cp_im[1:]]).reshape(4, CHUNK, g_all * h, p)
    are = pw_re[CHUNK].reshape(no, OCT, 1, p)
    aim = pw_im[CHUNK].reshape(no, OCT, 1, p)
    a1 = jnp.concatenate([are, are], axis=2).reshape(no, 1, OCT * LANES)
    a2 = jnp.concatenate([-aim, aim], axis=2).reshape(no, 1, OCT * LANES)
    return lag_t, slabs, a1, a2


def _gelu_tanh(x):
    return 0.5 * x * (1.0 + jnp.tanh(math.sqrt(2.0 / math.pi) * (x + 0.044715 * x * x * x)))


def _s5_kernel(u_ref, lag_ref, slab_ref, a1_ref, a2_ref, d_ref, y_ref,
               uscr, sscr, pscr, yscr, state, t_op, b_op, c_op):
    nb, tm, _ = u_ref.shape
    nc = tm // CHUNK
    nq = a1_ref.shape[-1] // LANES

    @pl.when(pl.program_id(1) == 0)
    def _():
        state[...] = jnp.zeros_like(state)
        rowg = lax.broadcasted_iota(I32, (LANES, LANES), 0) // SSM_GROUP_DIM
        colg = lax.broadcasted_iota(I32, (LANES, LANES), 1) // SSM_GROUP_DIM
        zero = jnp.zeros((LANES, LANES), BF16)
        rep = (lax.broadcasted_iota(I32, (SSM_GROUP_DIM, LANES), 1) % SSM_GROUP_DIM
               == lax.broadcasted_iota(I32, (SSM_GROUP_DIM, LANES), 0)).astype(BF16)
        lag = []
        for j in range(CHUNK):
            wide = jnp.dot(lag_ref[j].astype(BF16), rep, preferred_element_type=F32).astype(BF16)
            lag.append(jnp.where(rowg == colg, wide, zero))
        for k in range(CHUNK):
            in_k = jnp.concatenate([slab_ref[0, k], slab_ref[1, k]], axis=1).astype(BF16)
            out_k = jnp.concatenate([slab_ref[2, k], slab_ref[3, k]], axis=1).T.astype(BF16)
            for q in range(nq):
                b_op[k * LANES:(k + 1) * LANES, q * LANES:(q + 1) * LANES] = jnp.where(rowg == q, in_k, zero)
                c_op[q * LANES:(q + 1) * LANES, k * LANES:(k + 1) * LANES] = jnp.where(colg == q, out_k, zero)
            for t in range(CHUNK):
                t_op[k * LANES:(k + 1) * LANES, t * LANES:(t + 1) * LANES] = lag[t - k] if t >= k else zero

    for bi in range(nb):
        uscr[bi * tm:(bi + 1) * tm, :] = u_ref[bi].astype(F32)
    xk = jnp.concatenate(
        [uscr[pl.ds(k, nb * nc, stride=CHUNK), :] for k in range(CHUNK)], axis=1).astype(BF16)
    rr = nb * nc
    sc_all = jnp.dot(xk, b_op[...], preferred_element_type=F32)
    for q in range(nq):
        sscr[q * rr:(q + 1) * rr, :] = sc_all[:, q * LANES:(q + 1) * LANES]

    a1 = [a1_ref[0, :, q * LANES:(q + 1) * LANES] for q in range(nq)]
    a2 = [a2_ref[0, :, q * LANES:(q + 1) * LANES] for q in range(nq)]
    st = [state[:, q * LANES:(q + 1) * LANES] for q in range(nq)]
    for c in range(nc):
        for q in range(nq):
            pscr[pl.ds(q * rr + c, nb, stride=nc), :] = st[q]
            sc = sscr[pl.ds(q * rr + c, nb, stride=nc), :]
            st[q] = a1[q] * st[q] + a2[q] * pltpu.roll(st[q], LANES // 2, axis=1) + sc
    for q in range(nq):
        state[:, q * LANES:(q + 1) * LANES] = st[q]

    xprev = jnp.concatenate([pscr[q * rr:(q + 1) * rr, :] for q in range(nq)], axis=1)
    y = (jnp.dot(xk, t_op[...], preferred_element_type=F32)
         + jnp.dot(xprev.astype(BF16), c_op[...], preferred_element_type=F32))
    for t in range(CHUNK):
        yscr[pl.ds(t, nb * nc, stride=CHUNK), :] = y[:, t * LANES:(t + 1) * LANES]
    out = _gelu_tanh(yscr[...] + d_ref[...] * uscr[...])
    for bi in range(nb):
        y_ref[bi] = out[bi * tm:(bi + 1) * tm, :].astype(BF16)


def _s5(z, col0, lag_t, slabs, a1, a2, d_skip, tm=1024):
    nb, s, _ = z.shape
    no = a1.shape[0]
    w = no * LANES
    ns = a1.shape[-1]
    tm = min(tm, s)
    rows = nb * tm
    cb0 = col0 // LANES
    lag_spec = pl.BlockSpec((CHUNK, LANES, lag_t.shape[-1]), lambda o, i: (0, o, 0))
    slab_spec = pl.BlockSpec((4, CHUNK, LANES, slabs.shape[-1]), lambda o, i: (0, 0, o, 0))
    return pl.pallas_call(
        _s5_kernel,
        out_shape=jax.ShapeDtypeStruct((nb, s, w), BF16),
        grid=(no, s // tm),
        in_specs=[pl.BlockSpec((nb, tm, LANES), lambda o, i: (0, i, cb0 + o)),
                  lag_spec, slab_spec,
                  pl.BlockSpec((1, 1, ns), lambda o, i: (o, 0, 0)),
                  pl.BlockSpec((1, 1, ns), lambda o, i: (o, 0, 0)),
                  pl.BlockSpec((1, LANES), lambda o, i: (0, o))],
        out_specs=pl.BlockSpec((nb, tm, LANES), lambda o, i: (0, i, o)),
        scratch_shapes=[pltpu.VMEM((rows, LANES), F32),
                        pltpu.VMEM((rows // CHUNK * (ns // LANES), LANES), F32),
                        pltpu.VMEM((rows // CHUNK * (ns // LANES), LANES), F32),
                        pltpu.VMEM((rows, LANES), F32),
                        pltpu.VMEM((nb, ns), F32),
                        pltpu.VMEM((CHUNK * LANES, CHUNK * LANES), BF16),
                        pltpu.VMEM((CHUNK * LANES, ns), BF16),
                        pltpu.VMEM((ns, CHUNK * LANES), BF16)],
        compiler_params=_cparams(("arbitrary", "arbitrary")),
        name="s5",
    )(z, lag_t, slabs, a1, a2, d_skip.reshape(1, w))


def _mixout_kernel(up_ref, halo_ref, ys_ref, gp_ref, gs_ref, x_ref, gt_ref,
                   pw_ref, ps_ref, wpo_ref, wglu_ref, bglu_ref, wout_ref, o_ref, escr):
    tm = up_ref.shape[1]
    i = pl.program_id(1)
    gw = pw_ref.shape[1]
    hal = halo_ref.shape[1]
    d = o_ref.shape[2]
    escr[0:hal, :] = jnp.where(i > 0, halo_ref[0].astype(F32), 0.0)
    escr[hal:hal + tm, :] = up_ref[0].astype(F32)
    pos = (i * tm + lax.broadcasted_iota(I32, (tm, gw), 0) + 1).astype(F32)
    mixed = []
    for g, win in enumerate(POOL_WINDOWS):
        cols = slice(g * gw, (g + 1) * gw)
        cur = escr[hal:hal + tm, cols]
        acc = cur
        for j in range(1, win):
            acc = acc + escr[hal - j:hal - j + tm, cols]
        pooled = acc / jnp.minimum(pos, float(win)) - cur
        mixed.append(jnp.dot(pooled.astype(BF16), pw_ref[g], preferred_element_type=F32))
    mixed = jnp.concatenate(mixed, axis=1) * ps_ref[...]
    y_pool = jnp.dot(mixed.astype(BF16), wpo_ref[...], preferred_element_type=F32)
    glu = jnp.dot(ys_ref[0], wglu_ref[...], preferred_element_type=F32) + bglu_ref[...]
    y_ssm = glu[:, :d] * jax.nn.sigmoid(glu[:, d:])
    merged = (jax.nn.sigmoid(gp_ref[0].astype(F32)) * y_pool
              + jax.nn.sigmoid(gs_ref[0].astype(F32)) * y_ssm)
    mix = jnp.dot(merged.astype(BF16), wout_ref[...], preferred_element_type=F32)
    o_ref[0] = x_ref[0] + gt_ref[0] * mix


def _mixout(z, ys, x, gt, pool_w, pool_scale, w_pool_out, w_glu, b_glu, w_out, tm=256):
    b, s, d = x.shape
    pwid = w_pool_out.shape[0]
    hal = max(POOL_WINDOWS)
    tpb = tm // hal
    gcol = (pwid + ys.shape[2]) // d
    return pl.pallas_call(
        _mixout_kernel,
        out_shape=jax.ShapeDtypeStruct((b, s, d), F32),
        grid=(b, s // tm),
        in_specs=[pl.BlockSpec((1, tm, pwid), lambda bi, i: (bi, i, 0)),
                  pl.BlockSpec((1, hal, pwid), lambda bi, i: (bi, jnp.maximum(i * tpb - 1, 0), 0)),
                  pl.BlockSpec((1, tm, ys.shape[2]), lambda bi, i: (bi, i, 0)),
                  pl.BlockSpec((1, tm, d), lambda bi, i: (bi, i, gcol)),
                  pl.BlockSpec((1, tm, d), lambda bi, i: (bi, i, gcol + 1)),
                  pl.BlockSpec((1, tm, d), lambda bi, i: (bi, i, 0)),
                  pl.BlockSpec((1, 1, d), lambda bi, i: (bi, 0, 0)),
                  _const_spec(pool_w.shape),
                  _const_spec((1, pwid)),
                  _const_spec(w_pool_out.shape),
                  _const_spec(w_glu.shape),
                  _const_spec((1, w_glu.shape[1])),
                  _const_spec(w_out.shape)],
        out_specs=pl.BlockSpec((1, tm, d), lambda bi, i: (bi, i, 0)),
        scratch_shapes=[pltpu.VMEM((hal + tm, pwid), F32)],
        compiler_params=_cparams(("arbitrary", "arbitrary")),
        name="mixout",
    )(z, z, ys, z, z, x, gt.reshape(b, 1, d), pool_w, pool_scale.reshape(1, pwid),
      w_pool_out, w_glu, b_glu.reshape(1, -1), w_out)


def _router_kernel(x_ref, g_ref, sh_ref, sc_ref, wr_ref, br_ref,
                   hp_ref, idx_ref, wt_ref, rank_ref, cnt_ref, carry, *, n_exp):
    tm = x_ref.shape[1]
    first = (pl.program_id(0) == 0) & (pl.program_id(1) == 0)

    @pl.when(first)
    def _():
        carry[...] = jnp.zeros_like(carry)

    h2 = _rms_mod(x_ref[0], g_ref[...], sh_ref[0], sc_ref[0])
    hp_ref[...] = _pack_rows(h2)
    h_hi = h2.astype(BF16)
    h_lo = (h2 - h_hi.astype(F32)).astype(BF16)
    logits = (jnp.dot(h_hi, wr_ref[0], preferred_element_type=F32)
              + jnp.dot(h_hi, wr_ref[1], preferred_element_type=F32)
              + jnp.dot(h_lo, wr_ref[0], preferred_element_type=F32)) + br_ref[...]
    lt = logits.T[:n_exp, :]
    eid = lax.broadcasted_iota(I32, lt.shape, 0).astype(F32)
    vals, idxs, hots = [], [], []
    v = lt
    for _ in range(TOP_K):
        m = jnp.max(v, axis=0, keepdims=True)
        sel = jnp.min(jnp.where(v == m, eid, float(n_exp)), axis=0, keepdims=True)
        hot = eid == sel
        vals.append(m)
        idxs.append(sel)
        hots.append(hot)
        v = jnp.where(hot, -jnp.inf, v)
    ex = [jnp.exp(m - vals[0]) for m in vals]
    tot = ex[0] + ex[1] + ex[2] + ex[3]
    msum = sum(h.astype(F32) for h in hots)
    tri = (lax.broadcasted_iota(I32, (tm, tm), 0) < lax.broadcasted_iota(I32, (tm, tm), 1))
    before = jnp.dot(msum.astype(BF16), tri.astype(BF16), preferred_element_type=F32) + carry[:, 0:1]
    for k in range(TOP_K):
        idx_ref[k:k + 1, :] = idxs[k].astype(I32)
        wt_ref[k:k + 1, :] = ex[k] / tot
        rank_ref[k:k + 1, :] = jnp.sum(jnp.where(hots[k], before, 0.0), axis=0,
                                       keepdims=True).astype(I32)
    carry[...] = carry[...] + jnp.sum(msum, axis=1, keepdims=True)
    cnt_ref[...] = carry[...].astype(I32)


def _router(x1, g, sh, sc, w_router, b_router, tm=512):
    b, s, d = x1.shape
    t = b * s
    e = w_router.shape[1]
    wd = d // 2
    wr = jnp.zeros((d, LANES), F32).at[:, :e].set(w_router.astype(F32))
    wr_hi = wr.astype(BF16)
    wr = jnp.stack([wr_hi, (wr - wr_hi.astype(F32)).astype(BF16)])
    br = jnp.full((1, LANES), -1e30, F32).at[0, :e].set(b_router.astype(F32))
    spb = s // tm
    return pl.pallas_call(
        functools.partial(_router_kernel, n_exp=e),
        out_shape=[jax.ShapeDtypeStruct((t, wd), U32),
                   jax.ShapeDtypeStruct((TOP_K, t), I32),
                   jax.ShapeDtypeStruct((TOP_K, t), F32),
                   jax.ShapeDtypeStruct((TOP_K, t), I32),
                   jax.ShapeDtypeStruct((e, LANES), I32)],
        grid=(b, spb),
        in_specs=[pl.BlockSpec((1, tm, d), lambda bi, i: (bi, i, 0)),
                  _const_spec((1, d)),
                  pl.BlockSpec((1, 1, d), lambda bi, i: (bi, 0, 0)),
                  pl.BlockSpec((1, 1, d), lambda bi, i: (bi, 0, 0)),
                  _const_spec((2, d, LANES)),
                  _const_spec((1, LANES))],
        out_specs=[pl.BlockSpec((tm, wd), lambda bi, i: (bi * spb + i, 0)),
                   pl.BlockSpec((TOP_K, tm), lambda bi, i: (0, bi * spb + i)),
                   pl.BlockSpec((TOP_K, tm), lambda bi, i: (0, bi * spb + i)),
                   pl.BlockSpec((TOP_K, tm), lambda bi, i: (0, bi * spb + i)),
                   pl.BlockSpec((e, LANES), lambda bi, i: (0, 0))],
        scratch_shapes=[pltpu.VMEM((e, LANES), F32)],
        compiler_params=_cparams(("arbitrary", "arbitrary")),
        name="router",
    )(x1, g.reshape(1, d), sh.reshape(b, 1, d), sc.reshape(b, 1, d), wr, br)


def _dispatch_kernel(dest_ref, zflag_ref, hp_ref, xs_ref, zbuf, sem, zsem, *, n_tok):
    tm = hp_ref.shape[0]
    i = pl.program_id(0)

    @pl.when(i == 0)
    def _():
        zbuf[...] = jnp.zeros_like(zbuf)

        def zcopy(q):
            r0 = pl.multiple_of(q * SUB, SUB)
            return pltpu.make_async_copy(zbuf, xs_ref.at[pl.ds(r0, SUB)], zsem)

        def zstart(q, carry):
            pl.when(zflag_ref[q] != 0)(lambda: zcopy(q).start())
            return carry

        def zwait(q, carry):
            pl.when(zflag_ref[q] != 0)(lambda: zcopy(q).wait())
            return carry

        lax.fori_loop(0, zflag_ref.shape[0], zstart, 0)
        lax.fori_loop(0, zflag_ref.shape[0], zwait, 0)

    def issue(t, carry):
        for k in range(TOP_K):
            dst = dest_ref[k * n_tok + i * tm + t]
            pltpu.make_async_copy(hp_ref.at[pl.ds(t, 1)], xs_ref.at[pl.ds(dst, 1)],
                                  sem).start(priority=k % 2)
        return carry

    lax.fori_loop(0, tm, issue, 0, unroll=4)
    for k in range(TOP_K):
        pltpu.make_async_copy(hp_ref, xs_ref.at[pl.ds(0, tm)], sem).wait()


def _dispatch(dest_flat, zflag, hp, n_rows, tm=256):
    t, wd = hp.shape
    return pl.pallas_call(
        functools.partial(_dispatch_kernel, n_tok=t),
        out_shape=jax.ShapeDtypeStruct((n_rows, wd), U32),
        grid_spec=pltpu.PrefetchScalarGridSpec(
            num_scalar_prefetch=2,
            grid=(t // tm,),
            in_specs=[pl.BlockSpec((tm, wd), lambda i, *_: (i, 0))],
            out_specs=pl.BlockSpec(memory_space=pl.ANY),
            scratch_shapes=[pltpu.VMEM((SUB, wd), U32),
                            pltpu.SemaphoreType.DMA,
                            pltpu.SemaphoreType.DMA]),
        compiler_params=_cparams(("arbitrary",)),
        name="dispatch",
    )(dest_flat, zflag, hp)


def _experts_kernel(sbe_ref, nsub_ref, nused_ref, x_ref, b1_ref, b2_ref, w1_hbm, w2_hbm, o_ref,
                    xb, actb, yb, wt, sem):
    s = pl.program_id(0)
    n = nsub_ref[s]
    nused = nused_ref[0]
    nj, _, tf = actb.shape
    nn, _, pair = yb.shape
    d = xb.shape[1]
    f = nj * tf
    nslot = wt.shape[0]
    ntile = nj + nn
    per = SUPER // SUB

    def a_copies(e, j, slot):
        return [pltpu.make_async_copy(
            w1_hbm.at[e, :, pl.ds(pl.multiple_of(part * f + j * tf, tf), tf)],
            wt.at[slot, part, 0:d, 0:tf], sem.at[slot]) for part in range(2)]

    def b_copies(e, c, slot):
        return [pltpu.make_async_copy(
            w2_hbm.at[e, :, pl.ds(pl.multiple_of((2 * c + part) * pair, pair), pair)],
            wt.at[slot, part, 0:f, 0:pair], sem.at[slot]) for part in range(2)]

    def slot_of(sb, q):
        return lax.rem(sb * ntile + q, nslot)

    def start_tile(sb, q):
        e = sbe_ref[sb]
        slot = slot_of(sb, q)

        @pl.when(q < nj)
        def _():
            for cp in a_copies(e, q, slot):
                cp.start()

        @pl.when(q >= nj)
        def _():
            for cp in b_copies(e, q - nj, slot):
                cp.start()

    def prefetch(q):
        qq = q + nslot - 1

        @pl.when(qq < ntile)
        def _():
            start_tile(s, qq)

        @pl.when((qq >= ntile) & (s + 1 < nused))
        def _():
            start_tile(s + 1, qq - ntile)

    @pl.when((s == 0) & (nused > 0))
    def _():
        for q in range(nslot - 1):
            start_tile(0, jnp.int32(q))

    def run(v):
        m = v * SUB
        e = sbe_ref[s]
        for r in range(v):
            rows = slice(r * SUB, (r + 1) * SUB)
            xb[rows, :] = _unpack_rows(x_ref[rows, :]).astype(BF16)

        def a_body(j, carry):
            slot = slot_of(s, j)
            for cp in a_copies(e, j, slot):
                cp.wait()
            prefetch(j)
            x = xb[0:m, :]
            g = jnp.dot(x, wt[slot, 0, 0:d, 0:tf].astype(BF16), preferred_element_type=F32) + b1_ref[0, j]
            l = jnp.dot(x, wt[slot, 1, 0:d, 0:tf].astype(BF16), preferred_element_type=F32) + b1_ref[0, nj + j]
            xg = jnp.minimum(g, SWIGLU_LIMIT)
            xl = jnp.clip(l, -SWIGLU_LIMIT, SWIGLU_LIMIT)
            act = xg * jax.nn.sigmoid(SWIGLU_ALPHA * xg) * (xl + 1.0)
            actb[j, 0:m, :] = act.astype(BF16)
            return carry

        lax.fori_loop(0, nj, a_body, 0)

        def b_body(c, carry):
            slot = slot_of(s, nj + c)
            for cp in b_copies(e, c, slot):
                cp.wait()
            prefetch(nj + c)
            halves = []
            for part in range(2):
                w = wt[slot, part, 0:f, 0:pair].astype(BF16)
                y = b2_ref[0, 2 * c + part] + jnp.dot(actb[0, 0:m, :], w[0:tf, :],
                                                      preferred_element_type=F32)
                for j in range(1, nj):
                    y = y + jnp.dot(actb[j, 0:m, :], w[j * tf:(j + 1) * tf, :],
                                    preferred_element_type=F32)
                halves.append(y)
            yb[c, 0:m, :] = _pack_rows(jnp.concatenate(halves, axis=1))
            return carry

        lax.fori_loop(0, nn, b_body, 0)

        for c in range(nn):
            o_ref[0:m, c * pair:(c + 1) * pair] = yb[c, 0:m, :]
        if m < SUPER:
            o_ref[m:SUPER, :] = jnp.zeros((SUPER - m, o_ref.shape[1]), U32)

    for v in range(1, per + 1):
        pl.when((s < nused) & (n == v))(functools.partial(run, v))

    @pl.when(s >= nused)
    def _():
        o_ref[...] = jnp.zeros(o_ref.shape, U32)


def _experts(sbe, nsub, nused, xs, w1, b1, w2, b2, n_super, tf=256, nslot=3):
    n_exp, d, f2 = w1.shape
    f = f2 // 2
    tf = min(tf, f)
    pair = _pair_width(d)
    nj = f // tf
    nn = d // (2 * pair)
    wd = xs.shape[1]

    def x_map(s, sbe, nsub, nused):
        return (jnp.minimum(s, nused[0] - 1), 0)

    return pl.pallas_call(
        _experts_kernel,
        out_shape=jax.ShapeDtypeStruct((n_super * SUPER, wd), U32),
        grid_spec=pltpu.PrefetchScalarGridSpec(
            num_scalar_prefetch=3,
            grid=(n_super,),
            in_specs=[pl.BlockSpec((SUPER, wd), x_map),
                      pl.BlockSpec((1, 2 * nj, 1, tf), lambda s, sbe, nsub, nu: (sbe[s], 0, 0, 0)),
                      pl.BlockSpec((1, 2 * nn, 1, pair), lambda s, sbe, nsub, nu: (sbe[s], 0, 0, 0)),
                      pl.BlockSpec(memory_space=pl.ANY),
                      pl.BlockSpec(memory_space=pl.ANY)],
            out_specs=pl.BlockSpec((SUPER, wd), lambda s, *_: (s, 0)),
            scratch_shapes=[pltpu.VMEM((SUPER, d), BF16),
                            pltpu.VMEM((nj, SUPER, tf), BF16),
                            pltpu.VMEM((nn, SUPER, pair), U32),
                            pltpu.VMEM((nslot, 2, max(d, f), max(tf, pair)), F32),
                            pltpu.SemaphoreType.DMA((nslot,))]),
        compiler_params=_cparams(("arbitrary",)),
        name="experts",
    )(sbe, nsub, nused, xs, b1.reshape(n_exp, 2 * nj, 1, tf), b2.reshape(n_exp, 2 * nn, 1, pair), w1, w2)


def _combine_kernel(dest_ref, x_ref, wt_ref, gt_ref, g_ref, sh_ref, sc_ref, ys_ref, o_ref,
                    gbuf, sem, *, n_tok, tiles_per_batch, apply_norm):
    tm = x_ref.shape[1]
    i = pl.program_id(0) * tiles_per_batch + pl.program_id(1)
    n_tiles = pl.num_programs(0) * tiles_per_batch
    slot = lax.rem(i, 2)

    def gather(tile, buf):
        def issue(t, carry):
            for k in range(TOP_K):
                src = dest_ref[k * n_tok + tile * tm + t]
                pltpu.make_async_copy(ys_ref.at[pl.ds(src, 1)], gbuf.at[buf, k, pl.ds(t, 1)],
                                      sem.at[buf]).start(priority=k % 2)
            return carry

        lax.fori_loop(0, tm, issue, 0, unroll=4)

    @pl.when(i == 0)
    def _():
        gather(i, slot)

    @pl.when(i + 1 < n_tiles)
    def _():
        gather(i + 1, 1 - slot)

    for k in range(TOP_K):
        pltpu.make_async_copy(ys_ref.at[pl.ds(0, tm)], gbuf.at[slot, k], sem.at[slot]).wait()
    wt = wt_ref[...]
    ffn = wt[:, 0:1] * _unpack_rows(gbuf[slot, 0])
    for k in range(1, TOP_K):
        ffn = ffn + wt[:, k:k + 1] * _unpack_rows(gbuf[slot, k])
    x2 = x_ref[0] + gt_ref[0] * ffn
    o_ref[0] = _rms_mod(x2, g_ref[...], sh_ref[0], sc_ref[0]) if apply_norm else x2


def _combine(dest_flat, x1, wt_t, gt, g, sh, sc, ys, apply_norm, tm=256):
    b, s, d = x1.shape
    t = b * s
    spb = s // tm
    wd = ys.shape[1]
    return pl.pallas_call(
        functools.partial(_combine_kernel, n_tok=t, tiles_per_batch=spb, apply_norm=apply_norm),
        out_shape=jax.ShapeDtypeStruct((b, s, d), F32),
        grid_spec=pltpu.PrefetchScalarGridSpec(
            num_scalar_prefetch=1,
            grid=(b, spb),
            in_specs=[pl.BlockSpec((1, tm, d), lambda bi, i, *_: (bi, i, 0)),
                      pl.BlockSpec((tm, TOP_K), lambda bi, i, *_: (bi * spb + i, 0)),
                      pl.BlockSpec((1, 1, d), lambda bi, i, *_: (bi, 0, 0)),
                      pl.BlockSpec((1, d), lambda bi, i, *_: (0, 0)),
                      pl.BlockSpec((1, 1, d), lambda bi, i, *_: (bi, 0, 0)),
                      pl.BlockSpec((1, 1, d), lambda bi, i, *_: (bi, 0, 0)),
                      pl.BlockSpec(memory_space=pl.ANY)],
            out_specs=pl.BlockSpec((1, tm, d), lambda bi, i, *_: (bi, i, 0)),
            scratch_shapes=[pltpu.VMEM((2, TOP_K, tm, wd), U32),
                            pltpu.SemaphoreType.DMA((2,))]),
        compiler_params=_cparams(("arbitrary", "arbitrary")),
        name="combine",
    )(dest_flat, x1, wt_t, gt.reshape(b, 1, d), g.reshape(1, d), sh.reshape(b, 1, d),
      sc.reshape(b, 1, d), ys)


def _dest_kernel(start_ref, rps_ref, idx_ref, rank_ref, o_ref):
    idx = idx_ref[...]
    start = jnp.zeros(idx.shape, I32)
    per = jnp.ones(idx.shape, I32)
    for e in range(start_ref.shape[0]):
        sel = idx == e
        start = jnp.where(sel, start_ref[e], start)
        per = jnp.where(sel, rps_ref[e], per)
    rank = rank_ref[...]
    q = jnp.floor(rank.astype(F32) / per.astype(F32)).astype(I32)
    rem = rank - q * per
    q = jnp.where(rem < 0, q - 1, jnp.where(rem >= per, q + 1, q))
    rem = rank - q * per
    o_ref[...] = (start + q) * SUPER + rem


def _dest(sb_start, rps, idx, rank):
    return pl.pallas_call(
        _dest_kernel,
        out_shape=jax.ShapeDtypeStruct(idx.shape, I32),
        grid_spec=pltpu.PrefetchScalarGridSpec(
            num_scalar_prefetch=2,
            grid=(1,),
            in_specs=[pl.BlockSpec(idx.shape, lambda i, *_: (0, 0)),
                      pl.BlockSpec(idx.shape, lambda i, *_: (0, 0))],
            out_specs=pl.BlockSpec(idx.shape, lambda i, *_: (0, 0))),
        compiler_params=_cparams(("arbitrary",)),
        name="dest",
    )(sb_start.astype(I32), rps.astype(I32), idx, rank)


def _plan(idx, rank, counts, n_super):
    n_exp = counts.shape[0]
    nsb = (counts + SUPER - 1) // SUPER
    rps = (counts + jnp.maximum(nsb, 1) - 1) // jnp.maximum(nsb, 1)
    rps = jnp.maximum(rps, 1).astype(I32)
    sb_end = jnp.cumsum(nsb)
    sb_start = sb_end - nsb
    dest = _dest(sb_start, rps, idx, rank)
    s_ids = jnp.arange(n_super, dtype=I32)
    nused = sb_end[-1]
    sbe = jnp.minimum(jnp.searchsorted(sb_end, jnp.minimum(s_ids, nused - 1), side='right'),
                      n_exp - 1).astype(I32)
    per = SUPER // SUB
    q_ids = jnp.arange(n_super * per, dtype=I32)
    q_s = q_ids // per
    q_e = sbe[q_s]
    in_sb = jnp.minimum(rps[q_e], counts[q_e] - (q_s - sb_start[q_e]) * rps[q_e])
    left = in_sb - (q_ids % per) * SUB
    vrows = jnp.where(q_s < nused, jnp.clip(left, 0, SUB), 0)
    nsub = jnp.sum((vrows > 0).reshape(n_super, per), axis=1).astype(I32)
    zflag = (vrows < SUB).astype(I32)
    return dest.reshape(-1).astype(I32), sbe, nsub, nused.reshape(1).astype(I32), zflag


def kernel(x, c, ada_w, ada_b, norm1_g, w_in, pool_w, pool_scale, w_pool_out, ssm_lam_re, ssm_lam_im,
           ssm_log_dt, ssm_b_re, ssm_b_im, ssm_c_re, ssm_c_im, ssm_d, w_glu, b_glu, w_out, norm2_g,
           w_router, b_router, w1, b1, w2, b2, final_ada_w, final_ada_b, final_norm_g):
    b, s, d = x.shape
    t = b * s
    depth = ada_w.shape[0]
    n_exp = w_router.shape[-1]
    pwid = w_pool_out.shape[1]
    n_super = (t * TOP_K) // SUPER + n_exp

    c8 = jnp.zeros((8, d), F32).at[:b].set(c.astype(F32))
    fmod = _ada(c8, final_ada_w, final_ada_b)[:b]
    sh_o, sc_o = jnp.split(fmod, 2, axis=-1)
    for l in range(depth):
        mod = _ada(c8, ada_w[l], ada_b[l])[:b]
        sh_m, sc_m, gt_m, sh_f, sc_f, gt_f = jnp.split(mod, 6, axis=-1)
        z = _inproj(x, norm1_g[l], sh_m, sc_m, w_in[l].astype(BF16))
        ops = _s5_weights(ssm_lam_re[l], ssm_lam_im[l], ssm_log_dt[l], ssm_b_re[l], ssm_b_im[l],
                          ssm_c_re[l], ssm_c_im[l])
        ys = _s5(z, pwid, *ops, ssm_d[l])
        x = _mixout(z, ys, x, gt_m, pool_w[l].astype(BF16), pool_scale[l], w_pool_out[l].astype(BF16),
                    w_glu[l].astype(BF16), b_glu[l], w_out[l].astype(BF16))
        hp, idx, wt, rank, cnt = _router(x, norm2_g[l], sh_f, sc_f, w_router[l], b_router[l])
        dest, sbe, nsub, nused, zflag = _plan(idx, rank, cnt[:, 0], n_super)
        xs = _dispatch(dest, zflag, hp, n_super * SUPER)
        ye = _experts(sbe, nsub, nused, xs, w1[l], b1[l], w2[l], b2[l], n_super)
        last = l == depth - 1
        x = _combine(dest, x, wt.T, gt_f, final_norm_g, sh_o, sc_o, ye, apply_norm=last)
    return x
```

```python
import functools
import math

import jax
import jax.numpy as jnp
from jax import lax
from jax.experimental import pallas as pl
from jax.experimental.pallas import tpu as pltpu

F32 = jnp.float32
BF16 = jnp.bfloat16
I32 = jnp.int32
U32 = jnp.uint32

RMS_EPS = 1e-6
POOL_WINDOWS = (2, 4, 8, 16)
SSM_GROUP_DIM = 16
SSM_STATE = 64
LAMBDA_RE_MAX = -1e-4
TOP_K = 4
SWIGLU_ALPHA = 1.702
SWIGLU_LIMIT = 7.0

LANES = 128
V7X_VMEM_LIMIT = 56 * 1024 * 1024

CHUNK = 16
OCT = LANES // SSM_GROUP_DIM
SUB = 128
SUPER = 9 * SUB


def _cparams(sem, vmem=V7X_VMEM_LIMIT):
    return pltpu.CompilerParams(dimension_semantics=sem, vmem_limit_bytes=vmem)


def _const_spec(shape):
    nd = len(shape)
    return pl.BlockSpec(shape, lambda *_: (0,) * nd, pipeline_mode=pl.Buffered(1))


def _rms_mod(x, g, sh, sc):
    y = x * lax.rsqrt(jnp.mean(x * x, axis=-1, keepdims=True) + RMS_EPS) * g
    return y * (1.0 + sc) + sh


def _pair_width(d):
    return min(2 * LANES, d // 2)


def _pack_rows(h):
    d = h.shape[1]
    pair = _pair_width(d)
    hb = lax.bitcast_convert_type(h.astype(BF16).astype(F32), U32)
    words = []
    for n in range(d // (2 * pair)):
        lo = hb[:, 2 * pair * n:2 * pair * n + pair]
        hi = hb[:, 2 * pair * n + pair:2 * pair * (n + 1)]
        words.append((lo >> 16) | (hi & jnp.uint32(0xFFFF0000)))
    return jnp.concatenate(words, axis=1)


def _unpack_rows(w):
    d = 2 * w.shape[1]
    pair = _pair_width(d)
    cols = []
    for n in range(d // (2 * pair)):
        wn = w[:, pair * n:pair * (n + 1)]
        cols.append(lax.bitcast_convert_type(wn << 16, F32))
        cols.append(lax.bitcast_convert_type(wn & jnp.uint32(0xFFFF0000), F32))
    return jnp.concatenate(cols, axis=1)


def _ada_kernel(c_ref, w_ref, b_ref, o_ref):
    c = c_ref[...]
    ca = c * jax.nn.sigmoid(c)
    o_ref[...] = jnp.dot(ca.astype(BF16), w_ref[...].astype(BF16),
                         preferred_element_type=F32) + b_ref[...]


def _ada(c8, w, b, tn=1024):
    d, n = w.shape
    tn = math.gcd(tn, n)
    return pl.pallas_call(
        _ada_kernel,
        out_shape=jax.ShapeDtypeStruct((c8.shape[0], n), F32),
        grid=(n // tn,),
        in_specs=[pl.BlockSpec(c8.shape, lambda j: (0, 0)),
                  pl.BlockSpec((d, tn), lambda j: (0, j)),
                  pl.BlockSpec((1, tn), lambda j: (0, j))],
        out_specs=pl.BlockSpec((c8.shape[0], tn), lambda j: (0, j)),
        compiler_params=_cparams(("arbitrary",)),
        name="ada",
    )(c8, w, b.reshape(1, n))


def _inproj_kernel(x_ref, g_ref, sh_ref, sc_ref, w_ref, z_ref, *, ncol):
    h = _rms_mod(x_ref[0], g_ref[...], sh_ref[0], sc_ref[0]).astype(BF16)
    n = w_ref.shape[1]
    for c in range(n // ncol):
        z_ref[0, :, c * ncol:(c + 1) * ncol] = jnp.dot(
            h, w_ref[:, c * ncol:(c + 1) * ncol], preferred_element_type=F32).astype(BF16)


def _inproj(x, g, sh, sc, w, tm=512, ncol=1024):
    b, s, d = x.shape
    n = w.shape[1]
    ncol = math.gcd(ncol, n)
    return pl.pallas_call(
        functools.partial(_inproj_kernel, ncol=ncol),
        out_shape=jax.ShapeDtypeStruct((b, s, n), BF16),
        grid=(b, s // tm),
        in_specs=[pl.BlockSpec((1, tm, d), lambda bi, i: (bi, i, 0)),
                  _const_spec((1, d)),
                  pl.BlockSpec((1, 1, d), lambda bi, i: (bi, 0, 0)),
                  pl.BlockSpec((1, 1, d), lambda bi, i: (bi, 0, 0)),
                  _const_spec((d, n))],
        out_specs=pl.BlockSpec((1, tm, n), lambda bi, i: (bi, i, 0)),
        compiler_params=_cparams(("arbitrary", "arbitrary")),
        name="inproj",
    )(x, g.reshape(1, d), sh.reshape(b, 1, d), sc.reshape(b, 1, d), w)


def _s5_weights(lam_re, lam_im, log_dt, b_re, b_im, c_re, c_im):
    hp = lax.Precision.HIGHEST
    g_all, p = lam_re.shape
    h = b_re.shape[-1]
    no = g_all // OCT
    dt = jnp.exp(log_dt.astype(F32))[:, None]
    lre = jnp.minimum(lam_re.astype(F32), LAMBDA_RE_MAX)
    lim = lam_im.astype(F32)
    mag = jnp.exp(lre * dt)
    ang = lim * dt
    ab_re = mag * jnp.cos(ang)
    ab_im = mag * jnp.sin(ang)
    den = lre * lre + lim * lim
    nr = ab_re - 1.0
    ni = ab_im
    f_re = (nr * lre + ni * lim) / den
    f_im = (ni * lre - nr * lim) / den
    br_, bi_ = b_re.astype(F32), b_im.astype(F32)
    bb_re = f_re[..., None] * br_ - f_im[..., None] * bi_
    bb_im = f_re[..., None] * bi_ + f_im[..., None] * br_
    cr, ci = c_re.astype(F32), c_im.astype(F32)
    j = jnp.arange(CHUNK + 1, dtype=F32)[:, None, None]
    pw_mag = jnp.exp(j * (lre * dt)[None])
    pw_re = pw_mag * jnp.cos(j * ang[None])
    pw_im = pw_mag * jnp.sin(j * ang[None])
    pw_re, pw_im, bb_re, bb_im = lax.optimization_barrier((pw_re, pw_im, bb_re, bb_im))
    cp_re = cr[None] * pw_re[:, :, None, :] - ci[None] * pw_im[:, :, None, :]
    cp_im = cr[None] * pw_im[:, :, None, :] + ci[None] * pw_re[:, :, None, :]
    assert 2 * p == LANES and h * OCT == LANES
    tk = jnp.arange(CHUNK)

    klag = (jnp.einsum('jghp,gpi->jgih', cp_re[:CHUNK], bb_re, precision=hp)
            - jnp.einsum('jghp,gpi->jgih', cp_im[:CHUNK], bb_im, precision=hp))
    lag_t = klag.reshape(CHUNK, g_all * h, h)
    bt_re = jnp.swapaxes(bb_re, 1, 2)[None]
    bt_im = jnp.swapaxes(bb_im, 1, 2)[None]
    pwr = pw_re[CHUNK - 1 - tk][:, :, None, :]
    pwi = pw_im[CHUNK - 1 - tk][:, :, None, :]
    slabs = jnp.stack([pwr * bt_re - pwi * bt_im, pwr * bt_im + pwi * bt_re,
                       cp_re[1:], -cp_im[1:]]).reshape(4, CHUNK, g_all * h, p)
    are = pw_re[CHUNK].reshape(no, OCT, 1, p)
    aim = pw_im[CHUNK].reshape(no, OCT, 1, p)
    a1 = jnp.concatenate([are, are], axis=2).reshape(no, 1, OCT * LANES)
    a2 = jnp.concatenate([-aim, aim], axis=2).reshape(no, 1, OCT * LANES)
    return lag_t, slabs, a1, a2


def _gelu_tanh(x):
    return 0.5 * x * (1.0 + jnp.tanh(math.sqrt(2.0 / math.pi) * (x + 0.044715 * x * x * x)))


def _s5_kernel(u_ref, lag_ref, slab_ref, a1_ref, a2_ref, d_ref, y_ref,
               uscr, sscr, pscr, yscr, state, t_op, b_op, c_op):
    nb, tm, _ = u_ref.shape
    nc = tm // CHUNK
    nq = a1_ref.shape[-1] // LANES

    @pl.when(pl.program_id(1) == 0)
    def _():
        state[...] = jnp.zeros_like(state)
        rowg = lax.broadcasted_iota(I32, (LANES, LANES), 0) // SSM_GROUP_DIM
        colg = lax.broadcasted_iota(I32, (LANES, LANES), 1) // SSM_GROUP_DIM
        zero = jnp.zeros((LANES, LANES), BF16)
        rep = (lax.broadcasted_iota(I32, (SSM_GROUP_DIM, LANES), 1) % SSM_GROUP_DIM
               == lax.broadcasted_iota(I32, (SSM_GROUP_DIM, LANES), 0)).astype(BF16)
        lag = []
        for j in range(CHUNK):
            wide = jnp.dot(lag_ref[j].astype(BF16), rep, preferred_element_type=F32).astype(BF16)
            lag.append(jnp.where(rowg == colg, wide, zero))
        for k in range(CHUNK):
            in_k = jnp.concatenate([slab_ref[0, k], slab_ref[1, k]], axis=1).astype(BF16)
            out_k = jnp.concatenate([slab_ref[2, k], slab_ref[3, k]], axis=1).T.astype(BF16)
            for q in range(nq):
                b_op[k * LANES:(k + 1) * LANES, q * LANES:(q + 1) * LANES] = jnp.where(rowg == q, in_k, zero)
                c_op[q * LANES:(q + 1) * LANES, k * LANES:(k + 1) * LANES] = jnp.where(colg == q, out_k, zero)
            for t in range(CHUNK):
                t_op[k * LANES:(k + 1) * LANES, t * LANES:(t + 1) * LANES] = lag[t - k] if t >= k else zero

    for bi in range(nb):
        uscr[bi * tm:(bi + 1) * tm, :] = u_ref[bi].astype(F32)
    xk = jnp.concatenate(
        [uscr[pl.ds(k, nb * nc, stride=CHUNK), :] for k in range(CHUNK)], axis=1).astype(BF16)
    rr = nb * nc
    sc_all = jnp.dot(xk, b_op[...], preferred_element_type=F32)
    for q in range(nq):
        sscr[q * rr:(q + 1) * rr, :] = sc_all[:, q * LANES:(q + 1) * LANES]

    a1 = [a1_ref[0, :, q * LANES:(q + 1) * LANES] for q in range(nq)]
    a2 = [a2_ref[0, :, q * LANES:(q + 1) * LANES] for q in range(nq)]
    st = [state[:, q * LANES:(q + 1) * LANES] for q in range(nq)]
    for c in range(nc):
        for q in range(nq):
            pscr[pl.ds(q * rr + c, nb, stride=nc), :] = st[q]
            sc = sscr[pl.ds(q * rr + c, nb, stride=nc), :]
            st[q] = a1[q] * st[q] + a2[q] * pltpu.roll(st[q], LANES // 2, axis=1) + sc
    for q in range(nq):
        state[:, q * LANES:(q + 1) * LANES] = st[q]

    xprev = jnp.concatenate([pscr[q * rr:(q + 1) * rr, :] for q in range(nq)], axis=1)
    y = (jnp.dot(xk, t_op[...], preferred_element_type=F32)
         + jnp.dot(xprev.astype(BF16), c_op[...], preferred_element_type=F32))
    for t in range(CHUNK):
        yscr[pl.ds(t, nb * nc, stride=CHUNK), :] = y[:, t * LANES:(t + 1) * LANES]
    out = _gelu_tanh(yscr[...] + d_ref[...] * uscr[...])
    for bi in range(nb):
        y_ref[bi] = out[bi * tm:(bi + 1) * tm, :].astype(BF16)


def _s5(z, col0, lag_t, slabs, a1, a2, d_skip, tm=1024):
    nb, s, _ = z.shape
    no = a1.shape[0]
    w = no * LANES
    ns = a1.shape[-1]
    tm = min(tm, s)
    rows = nb * tm
    cb0 = col0 // LANES
    lag_spec = pl.BlockSpec((CHUNK, LANES, lag_t.shape[-1]), lambda o, i: (0, o, 0))
    slab_spec = pl.BlockSpec((4, CHUNK, LANES, slabs.shape[-1]), lambda o, i: (0, 0, o, 0))
    return pl.pallas_call(
        _s5_kernel,
        out_shape=jax.ShapeDtypeStruct((nb, s, w), BF16),
        grid=(no, s // tm),
        in_specs=[pl.BlockSpec((nb, tm, LANES), lambda o, i: (0, i, cb0 + o)),
                  lag_spec, slab_spec,
                  pl.BlockSpec((1, 1, ns), lambda o, i: (o, 0, 0)),
                  pl.BlockSpec((1, 1, ns), lambda o, i: (o, 0, 0)),
                  pl.BlockSpec((1, LANES), lambda o, i: (0, o))],
        out_specs=pl.BlockSpec((nb, tm, LANES), lambda o, i: (0, i, o)),
        scratch_shapes=[pltpu.VMEM((rows, LANES), F32),
                        pltpu.VMEM((rows // CHUNK * (ns // LANES), LANES), F32),
                        pltpu.VMEM((rows // CHUNK * (ns // LANES), LANES), F32),
                        pltpu.VMEM((rows, LANES), F32),
                        pltpu.VMEM((nb, ns), F32),
                        pltpu.VMEM((CHUNK * LANES, CHUNK * LANES), BF16),
                        pltpu.VMEM((CHUNK * LANES, ns), BF16),
                        pltpu.VMEM((ns, CHUNK * LANES), BF16)],
        compiler_params=_cparams(("arbitrary", "arbitrary")),
        name="s5",
    )(z, lag_t, slabs, a1, a2, d_skip.reshape(1, w))


def _mixout_kernel(up_ref, halo_ref, ys_ref, gp_ref, gs_ref, x_ref, gt_ref,
                   pw_ref, ps_ref, wpo_ref, wglu_ref, bglu_ref, wout_ref, o_ref, escr):
    tm = up_ref.shape[1]
    i = pl.program_id(1)
    gw = pw_ref.shape[1]
    hal = halo_ref.shape[1]
    d = o_ref.shape[2]
    escr[0:hal, :] = jnp.where(i > 0, halo_ref[0].astype(F32), 0.0)
    escr[hal:hal + tm, :] = up_ref[0].astype(F32)
    pos = (i * tm + lax.broadcasted_iota(I32, (tm, gw), 0) + 1).astype(F32)
    mixed = []
    for g, win in enumerate(POOL_WINDOWS):
        cols = slice(g * gw, (g + 1) * gw)
        cur = escr[hal:hal + tm, cols]
        acc = cur
        for j in range(1, win):
            acc = acc + escr[hal - j:hal - j + tm, cols]
        pooled = acc / jnp.minimum(pos, float(win)) - cur
        mixed.append(jnp.dot(pooled.astype(BF16), pw_ref[g], preferred_element_type=F32))
    mixed = jnp.concatenate(mixed, axis=1) * ps_ref[...]
    y_pool = jnp.dot(mixed.astype(BF16), wpo_ref[...], preferred_element_type=F32)
    glu = jnp.dot(ys_ref[0], wglu_ref[...], preferred_element_type=F32) + bglu_ref[...]
    y_ssm = glu[:, :d] * jax.nn.sigmoid(glu[:, d:])
    merged = (jax.nn.sigmoid(gp_ref[0].astype(F32)) * y_pool
              + jax.nn.sigmoid(gs_ref[0].astype(F32)) * y_ssm)
    mix = jnp.dot(merged.astype(BF16), wout_ref[...], preferred_element_type=F32)
    o_ref[0] = x_ref[0] + gt_ref[0] * mix


def _mixout(z, ys, x, gt, pool_w, pool_scale, w_pool_out, w_glu, b_glu, w_out, tm=256):
    b, s, d = x.shape
    pwid = w_pool_out.shape[0]
    hal = max(POOL_WINDOWS)
    tpb = tm // hal
    gcol = (pwid + ys.shape[2]) // d
    return pl.pallas_call(
        _mixout_kernel,
        out_shape=jax.ShapeDtypeStruct((b, s, d), F32),
        grid=(b, s // tm),
        in_specs=[pl.BlockSpec((1, tm, pwid), lambda bi, i: (bi, i, 0)),
                  pl.BlockSpec((1, hal, pwid), lambda bi, i: (bi, jnp.maximum(i * tpb - 1, 0), 0)),
                  pl.BlockSpec((1, tm, ys.shape[2]), lambda bi, i: (bi, i, 0)),
                  pl.BlockSpec((1, tm, d), lambda bi, i: (bi, i, gcol)),
                  pl.BlockSpec((1, tm, d), lambda bi, i: (bi, i, gcol + 1)),
                  pl.BlockSpec((1, tm, d), lambda bi, i: (bi, i, 0)),
                  pl.BlockSpec((1, 1, d), lambda bi, i: (bi, 0, 0)),
                  _const_spec(pool_w.shape),
                  _const_spec((1, pwid)),
                  _const_spec(w_pool_out.shape),
                  _const_spec(w_glu.shape),
                  _const_spec((1, w_glu.shape[1])),
                  _const_spec(w_out.shape)],
        out_specs=pl.BlockSpec((1, tm, d), lambda bi, i: (bi, i, 0)),
        scratch_shapes=[pltpu.VMEM((hal + tm, pwid), F32)],
        compiler_params=_cparams(("arbitrary", "arbitrary")),
        name="mixout",
    )(z, z, ys, z, z, x, gt.reshape(b, 1, d), pool_w, pool_scale.reshape(1, pwid),
      w_pool_out, w_glu, b_glu.reshape(1, -1), w_out)


def _router_kernel(x_ref, g_ref, sh_ref, sc_ref, wr_ref, br_ref,
                   hp_ref, idx_ref, wt_ref, rank_ref, cnt_ref, carry, *, n_exp):
    tm = x_ref.shape[1]
    first = (pl.program_id(0) == 0) & (pl.program_id(1) == 0)

    @pl.when(first)
    def _():
        carry[...] = jnp.zeros_like(carry)

    h2 = _rms_mod(x_ref[0], g_ref[...], sh_ref[0], sc_ref[0])
    hp_ref[...] = _pack_rows(h2)
    h_hi = h2.astype(BF16)
    h_lo = (h2 - h_hi.astype(F32)).astype(BF16)
    logits = (jnp.dot(h_hi, wr_ref[0], preferred_element_type=F32)
              + jnp.dot(h_hi, wr_ref[1], preferred_element_type=F32)
              + jnp.dot(h_lo, wr_ref[0], preferred_element_type=F32)) + br_ref[...]
    lt = logits.T[:n_exp, :]
    eid = lax.broadcasted_iota(I32, lt.shape, 0).astype(F32)
    vals, idxs, hots = [], [], []
    v = lt
    for _ in range(TOP_K):
        m = jnp.max(v, axis=0, keepdims=True)
        sel = jnp.min(jnp.where(v == m, eid, float(n_exp)), axis=0, keepdims=True)
        hot = eid == sel
        vals.append(m)
        idxs.append(sel)
        hots.append(hot)
        v = jnp.where(hot, -jnp.inf, v)
    ex = [jnp.exp(m - vals[0]) for m in vals]
    tot = ex[0] + ex[1] + ex[2] + ex[3]
    msum = sum(h.astype(F32) for h in hots)
    tri = (lax.broadcasted_iota(I32, (tm, tm), 0) < lax.broadcasted_iota(I32, (tm, tm), 1))
    before = jnp.dot(msum.astype(BF16), tri.astype(BF16), preferred_element_type=F32) + carry[:, 0:1]
    for k in range(TOP_K):
        idx_ref[k:k + 1, :] = idxs[k].astype(I32)
        wt_ref[k:k + 1, :] = ex[k] / tot
        rank_ref[k:k + 1, :] = jnp.sum(jnp.where(hots[k], before, 0.0), axis=0,
                                       keepdims=True).astype(I32)
    carry[...] = carry[...] + jnp.sum(msum, axis=1, keepdims=True)
    cnt_ref[...] = carry[...].astype(I32)


def _router(x1, g, sh, sc, w_router, b_router, tm=512):
    b, s, d = x1.shape
    t = b * s
    e = w_router.shape[1]
    wd = d // 2
    wr = jnp.zeros((d, LANES), F32).at[:, :e].set(w_router.astype(F32))
    wr_hi = wr.astype(BF16)
    wr = jnp.stack([wr_hi, (wr - wr_hi.astype(F32)).astype(BF16)])
    br = jnp.full((1, LANES), -1e30, F32).at[0, :e].set(b_router.astype(F32))
    spb = s // tm
    return pl.pallas_call(
        functools.partial(_router_kernel, n_exp=e),
        out_shape=[jax.ShapeDtypeStruct((t, wd), U32),
                   jax.ShapeDtypeStruct((TOP_K, t), I32),
                   jax.ShapeDtypeStruct((TOP_K, t), F32),
                   jax.ShapeDtypeStruct((TOP_K, t), I32),
                   jax.ShapeDtypeStruct((e, LANES), I32)],
        grid=(b, spb),
        in_specs=[pl.BlockSpec((1, tm, d), lambda bi, i: (bi, i, 0)),
                  _const_spec((1, d)),
                  pl.BlockSpec((1, 1, d), lambda bi, i: (bi, 0, 0)),
                  pl.BlockSpec((1, 1, d), lambda bi, i: (bi, 0, 0)),
                  _const_spec((2, d, LANES)),
                  _const_spec((1, LANES))],
        out_specs=[pl.BlockSpec((tm, wd), lambda bi, i: (bi * spb + i, 0)),
                   pl.BlockSpec((TOP_K, tm), lambda bi, i: (0, bi * spb + i)),
                   pl.BlockSpec((TOP_K, tm), lambda bi, i: (0, bi * spb + i)),
                   pl.BlockSpec((TOP_K, tm), lambda bi, i: (0, bi * spb + i)),
                   pl.BlockSpec((e, LANES), lambda bi, i: (0, 0))],
        scratch_shapes=[pltpu.VMEM((e, LANES), F32)],
        compiler_params=_cparams(("arbitrary", "arbitrary")),
        name="router",
    )(x1, g.reshape(1, d), sh.reshape(b, 1, d), sc.reshape(b, 1, d), wr, br)


def _dispatch_kernel(dest_ref, zflag_ref, hp_ref, xs_ref, zbuf, sem, zsem, *, n_tok):
    tm = hp_ref.shape[0]
    i = pl.program_id(0)

    @pl.when(i == 0)
    def _():
        zbuf[...] = jnp.zeros_like(zbuf)

        def zcopy(q):
            r0 = pl.multiple_of(q * SUB, SUB)
            return pltpu.make_async_copy(zbuf, xs_ref.at[pl.ds(r0, SUB)], zsem)

        def zstart(q, carry):
            pl.when(zflag_ref[q] != 0)(lambda: zcopy(q).start())
            return carry

        def zwait(q, carry):
            pl.when(zflag_ref[q] != 0)(lambda: zcopy(q).wait())
            return carry

        lax.fori_loop(0, zflag_ref.shape[0], zstart, 0)
        lax.fori_loop(0, zflag_ref.shape[0], zwait, 0)

    def issue(t, carry):
        for k in range(TOP_K):
            dst = dest_ref[k * n_tok + i * tm + t]
            pltpu.make_async_copy(hp_ref.at[pl.ds(t, 1)], xs_ref.at[pl.ds(dst, 1)],
                                  sem).start(priority=k % 2)
        return carry

    lax.fori_loop(0, tm, issue, 0, unroll=4)
    for k in range(TOP_K):
        pltpu.make_async_copy(hp_ref, xs_ref.at[pl.ds(0, tm)], sem).wait()


def _dispatch(dest_flat, zflag, hp, n_rows, tm=256):
    t, wd = hp.shape
    return pl.pallas_call(
        functools.partial(_dispatch_kernel, n_tok=t),
        out_shape=jax.ShapeDtypeStruct((n_rows, wd), U32),
        grid_spec=pltpu.PrefetchScalarGridSpec(
            num_scalar_prefetch=2,
            grid=(t // tm,),
            in_specs=[pl.BlockSpec((tm, wd), lambda i, *_: (i, 0))],
            out_specs=pl.BlockSpec(memory_space=pl.ANY),
            scratch_shapes=[pltpu.VMEM((SUB, wd), U32),
                            pltpu.SemaphoreType.DMA,
                            pltpu.SemaphoreType.DMA]),
        compiler_params=_cparams(("arbitrary",)),
        name="dispatch",
    )(dest_flat, zflag, hp)


def _experts_kernel(sbe_ref, nsub_ref, nused_ref, x_ref, b1_ref, b2_ref, w1_hbm, w2_hbm, o_ref,
                    xb, actb, yb, wt, sem):
    s = pl.program_id(0)
    n = nsub_ref[s]
    nused = nused_ref[0]
    nj, _, tf = actb.shape
    nn, _, pair = yb.shape
    d = xb.shape[1]
    f = nj * tf
    nslot = wt.shape[0]
    ntile = nj + nn
    per = SUPER // SUB

    def a_copies(e, j, slot):
        return [pltpu.make_async_copy(
            w1_hbm.at[e, :, pl.ds(pl.multiple_of(part * f + j * tf, tf), tf)],
            wt.at[slot, part, 0:d, 0:tf], sem.at[slot]) for part in range(2)]

    def b_copies(e, c, slot):
        return [pltpu.make_async_copy(
            w2_hbm.at[e, :, pl.ds(pl.multiple_of((2 * c + part) * pair, pair), pair)],
            wt.at[slot, part, 0:f, 0:pair], sem.at[slot]) for part in range(2)]

    def slot_of(sb, q):
        return lax.rem(sb * ntile + q, nslot)

    def start_tile(sb, q):
        e = sbe_ref[sb]
        slot = slot_of(sb, q)

        @pl.when(q < nj)
        def _():
            for cp in a_copies(e, q, slot):
                cp.start()

        @pl.when(q >= nj)
        def _():
            for cp in b_copies(e, q - nj, slot):
                cp.start()

    def prefetch(q):
        qq = q + nslot - 1

        @pl.when(qq < ntile)
        def _():
            start_tile(s, qq)

        @pl.when((qq >= ntile) & (s + 1 < nused))
        def _():
            start_tile(s + 1, qq - ntile)

    @pl.when((s == 0) & (nused > 0))
    def _():
        for q in range(nslot - 1):
            start_tile(0, jnp.int32(q))

    def run(v):
        m = v * SUB
        e = sbe_ref[s]
        for r in range(v):
            rows = slice(r * SUB, (r + 1) * SUB)
            xb[rows, :] = _unpack_rows(x_ref[rows, :]).astype(BF16)

        def a_body(j, carry):
            slot = slot_of(s, j)
            for cp in a_copies(e, j, slot):
                cp.wait()
            prefetch(j)
            x = xb[0:m, :]
            g = jnp.dot(x, wt[slot, 0, 0:d, 0:tf].astype(BF16), preferred_element_type=F32) + b1_ref[0, j]
            l = jnp.dot(x, wt[slot, 1, 0:d, 0:tf].astype(BF16), preferred_element_type=F32) + b1_ref[0, nj + j]
            xg = jnp.minimum(g, SWIGLU_LIMIT)
            xl = jnp.clip(l, -SWIGLU_LIMIT, SWIGLU_LIMIT)
            act = xg * jax.nn.sigmoid(SWIGLU_ALPHA * xg) * (xl + 1.0)
            actb[j, 0:m, :] = act.astype(BF16)
            return carry

        lax.fori_loop(0, nj, a_body, 0)

        def b_body(c, carry):
            slot = slot_of(s, nj + c)
            for cp in b_copies(e, c, slot):
                cp.wait()
            prefetch(nj + c)
            halves = []
            for part in range(2):
                w = wt[slot, part, 0:f, 0:pair].astype(BF16)
                y = b2_ref[0, 2 * c + part] + jnp.dot(actb[0, 0:m, :], w[0:tf, :],
                                                      preferred_element_type=F32)
                for j in range(1, nj):
                    y = y + jnp.dot(actb[j, 0:m, :], w[j * tf:(j + 1) * tf, :],
                                    preferred_element_type=F32)
                halves.append(y)
            yb[c, 0:m, :] = _pack_rows(jnp.concatenate(halves, axis=1))
            return carry

        lax.fori_loop(0, nn, b_body, 0)

        for c in range(nn):
            o_ref[0:m, c * pair:(c + 1) * pair] = yb[c, 0:m, :]
        if m < SUPER:
            o_ref[m:SUPER, :] = jnp.zeros((SUPER - m, o_ref.shape[1]), U32)

    for v in range(1, per + 1):
        pl.when((s < nused) & (n == v))(functools.partial(run, v))

    @pl.when(s >= nused)
    def _():
        o_ref[...] = jnp.zeros(o_ref.shape, U32)


def _experts(sbe, nsub, nused, xs, w1, b1, w2, b2, n_super, tf=256, nslot=3):
    n_exp, d, f2 = w1.shape
    f = f2 // 2
    tf = min(tf, f)
    pair = _pair_width(d)
    nj = f // tf
    nn = d // (2 * pair)
    wd = xs.shape[1]

    def x_map(s, sbe, nsub, nused):
        return (jnp.minimum(s, nused[0] - 1), 0)

    return pl.pallas_call(
        _experts_kernel,
        out_shape=jax.ShapeDtypeStruct((n_super * SUPER, wd), U32),
        grid_spec=pltpu.PrefetchScalarGridSpec(
            num_scalar_prefetch=3,
            grid=(n_super,),
            in_specs=[pl.BlockSpec((SUPER, wd), x_map),
                      pl.BlockSpec((1, 2 * nj, 1, tf), lambda s, sbe, nsub, nu: (sbe[s], 0, 0, 0)),
                      pl.BlockSpec((1, 2 * nn, 1, pair), lambda s, sbe, nsub, nu: (sbe[s], 0, 0, 0)),
                      pl.BlockSpec(memory_space=pl.ANY),
                      pl.BlockSpec(memory_space=pl.ANY)],
            out_specs=pl.BlockSpec((SUPER, wd), lambda s, *_: (s, 0)),
            scratch_shapes=[pltpu.VMEM((SUPER, d), BF16),
                            pltpu.VMEM((nj, SUPER, tf), BF16),
                            pltpu.VMEM((nn, SUPER, pair), U32),
                            pltpu.VMEM((nslot, 2, max(d, f), max(tf, pair)), F32),
                            pltpu.SemaphoreType.DMA((nslot,))]),
        compiler_params=_cparams(("arbitrary",)),
        name="experts",
    )(sbe, nsub, nused, xs, b1.reshape(n_exp, 2 * nj, 1, tf), b2.reshape(n_exp, 2 * nn, 1, pair), w1, w2)


def _combine_kernel(dest_ref, x_ref, wt_ref, gt_ref, g_ref, sh_ref, sc_ref, ys_ref, o_ref,
                    gbuf, sem, *, n_tok, tiles_per_batch, apply_norm):
    tm = x_ref.shape[1]
    i = pl.program_id(0) * tiles_per_batch + pl.program_id(1)
    n_tiles = pl.num_programs(0) * tiles_per_batch
    slot = lax.rem(i, 2)

    def gather(tile, buf):
        def issue(t, carry):
            for k in range(TOP_K):
                src = dest_ref[k * n_tok + tile * tm + t]
                pltpu.make_async_copy(ys_ref.at[pl.ds(src, 1)], gbuf.at[buf, k, pl.ds(t, 1)],
                                      sem.at[buf]).start(priority=k % 2)
            return carry

        lax.fori_loop(0, tm, issue, 0, unroll=4)

    @pl.when(i == 0)
    def _():
        gather(i, slot)

    @pl.when(i + 1 < n_tiles)
    def _():
        gather(i + 1, 1 - slot)

    for k in range(TOP_K):
        pltpu.make_async_copy(ys_ref.at[pl.ds(0, tm)], gbuf.at[slot, k], sem.at[slot]).wait()
    wt = wt_ref[...]
    ffn = wt[:, 0:1] * _unpack_rows(gbuf[slot, 0])
    for k in range(1, TOP_K):
        ffn = ffn + wt[:, k:k + 1] * _unpack_rows(gbuf[slot, k])
    x2 = x_ref[0] + gt_ref[0] * ffn
    o_ref[0] = _rms_mod(x2, g_ref[...], sh_ref[0], sc_ref[0]) if apply_norm else x2


def _combine(dest_flat, x1, wt_t, gt, g, sh, sc, ys, apply_norm, tm=256):
    b, s, d = x1.shape
    t = b * s
    spb = s // tm
    wd = ys.shape[1]
    return pl.pallas_call(
        functools.partial(_combine_kernel, n_tok=t, tiles_per_batch=spb, apply_norm=apply_norm),
        out_shape=jax.ShapeDtypeStruct((b, s, d), F32),
        grid_spec=pltpu.PrefetchScalarGridSpec(
            num_scalar_prefetch=1,
            grid=(b, spb),
            in_specs=[pl.BlockSpec((1, tm, d), lambda bi, i, *_: (bi, i, 0)),
                      pl.BlockSpec((tm, TOP_K), lambda bi, i, *_: (bi * spb + i, 0)),
                      pl.BlockSpec((1, 1, d), lambda bi, i, *_: (bi, 0, 0)),
                      pl.BlockSpec((1, d), lambda bi, i, *_: (0, 0)),
                      pl.BlockSpec((1, 1, d), lambda bi, i, *_: (bi, 0, 0)),
                      pl.BlockSpec((1, 1, d), lambda bi, i, *_: (bi, 0, 0)),
                      pl.BlockSpec(memory_space=pl.ANY)],
            out_specs=pl.BlockSpec((1, tm, d), lambda bi, i, *_: (bi, i, 0)),
            scratch_shapes=[pltpu.VMEM((2, TOP_K, tm, wd), U32),
                            pltpu.SemaphoreType.DMA((2,))]),
        compiler_params=_cparams(("arbitrary", "arbitrary")),
        name="combine",
    )(dest_flat, x1, wt_t, gt.reshape(b, 1, d), g.reshape(1, d), sh.reshape(b, 1, d),
      sc.reshape(b, 1, d), ys)


def _dest_kernel(start_ref, rps_ref, idx_ref, rank_ref, o_ref):
    idx = idx_ref[...]
    start = jnp.zeros(idx.shape, I32)
    per = jnp.ones(idx.shape, I32)
    for e in range(start_ref.shape[0]):
        sel = idx == e
        start = jnp.where(sel, start_ref[e], start)
        per = jnp.where(sel, rps_ref[e], per)
    rank = rank_ref[...]
    q = jnp.floor(rank.astype(F32) / per.astype(F32)).astype(I32)
    rem = rank - q * per
    q = jnp.where(rem < 0, q - 1, jnp.where(rem >= per, q + 1, q))
    rem = rank - q * per
    o_ref[...] = (start + q) * SUPER + rem


def _dest(sb_start, rps, idx, rank):
    return pl.pallas_call(
        _dest_kernel,
        out_shape=jax.ShapeDtypeStruct(idx.shape, I32),
        grid_spec=pltpu.PrefetchScalarGridSpec(
            num_scalar_prefetch=2,
            grid=(1,),
            in_specs=[pl.BlockSpec(idx.shape, lambda i, *_: (0, 0)),
                      pl.BlockSpec(idx.shape, lambda i, *_: (0, 0))],
            out_specs=pl.BlockSpec(idx.shape, lambda i, *_: (0, 0))),
        compiler_params=_cparams(("arbitrary",)),
        name="dest",
    )(sb_start.astype(I32), rps.astype(I32), idx, rank)


def _plan(idx, rank, counts, n_super):
    n_exp = counts.shape[0]
    nsb = (counts + SUPER - 1) // SUPER
    rps = (counts + jnp.maximum(nsb, 1) - 1) // jnp.maximum(nsb, 1)
    rps = jnp.maximum(rps, 1).astype(I32)
    sb_end = jnp.cumsum(nsb)
    sb_start = sb_end - nsb
    dest = _dest(sb_start, rps, idx, rank)
    s_ids = jnp.arange(n_super, dtype=I32)
    nused = sb_end[-1]
    sbe = jnp.minimum(jnp.searchsorted(sb_end, jnp.minimum(s_ids, nused - 1), side='right'),
                      n_exp - 1).astype(I32)
    per = SUPER // SUB
    q_ids = jnp.arange(n_super * per, dtype=I32)
    q_s = q_ids // per
    q_e = sbe[q_s]
    in_sb = jnp.minimum(rps[q_e], counts[q_e] - (q_s - sb_start[q_e]) * rps[q_e])
    left = in_sb - (q_ids % per) * SUB
    vrows = jnp.where(q_s < nused, jnp.clip(left, 0, SUB), 0)
    nsub = jnp.sum((vrows > 0).reshape(n_super, per), axis=1).astype(I32)
    zflag = (vrows < SUB).astype(I32)
    return dest.reshape(-1).astype(I32), sbe, nsub, nused.reshape(1).astype(I32), zflag


def kernel(x, c, ada_w, ada_b, norm1_g, w_in, pool_w, pool_scale, w_pool_out, ssm_lam_re, ssm_lam_im,
           ssm_log_dt, ssm_b_re, ssm_b_im, ssm_c_re, ssm_c_im, ssm_d, w_glu, b_glu, w_out, norm2_g,
           w_router, b_router, w1, b1, w2, b2, final_ada_w, final_ada_b, final_norm_g):
    b, s, d = x.shape
    t = b * s
    depth = ada_w.shape[0]
    n_exp = w_router.shape[-1]
    pwid = w_pool_out.shape[1]
    n_super = (t * TOP_K) // SUPER + n_exp

    c8 = jnp.zeros((8, d), F32).at[:b].set(c.astype(F32))
    fmod = _ada(c8, final_ada_w, final_ada_b)[:b]
    sh_o, sc_o = jnp.split(fmod, 2, axis=-1)
    for l in range(depth):
        mod = _ada(c8, ada_w[l], ada_b[l])[:b]
        sh_m, sc_m, gt_m, sh_f, sc_f, gt_f = jnp.split(mod, 6, axis=-1)
        z = _inproj(x, norm1_g[l], sh_m, sc_m, w_in[l].astype(BF16))
        ops = _s5_weights(ssm_lam_re[l], ssm_lam_im[l], ssm_log_dt[l], ssm_b_re[l], ssm_b_im[l],
                          ssm_c_re[l], ssm_c_im[l])
        ys = _s5(z, pwid, *ops, ssm_d[l])
        x = _mixout(z, ys, x, gt_m, pool_w[l].astype(BF16), pool_scale[l], w_pool_out[l].astype(BF16),
                    w_glu[l].astype(BF16), b_glu[l], w_out[l].astype(BF16))
        hp, idx, wt, rank, cnt = _router(x, norm2_g[l], sh_f, sc_f, w_router[l], b_router[l])
        dest, sbe, nsub, nused, zflag = _plan(idx, rank, cnt[:, 0], n_super)
        xs = _dispatch(dest, zflag, hp, n_super * SUPER)
        ye = _experts(sbe, nsub, nused, xs, w1[l], b1[l], w2[l], b2[l], n_super)
        last = l == depth - 1
        x = _combine(dest, x, wt.T, gt_f, final_norm_g, sh_o, sc_o, ye, apply_norm=last)
    return x
```

```python
import functools
import math

import jax
import jax.numpy as jnp
from jax import lax
from jax.experimental import pallas as pl
from jax.experimental.pallas import tpu as pltpu

F32 = jnp.float32
BF16 = jnp.bfloat16
I32 = jnp.int32
U32 = jnp.uint32

RMS_EPS = 1e-6
POOL_WINDOWS = (2, 4, 8, 16)
SSM_GROUP_DIM = 16
SSM_STATE = 64
LAMBDA_RE_MAX = -1e-4
TOP_K = 4
SWIGLU_ALPHA = 1.702
SWIGLU_LIMIT = 7.0

LANES = 128
V7X_VMEM_LIMIT = 56 * 1024 * 1024

CHUNK = 16
OCT = LANES // SSM_GROUP_DIM
SUB = 128
SUPER = 9 * SUB


def _cparams(sem, vmem=V7X_VMEM_LIMIT):
    return pltpu.CompilerParams(dimension_semantics=sem, vmem_limit_bytes=vmem)


def _const_spec(shape):
    nd = len(shape)
    return pl.BlockSpec(shape, lambda *_: (0,) * nd, pipeline_mode=pl.Buffered(1))


def _rms_mod(x, g, sh, sc):
    y = x * lax.rsqrt(jnp.mean(x * x, axis=-1, keepdims=True) + RMS_EPS) * g
    return y * (1.0 + sc) + sh


def _pair_width(d):
    return min(2 * LANES, d // 2)


def _pack_rows(h):
    d = h.shape[1]
    pair = _pair_width(d)
    hb = lax.bitcast_convert_type(h.astype(BF16).astype(F32), U32)
    words = []
    for n in range(d // (2 * pair)):
        lo = hb[:, 2 * pair * n:2 * pair * n + pair]
        hi = hb[:, 2 * pair * n + pair:2 * pair * (n + 1)]
        words.append((lo >> 16) | (hi & jnp.uint32(0xFFFF0000)))
    return jnp.concatenate(words, axis=1)


def _unpack_rows(w):
    d = 2 * w.shape[1]
    pair = _pair_width(d)
    cols = []
    for n in range(d // (2 * pair)):
        wn = w[:, pair * n:pair * (n + 1)]
        cols.append(lax.bitcast_convert_type(wn << 16, F32))
        cols.append(lax.bitcast_convert_type(wn & jnp.uint32(0xFFFF0000), F32))
    return jnp.concatenate(cols, axis=1)


def _ada_kernel(c_ref, w_ref, b_ref, o_ref):
    c = c_ref[...]
    ca = c * jax.nn.sigmoid(c)
    o_ref[...] = jnp.dot(ca.astype(BF16), w_ref[...].astype(BF16),
                         preferred_element_type=F32) + b_ref[...]


def _ada(c8, w, b, tn=1024):
    d, n = w.shape
    tn = math.gcd(tn, n)
    return pl.pallas_call(
        _ada_kernel,
        out_shape=jax.ShapeDtypeStruct((c8.shape[0], n), F32),
        grid=(n // tn,),
        in_specs=[pl.BlockSpec(c8.shape, lambda j: (0, 0)),
                  pl.BlockSpec((d, tn), lambda j: (0, j)),
                  pl.BlockSpec((1, tn), lambda j: (0, j))],
        out_specs=pl.BlockSpec((c8.shape[0], tn), lambda j: (0, j)),
        compiler_params=_cparams(("arbitrary",)),
        name="ada",
    )(c8, w, b.reshape(1, n))


def _inproj_kernel(x_ref, g_ref, sh_ref, sc_ref, w_ref, z_ref, *, ncol):
    h = _rms_mod(x_ref[0], g_ref[...], sh_ref[0], sc_ref[0]).astype(BF16)
    n = w_ref.shape[1]
    for c in range(n // ncol):
        z_ref[0, :, c * ncol:(c + 1) * ncol] = jnp.dot(
            h, w_ref[:, c * ncol:(c + 1) * ncol], preferred_element_type=F32).astype(BF16)


def _inproj(x, g, sh, sc, w, tm=512, ncol=1024):
    b, s, d = x.shape
    n = w.shape[1]
    ncol = math.gcd(ncol, n)
    return pl.pallas_call(
        functools.partial(_inproj_kernel, ncol=ncol),
        out_shape=jax.ShapeDtypeStruct((b, s, n), BF16),
        grid=(b, s // tm),
        in_specs=[pl.BlockSpec((1, tm, d), lambda bi, i: (bi, i, 0)),
                  _const_spec((1, d)),
                  pl.BlockSpec((1, 1, d), lambda bi, i: (bi, 0, 0)),
                  pl.BlockSpec((1, 1, d), lambda bi, i: (bi, 0, 0)),
                  _const_spec((d, n))],
        out_specs=pl.BlockSpec((1, tm, n), lambda bi, i: (bi, i, 0)),
        compiler_params=_cparams(("arbitrary", "arbitrary")),
        name="inproj",
    )(x, g.reshape(1, d), sh.reshape(b, 1, d), sc.reshape(b, 1, d), w)


def _s5_weights(lam_re, lam_im, log_dt, b_re, b_im, c_re, c_im):
    hp = lax.Precision.HIGHEST
    g_all, p = lam_re.shape
    h = b_re.shape[-1]
    no = g_all // OCT
    dt = jnp.exp(log_dt.astype(F32))[:, None]
    lre = jnp.minimum(lam_re.astype(F32), LAMBDA_RE_MAX)
    lim = lam_im.astype(F32)
    mag = jnp.exp(lre * dt)
    ang = lim * dt
    ab_re = mag * jnp.cos(ang)
    ab_im = mag * jnp.sin(ang)
    den = lre * lre + lim * lim
    nr = ab_re - 1.0
    ni = ab_im
    f_re = (nr * lre + ni * lim) / den
    f_im = (ni * lre - nr * lim) / den
    br_, bi_ = b_re.astype(F32), b_im.astype(F32)
    bb_re = f_re[..., None] * br_ - f_im[..., None] * bi_
    bb_im = f_re[..., None] * bi_ + f_im[..., None] * br_
    cr, ci = c_re.astype(F32), c_im.astype(F32)
    j = jnp.arange(CHUNK + 1, dtype=F32)[:, None, None]
    pw_mag = jnp.exp(j * (lre * dt)[None])
    pw_re = pw_mag * jnp.cos(j * ang[None])
    pw_im = pw_mag * jnp.sin(j * ang[None])
    pw_re, pw_im, bb_re, bb_im = lax.optimization_barrier((pw_re, pw_im, bb_re, bb_im))
    cp_re = cr[None] * pw_re[:, :, None, :] - ci[None] * pw_im[:, :, None, :]
    cp_im = cr[None] * pw_im[:, :, None, :] + ci[None] * pw_re[:, :, None, :]
    assert 2 * p == LANES and h * OCT == LANES
    tk = jnp.arange(CHUNK)

    bt_re = jnp.swapaxes(bb_re, 1, 2)
    bt_im = jnp.swapaxes(bb_im, 1, 2)
    cg_re = jnp.swapaxes(cp_re[:CHUNK], 0, 1).reshape(g_all, CHUNK * h, p)
    cg_im = jnp.swapaxes(cp_im[:CHUNK], 0, 1).reshape(g_all, CHUNK * h, p)
    klag = (jnp.einsum('gip,gap->gia', bt_re, cg_re, precision=hp)
            - jnp.einsum('gip,gap->gia', bt_im, cg_im, precision=hp))
    klag = jnp.transpose(klag.reshape(g_all, h, CHUNK, h), (2, 0, 1, 3))
    lag_t = klag.reshape(CHUNK, g_all * h, h)
    bt_re = bt_re[None]
    bt_im = bt_im[None]
    pwr = pw_re[CHUNK - 1 - tk][:, :, None, :]
    pwi = pw_im[CHUNK - 1 - tk][:, :, None, :]
    slabs = jnp.stack([pwr * bt_re - pwi * bt_im, pwr * bt_im + pwi * bt_re,
                       cp_re[1:], -cp_im[1:]]).reshape(4, CHUNK, g_all * h, p)
    are = pw_re[CHUNK].reshape(no, OCT, 1, p)
    aim = pw_im[CHUNK].reshape(no, OCT, 1, p)
    a1 = jnp.concatenate([are, are], axis=2).reshape(no, 1, OCT * LANES)
    a2 = jnp.concatenate([-aim, aim], axis=2).reshape(no, 1, OCT * LANES)
    return lag_t, slabs, a1, a2


def _gelu_tanh(x):
    return 0.5 * x * (1.0 + jnp.tanh(math.sqrt(2.0 / math.pi) * (x + 0.044715 * x * x * x)))


def _s5_kernel(u_ref, lag_ref, slab_ref, a1_ref, a2_ref, d_ref, y_ref,
               uscr, sscr, pscr, yscr, state, t_op, b_op, c_op):
    nb, tm, _ = u_ref.shape
    nc = tm // CHUNK
    nq = a1_ref.shape[-1] // LANES

    @pl.when(pl.program_id(1) == 0)
    def _():
        state[...] = jnp.zeros_like(state)
        rowg = lax.broadcasted_iota(I32, (LANES, LANES), 0) // SSM_GROUP_DIM
        colg = lax.broadcasted_iota(I32, (LANES, LANES), 1) // SSM_GROUP_DIM
        zero = jnp.zeros((LANES, LANES), BF16)
        rep = (lax.broadcasted_iota(I32, (SSM_GROUP_DIM, LANES), 1) % SSM_GROUP_DIM
               == lax.broadcasted_iota(I32, (SSM_GROUP_DIM, LANES), 0)).astype(BF16)
        lag = []
        for j in range(CHUNK):
            wide = jnp.dot(lag_ref[j].astype(BF16), rep, preferred_element_type=F32).astype(BF16)
            lag.append(jnp.where(rowg == colg, wide, zero))
        for k in range(CHUNK):
            in_k = jnp.concatenate([slab_ref[0, k], slab_ref[1, k]], axis=1).astype(BF16)
            out_k = jnp.concatenate([slab_ref[2, k], slab_ref[3, k]], axis=1).T.astype(BF16)
            for q in range(nq):
                b_op[k * LANES:(k + 1) * LANES, q * LANES:(q + 1) * LANES] = jnp.where(rowg == q, in_k, zero)
                c_op[q * LANES:(q + 1) * LANES, k * LANES:(k + 1) * LANES] = jnp.where(colg == q, out_k, zero)
            for t in range(CHUNK):
                t_op[k * LANES:(k + 1) * LANES, t * LANES:(t + 1) * LANES] = lag[t - k] if t >= k else zero

    for bi in range(nb):
        uscr[bi * tm:(bi + 1) * tm, :] = u_ref[bi].astype(F32)
    xk = jnp.concatenate(
        [uscr[pl.ds(k, nb * nc, stride=CHUNK), :] for k in range(CHUNK)], axis=1).astype(BF16)
    rr = nb * nc
    sc_all = jnp.dot(xk, b_op[...], preferred_element_type=F32)
    for q in range(nq):
        sscr[q * rr:(q + 1) * rr, :] = sc_all[:, q * LANES:(q + 1) * LANES]

    a1 = [a1_ref[0, :, q * LANES:(q + 1) * LANES] for q in range(nq)]
    a2 = [a2_ref[0, :, q * LANES:(q + 1) * LANES] for q in range(nq)]
    st = [state[:, q * LANES:(q + 1) * LANES] for q in range(nq)]
    for c in range(nc):
        for q in range(nq):
            pscr[pl.ds(q * rr + c, nb, stride=nc), :] = st[q]
            sc = sscr[pl.ds(q * rr + c, nb, stride=nc), :]
            st[q] = a1[q] * st[q] + a2[q] * pltpu.roll(st[q], LANES // 2, axis=1) + sc
    for q in range(nq):
        state[:, q * LANES:(q + 1) * LANES] = st[q]

    xprev = jnp.concatenate([pscr[q * rr:(q + 1) * rr, :] for q in range(nq)], axis=1)
    y = (jnp.dot(xk, t_op[...], preferred_element_type=F32)
         + jnp.dot(xprev.astype(BF16), c_op[...], preferred_element_type=F32))
    for t in range(CHUNK):
        yscr[pl.ds(t, nb * nc, stride=CHUNK), :] = y[:, t * LANES:(t + 1) * LANES]
    out = _gelu_tanh(yscr[...] + d_ref[...] * uscr[...])
    for bi in range(nb):
        y_ref[bi] = out[bi * tm:(bi + 1) * tm, :].astype(BF16)


def _s5(z, col0, lag_t, slabs, a1, a2, d_skip, tm=1024):
    nb, s, _ = z.shape
    no = a1.shape[0]
    w = no * LANES
    ns = a1.shape[-1]
    tm = min(tm, s)
    rows = nb * tm
    cb0 = col0 // LANES
    lag_spec = pl.BlockSpec((CHUNK, LANES, lag_t.shape[-1]), lambda o, i: (0, o, 0))
    slab_spec = pl.BlockSpec((4, CHUNK, LANES, slabs.shape[-1]), lambda o, i: (0, 0, o, 0))
    return pl.pallas_call(
        _s5_kernel,
        out_shape=jax.ShapeDtypeStruct((nb, s, w), BF16),
        grid=(no, s // tm),
        in_specs=[pl.BlockSpec((nb, tm, LANES), lambda o, i: (0, i, cb0 + o)),
                  lag_spec, slab_spec,
                  pl.BlockSpec((1, 1, ns), lambda o, i: (o, 0, 0)),
                  pl.BlockSpec((1, 1, ns), lambda o, i: (o, 0, 0)),
                  pl.BlockSpec((1, LANES), lambda o, i: (0, o))],
        out_specs=pl.BlockSpec((nb, tm, LANES), lambda o, i: (0, i, o)),
        scratch_shapes=[pltpu.VMEM((rows, LANES), F32),
                        pltpu.VMEM((rows // CHUNK * (ns // LANES), LANES), F32),
                        pltpu.VMEM((rows // CHUNK * (ns // LANES), LANES), F32),
                        pltpu.VMEM((rows, LANES), F32),
                        pltpu.VMEM((nb, ns), F32),
                        pltpu.VMEM((CHUNK * LANES, CHUNK * LANES), BF16),
                        pltpu.VMEM((CHUNK * LANES, ns), BF16),
                        pltpu.VMEM((ns, CHUNK * LANES), BF16)],
        compiler_params=_cparams(("arbitrary", "arbitrary")),
        name="s5",
    )(z, lag_t, slabs, a1, a2, d_skip.reshape(1, w))


def _mixout_kernel(up_ref, halo_ref, ys_ref, gp_ref, gs_ref, x_ref, gt_ref,
                   pw_ref, ps_ref, wpo_ref, wglu_ref, bglu_ref, wout_ref, o_ref, escr):
    tm = up_ref.shape[1]
    i = pl.program_id(1)
    gw = pw_ref.shape[1]
    hal = halo_ref.shape[1]
    d = o_ref.shape[2]
    escr[0:hal, :] = jnp.where(i > 0, halo_ref[0].astype(F32), 0.0)
    escr[hal:hal + tm, :] = up_ref[0].astype(F32)
    pos = (i * tm + lax.broadcasted_iota(I32, (tm, gw), 0) + 1).astype(F32)
    mixed = []
    for g, win in enumerate(POOL_WINDOWS):
        cols = slice(g * gw, (g + 1) * gw)
        cur = escr[hal:hal + tm, cols]
        acc = cur
        for j in range(1, win):
            acc = acc + escr[hal - j:hal - j + tm, cols]
        pooled = acc / jnp.minimum(pos, float(win)) - cur
        mixed.append(jnp.dot(pooled.astype(BF16), pw_ref[g], preferred_element_type=F32))
    mixed = jnp.concatenate(mixed, axis=1) * ps_ref[...]
    y_pool = jnp.dot(mixed.astype(BF16), wpo_ref[...], preferred_element_type=F32)
    glu = jnp.dot(ys_ref[0], wglu_ref[...], preferred_element_type=F32) + bglu_ref[...]
    y_ssm = glu[:, :d] * jax.nn.sigmoid(glu[:, d:])
    merged = (jax.nn.sigmoid(gp_ref[0].astype(F32)) * y_pool
              + jax.nn.sigmoid(gs_ref[0].astype(F32)) * y_ssm)
    mix = jnp.dot(merged.astype(BF16), wout_ref[...], preferred_element_type=F32)
    o_ref[0] = x_ref[0] + gt_ref[0] * mix


def _mixout(z, ys, x, gt, pool_w, pool_scale, w_pool_out, w_glu, b_glu, w_out, tm=256):
    b, s, d = x.shape
    pwid = w_pool_out.shape[0]
    hal = max(POOL_WINDOWS)
    tpb = tm // hal
    gcol = (pwid + ys.shape[2]) // d
    return pl.pallas_call(
        _mixout_kernel,
        out_shape=jax.ShapeDtypeStruct((b, s, d), F32),
        grid=(b, s // tm),
        in_specs=[pl.BlockSpec((1, tm, pwid), lambda bi, i: (bi, i, 0)),
                  pl.BlockSpec((1, hal, pwid), lambda bi, i: (bi, jnp.maximum(i * tpb - 1, 0), 0)),
                  pl.BlockSpec((1, tm, ys.shape[2]), lambda bi, i: (bi, i, 0)),
                  pl.BlockSpec((1, tm, d), lambda bi, i: (bi, i, gcol)),
                  pl.BlockSpec((1, tm, d), lambda bi, i: (bi, i, gcol + 1)),
                  pl.BlockSpec((1, tm, d), lambda bi, i: (bi, i, 0)),
                  pl.BlockSpec((1, 1, d), lambda bi, i: (bi, 0, 0)),
                  _const_spec(pool_w.shape),
                  _const_spec((1, pwid)),
                  _const_spec(w_pool_out.shape),
                  _const_spec(w_glu.shape),
                  _const_spec((1, w_glu.shape[1])),
                  _const_spec(w_out.shape)],
        out_specs=pl.BlockSpec((1, tm, d), lambda bi, i: (bi, i, 0)),
        scratch_shapes=[pltpu.VMEM((hal + tm, pwid), F32)],
        compiler_params=_cparams(("arbitrary", "arbitrary")),
        name="mixout",
    )(z, z, ys, z, z, x, gt.reshape(b, 1, d), pool_w, pool_scale.reshape(1, pwid),
      w_pool_out, w_glu, b_glu.reshape(1, -1), w_out)


def _router_kernel(x_ref, g_ref, sh_ref, sc_ref, wr_ref, br_ref,
                   hp_ref, idx_ref, wt_ref, rank_ref, cnt_ref, carry, *, n_exp):
    tm = x_ref.shape[1]
    first = (pl.program_id(0) == 0) & (pl.program_id(1) == 0)

    @pl.when(first)
    def _():
        carry[...] = jnp.zeros_like(carry)

    h2 = _rms_mod(x_ref[0], g_ref[...], sh_ref[0], sc_ref[0])
    hp_ref[...] = _pack_rows(h2)
    h_hi = h2.astype(BF16)
    h_lo = (h2 - h_hi.astype(F32)).astype(BF16)
    logits = (jnp.dot(h_hi, wr_ref[0], preferred_element_type=F32)
              + jnp.dot(h_hi, wr_ref[1], preferred_element_type=F32)
              + jnp.dot(h_lo, wr_ref[0], preferred_element_type=F32)) + br_ref[...]
    lt = logits.T[:n_exp, :]
    eid = lax.broadcasted_iota(I32, lt.shape, 0).astype(F32)
    vals, idxs, hots = [], [], []
    v = lt
    for _ in range(TOP_K):
        m = jnp.max(v, axis=0, keepdims=True)
        sel = jnp.min(jnp.where(v == m, eid, float(n_exp)), axis=0, keepdims=True)
        hot = eid == sel
        vals.append(m)
        idxs.append(sel)
        hots.append(hot)
        v = jnp.where(hot, -jnp.inf, v)
    ex = [jnp.exp(m - vals[0]) for m in vals]
    tot = ex[0] + ex[1] + ex[2] + ex[3]
    msum = sum(h.astype(F32) for h in hots)
    tri = (lax.broadcasted_iota(I32, (tm, tm), 0) < lax.broadcasted_iota(I32, (tm, tm), 1))
    before = jnp.dot(msum.astype(BF16), tri.astype(BF16), preferred_element_type=F32) + carry[:, 0:1]
    for k in range(TOP_K):
        idx_ref[k:k + 1, :] = idxs[k].astype(I32)
        wt_ref[k:k + 1, :] = ex[k] / tot
        rank_ref[k:k + 1, :] = jnp.sum(jnp.where(hots[k], before, 0.0), axis=0,
                                       keepdims=True).astype(I32)
    carry[...] = carry[...] + jnp.sum(msum, axis=1, keepdims=True)
    cnt_ref[...] = carry[...].astype(I32)


def _router(x1, g, sh, sc, w_router, b_router, tm=512):
    b, s, d = x1.shape
    t = b * s
    e = w_router.shape[1]
    wd = d // 2
    wr = jnp.zeros((d, LANES), F32).at[:, :e].set(w_router.astype(F32))
    wr_hi = wr.astype(BF16)
    wr = jnp.stack([wr_hi, (wr - wr_hi.astype(F32)).astype(BF16)])
    br = jnp.full((1, LANES), -1e30, F32).at[0, :e].set(b_router.astype(F32))
    spb = s // tm
    return pl.pallas_call(
        functools.partial(_router_kernel, n_exp=e),
        out_shape=[jax.ShapeDtypeStruct((t, wd), U32),
                   jax.ShapeDtypeStruct((TOP_K, t), I32),
                   jax.ShapeDtypeStruct((TOP_K, t), F32),
                   jax.ShapeDtypeStruct((TOP_K, t), I32),
                   jax.ShapeDtypeStruct((e, LANES), I32)],
        grid=(b, spb),
        in_specs=[pl.BlockSpec((1, tm, d), lambda bi, i: (bi, i, 0)),
                  _const_spec((1, d)),
                  pl.BlockSpec((1, 1, d), lambda bi, i: (bi, 0, 0)),
                  pl.BlockSpec((1, 1, d), lambda bi, i: (bi, 0, 0)),
                  _const_spec((2, d, LANES)),
                  _const_spec((1, LANES))],
        out_specs=[pl.BlockSpec((tm, wd), lambda bi, i: (bi * spb + i, 0)),
                   pl.BlockSpec((TOP_K, tm), lambda bi, i: (0, bi * spb + i)),
                   pl.BlockSpec((TOP_K, tm), lambda bi, i: (0, bi * spb + i)),
                   pl.BlockSpec((TOP_K, tm), lambda bi, i: (0, bi * spb + i)),
                   pl.BlockSpec((e, LANES), lambda bi, i: (0, 0))],
        scratch_shapes=[pltpu.VMEM((e, LANES), F32)],
        compiler_params=_cparams(("arbitrary", "arbitrary")),
        name="router",
    )(x1, g.reshape(1, d), sh.reshape(b, 1, d), sc.reshape(b, 1, d), wr, br)


def _dispatch_kernel(dest_ref, zflag_ref, hp_ref, xs_ref, zbuf, sem, zsem, *, n_tok):
    tm = hp_ref.shape[0]
    i = pl.program_id(0)

    @pl.when(i == 0)
    def _():
        zbuf[...] = jnp.zeros_like(zbuf)

        def zcopy(q):
            r0 = pl.multiple_of(q * SUB, SUB)
            return pltpu.make_async_copy(zbuf, xs_ref.at[pl.ds(r0, SUB)], zsem)

        def zstart(q, carry):
            pl.when(zflag_ref[q] != 0)(lambda: zcopy(q).start())
            return carry

        def zwait(q, carry):
            pl.when(zflag_ref[q] != 0)(lambda: zcopy(q).wait())
            return carry

        lax.fori_loop(0, zflag_ref.shape[0], zstart, 0)
        lax.fori_loop(0, zflag_ref.shape[0], zwait, 0)

    def issue(t, carry):
        for k in range(TOP_K):
            dst = dest_ref[k * n_tok + i * tm + t]
            pltpu.make_async_copy(hp_ref.at[pl.ds(t, 1)], xs_ref.at[pl.ds(dst, 1)],
                                  sem).start(priority=k % 2)
        return carry

    lax.fori_loop(0, tm, issue, 0, unroll=8)
    for k in range(TOP_K):
        pltpu.make_async_copy(hp_ref, xs_ref.at[pl.ds(0, tm)], sem).wait()


def _dispatch(dest_flat, zflag, hp, n_rows, tm=256):
    t, wd = hp.shape
    return pl.pallas_call(
        functools.partial(_dispatch_kernel, n_tok=t),
        out_shape=jax.ShapeDtypeStruct((n_rows, wd), U32),
        grid_spec=pltpu.PrefetchScalarGridSpec(
            num_scalar_prefetch=2,
            grid=(t // tm,),
            in_specs=[pl.BlockSpec((tm, wd), lambda i, *_: (i, 0))],
            out_specs=pl.BlockSpec(memory_space=pl.ANY),
            scratch_shapes=[pltpu.VMEM((SUB, wd), U32),
                            pltpu.SemaphoreType.DMA,
                            pltpu.SemaphoreType.DMA]),
        compiler_params=_cparams(("arbitrary",)),
        name="dispatch",
    )(dest_flat, zflag, hp)


def _experts_kernel(sbe_ref, nsub_ref, nused_ref, x_ref, b1_ref, b2_ref, w1_hbm, w2_hbm, o_ref,
                    xb, actb, yb, wt, sem):
    s = pl.program_id(0)
    n = nsub_ref[s]
    nused = nused_ref[0]
    nj, _, tf = actb.shape
    nn, _, pair = yb.shape
    d = xb.shape[1]
    f = nj * tf
    nslot = wt.shape[0]
    ntile = nj + nn
    per = SUPER // SUB

    def a_copies(e, j, slot):
        return [pltpu.make_async_copy(
            w1_hbm.at[e, :, pl.ds(pl.multiple_of(part * f + j * tf, tf), tf)],
            wt.at[slot, part, 0:d, 0:tf], sem.at[slot]) for part in range(2)]

    def b_copies(e, c, slot):
        return [pltpu.make_async_copy(
            w2_hbm.at[e, :, pl.ds(pl.multiple_of((2 * c + part) * pair, pair), pair)],
            wt.at[slot, part, 0:f, 0:pair], sem.at[slot]) for part in range(2)]

    def slot_of(sb, q):
        return lax.rem(sb * ntile + q, nslot)

    def start_tile(sb, q):
        e = sbe_ref[sb]
        slot = slot_of(sb, q)

        @pl.when(q < nj)
        def _():
            for cp in a_copies(e, q, slot):
                cp.start()

        @pl.when(q >= nj)
        def _():
            for cp in b_copies(e, q - nj, slot):
                cp.start()

    def prefetch(q):
        qq = q + nslot - 1

        @pl.when(qq < ntile)
        def _():
            start_tile(s, qq)

        @pl.when((qq >= ntile) & (s + 1 < nused))
        def _():
            start_tile(s + 1, qq - ntile)

    @pl.when((s == 0) & (nused > 0))
    def _():
        for q in range(nslot - 1):
            start_tile(0, jnp.int32(q))

    def run(v):
        m = v * SUB
        e = sbe_ref[s]
        for r in range(v):
            rows = slice(r * SUB, (r + 1) * SUB)
            xb[rows, :] = _unpack_rows(x_ref[rows, :]).astype(BF16)

        def a_body(j, carry):
            slot = slot_of(s, j)
            for cp in a_copies(e, j, slot):
                cp.wait()
            prefetch(j)
            x = xb[0:m, :]
            g = jnp.dot(x, wt[slot, 0, 0:d, 0:tf].astype(BF16), preferred_element_type=F32) + b1_ref[0, j]
            l = jnp.dot(x, wt[slot, 1, 0:d, 0:tf].astype(BF16), preferred_element_type=F32) + b1_ref[0, nj + j]
            xg = jnp.minimum(g, SWIGLU_LIMIT)
            xl = jnp.clip(l, -SWIGLU_LIMIT, SWIGLU_LIMIT)
            act = xg * jax.nn.sigmoid(SWIGLU_ALPHA * xg) * (xl + 1.0)
            actb[j, 0:m, :] = act.astype(BF16)
            return carry

        lax.fori_loop(0, nj, a_body, 0)

        def b_body(c, carry):
            slot = slot_of(s, nj + c)
            for cp in b_copies(e, c, slot):
                cp.wait()
            prefetch(nj + c)
            halves = []
            for part in range(2):
                w = wt[slot, part, 0:f, 0:pair].astype(BF16)
                y = b2_ref[0, 2 * c + part] + jnp.dot(actb[0, 0:m, :], w[0:tf, :],
                                                      preferred_element_type=F32)
                for j in range(1, nj):
                    y = y + jnp.dot(actb[j, 0:m, :], w[j * tf:(j + 1) * tf, :],
                                    preferred_element_type=F32)
                halves.append(y)
            yb[c, 0:m, :] = _pack_rows(jnp.concatenate(halves, axis=1))
            return carry

        lax.fori_loop(0, nn, b_body, 0)

        for c in range(nn):
            o_ref[0:m, c * pair:(c + 1) * pair] = yb[c, 0:m, :]
        if m < SUPER:
            o_ref[m:SUPER, :] = jnp.zeros((SUPER - m, o_ref.shape[1]), U32)

    for v in range(1, per + 1):
        pl.when((s < nused) & (n == v))(functools.partial(run, v))

    @pl.when(s >= nused)
    def _():
        o_ref[...] = jnp.zeros(o_ref.shape, U32)


def _experts(sbe, nsub, nused, xs, w1, b1, w2, b2, n_super, tf=256, nslot=3):
    n_exp, d, f2 = w1.shape
    f = f2 // 2
    tf = min(tf, f)
    pair = _pair_width(d)
    nj = f // tf
    nn = d // (2 * pair)
    wd = xs.shape[1]

    def x_map(s, sbe, nsub, nused):
        return (jnp.minimum(s, nused[0] - 1), 0)

    return pl.pallas_call(
        _experts_kernel,
        out_shape=jax.ShapeDtypeStruct((n_super * SUPER, wd), U32),
        grid_spec=pltpu.PrefetchScalarGridSpec(
            num_scalar_prefetch=3,
            grid=(n_super,),
            in_specs=[pl.BlockSpec((SUPER, wd), x_map),
                      pl.BlockSpec((1, 2 * nj, 1, tf), lambda s, sbe, nsub, nu: (sbe[s], 0, 0, 0)),
                      pl.BlockSpec((1, 2 * nn, 1, pair), lambda s, sbe, nsub, nu: (sbe[s], 0, 0, 0)),
                      pl.BlockSpec(memory_space=pl.ANY),
                      pl.BlockSpec(memory_space=pl.ANY)],
            out_specs=pl.BlockSpec((SUPER, wd), lambda s, *_: (s, 0)),
            scratch_shapes=[pltpu.VMEM((SUPER, d), BF16),
                            pltpu.VMEM((nj, SUPER, tf), BF16),
                            pltpu.VMEM((nn, SUPER, pair), U32),
                            pltpu.VMEM((nslot, 2, max(d, f), max(tf, pair)), F32),
                            pltpu.SemaphoreType.DMA((nslot,))]),
        compiler_params=_cparams(("arbitrary",)),
        name="experts",
    )(sbe, nsub, nused, xs, b1.reshape(n_exp, 2 * nj, 1, tf), b2.reshape(n_exp, 2 * nn, 1, pair), w1, w2)


def _combine_kernel(dest_ref, x_ref, wt_ref, gt_ref, g_ref, sh_ref, sc_ref, ys_ref, o_ref,
                    gbuf, sem, *, n_tok, tiles_per_batch, apply_norm):
    tm = x_ref.shape[1]
    i = pl.program_id(0) * tiles_per_batch + pl.program_id(1)
    n_tiles = pl.num_programs(0) * tiles_per_batch
    slot = lax.rem(i, 2)

    def gather(tile, buf):
        def issue(t, carry):
            for k in range(TOP_K):
                src = dest_ref[k * n_tok + tile * tm + t]
                pltpu.make_async_copy(ys_ref.at[pl.ds(src, 1)], gbuf.at[buf, k, pl.ds(t, 1)],
                                      sem.at[buf]).start(priority=k % 2)
            return carry

        lax.fori_loop(0, tm, issue, 0, unroll=8)

    @pl.when(i == 0)
    def _():
        gather(i, slot)

    @pl.when(i + 1 < n_tiles)
    def _():
        gather(i + 1, 1 - slot)

    for k in range(TOP_K):
        pltpu.make_async_copy(ys_ref.at[pl.ds(0, tm)], gbuf.at[slot, k], sem.at[slot]).wait()
    wt = wt_ref[...]
    ffn = wt[:, 0:1] * _unpack_rows(gbuf[slot, 0])
    for k in range(1, TOP_K):
        ffn = ffn + wt[:, k:k + 1] * _unpack_rows(gbuf[slot, k])
    x2 = x_ref[0] + gt_ref[0] * ffn
    o_ref[0] = _rms_mod(x2, g_ref[...], sh_ref[0], sc_ref[0]) if apply_norm else x2


def _combine(dest_flat, x1, wt_t, gt, g, sh, sc, ys, apply_norm, tm=256):
    b, s, d = x1.shape
    t = b * s
    spb = s // tm
    wd = ys.shape[1]
    return pl.pallas_call(
        functools.partial(_combine_kernel, n_tok=t, tiles_per_batch=spb, apply_norm=apply_norm),
        out_shape=jax.ShapeDtypeStruct((b, s, d), F32),
        grid_spec=pltpu.PrefetchScalarGridSpec(
            num_scalar_prefetch=1,
            grid=(b, spb),
            in_specs=[pl.BlockSpec((1, tm, d), lambda bi, i, *_: (bi, i, 0)),
                      pl.BlockSpec((tm, TOP_K), lambda bi, i, *_: (bi * spb + i, 0)),
                      pl.BlockSpec((1, 1, d), lambda bi, i, *_: (bi, 0, 0)),
                      pl.BlockSpec((1, d), lambda bi, i, *_: (0, 0)),
                      pl.BlockSpec((1, 1, d), lambda bi, i, *_: (bi, 0, 0)),
                      pl.BlockSpec((1, 1, d), lambda bi, i, *_: (bi, 0, 0)),
                      pl.BlockSpec(memory_space=pl.ANY)],
            out_specs=pl.BlockSpec((1, tm, d), lambda bi, i, *_: (bi, i, 0)),
            scratch_shapes=[pltpu.VMEM((2, TOP_K, tm, wd), U32),
                            pltpu.SemaphoreType.DMA((2,))]),
        compiler_params=_cparams(("arbitrary", "arbitrary")),
        name="combine",
    )(dest_flat, x1, wt_t, gt.reshape(b, 1, d), g.reshape(1, d), sh.reshape(b, 1, d),
      sc.reshape(b, 1, d), ys)


def _dest_kernel(start_ref, rps_ref, idx_ref, rank_ref, o_ref):
    idx = idx_ref[...]
    start = jnp.zeros(idx.shape, I32)
    per = jnp.ones(idx.shape, I32)
    for e in range(start_ref.shape[0]):
        sel = idx == e
        start = jnp.where(sel, start_ref[e], start)
        per = jnp.where(sel, rps_ref[e], per)
    rank = rank_ref[...]
    q = jnp.floor(rank.astype(F32) / per.astype(F32)).astype(I32)
    rem = rank - q * per
    q = jnp.where(rem < 0, q - 1, jnp.where(rem >= per, q + 1, q))
    rem = rank - q * per
    o_ref[...] = (start + q) * SUPER + rem


def _dest(sb_start, rps, idx, rank):
    return pl.pallas_call(
        _dest_kernel,
        out_shape=jax.ShapeDtypeStruct(idx.shape, I32),
        grid_spec=pltpu.PrefetchScalarGridSpec(
            num_scalar_prefetch=2,
            grid=(1,),
            in_specs=[pl.BlockSpec(idx.shape, lambda i, *_: (0, 0)),
                      pl.BlockSpec(idx.shape, lambda i, *_: (0, 0))],
            out_specs=pl.BlockSpec(idx.shape, lambda i, *_: (0, 0))),
        compiler_params=_cparams(("arbitrary",)),
        name="dest",
    )(sb_start.astype(I32), rps.astype(I32), idx, rank)


def _plan(idx, rank, counts, n_super):
    n_exp = counts.shape[0]
    nsb = (counts + SUPER - 1) // SUPER
    rps = (counts + jnp.maximum(nsb, 1) - 1) // jnp.maximum(nsb, 1)
    rps = jnp.maximum(rps, 1).astype(I32)
    sb_end = jnp.cumsum(nsb)
    sb_start = sb_end - nsb
    dest = _dest(sb_start, rps, idx, rank)
    s_ids = jnp.arange(n_super, dtype=I32)
    nused = sb_end[-1]
    sbe = jnp.minimum(jnp.searchsorted(sb_end, jnp.minimum(s_ids, nused - 1), side='right'),
                      n_exp - 1).astype(I32)
    per = SUPER // SUB
    q_ids = jnp.arange(n_super * per, dtype=I32)
    q_s = q_ids // per
    q_e = sbe[q_s]
    in_sb = jnp.minimum(rps[q_e], counts[q_e] - (q_s - sb_start[q_e]) * rps[q_e])
    left = in_sb - (q_ids % per) * SUB
    vrows = jnp.where(q_s < nused, jnp.clip(left, 0, SUB), 0)
    nsub = jnp.sum((vrows > 0).reshape(n_super, per), axis=1).astype(I32)
    zflag = (vrows < SUB).astype(I32)
    return dest.reshape(-1).astype(I32), sbe, nsub, nused.reshape(1).astype(I32), zflag


def kernel(x, c, ada_w, ada_b, norm1_g, w_in, pool_w, pool_scale, w_pool_out, ssm_lam_re, ssm_lam_im,
           ssm_log_dt, ssm_b_re, ssm_b_im, ssm_c_re, ssm_c_im, ssm_d, w_glu, b_glu, w_out, norm2_g,
           w_router, b_router, w1, b1, w2, b2, final_ada_w, final_ada_b, final_norm_g):
    b, s, d = x.shape
    t = b * s
    depth = ada_w.shape[0]
    n_exp = w_router.shape[-1]
    pwid = w_pool_out.shape[1]
    n_super = (t * TOP_K) // SUPER + n_exp

    c8 = jnp.zeros((8, d), F32).at[:b].set(c.astype(F32))
    fmod = _ada(c8, final_ada_w, final_ada_b)[:b]
    sh_o, sc_o = jnp.split(fmod, 2, axis=-1)
    for l in range(depth):
        mod = _ada(c8, ada_w[l], ada_b[l])[:b]
        sh_m, sc_m, gt_m, sh_f, sc_f, gt_f = jnp.split(mod, 6, axis=-1)
        z = _inproj(x, norm1_g[l], sh_m, sc_m, w_in[l].astype(BF16))
        ops = _s5_weights(ssm_lam_re[l], ssm_lam_im[l], ssm_log_dt[l], ssm_b_re[l], ssm_b_im[l],
                          ssm_c_re[l], ssm_c_im[l])
        ys = _s5(z, pwid, *ops, ssm_d[l])
        x = _mixout(z, ys, x, gt_m, pool_w[l].astype(BF16), pool_scale[l], w_pool_out[l].astype(BF16),
                    w_glu[l].astype(BF16), b_glu[l], w_out[l].astype(BF16))
        hp, idx, wt, rank, cnt = _router(x, norm2_g[l], sh_f, sc_f, w_router[l], b_router[l])
        dest, sbe, nsub, nused, zflag = _plan(idx, rank, cnt[:, 0], n_super)
        xs = _dispatch(dest, zflag, hp, n_super * SUPER)
        ye = _experts(sbe, nsub, nused, xs, w1[l], b1[l], w2[l], b2[l], n_super)
        last = l == depth - 1
        x = _combine(dest, x, wt.T, gt_f, final_norm_g, sh_o, sc_o, ye, apply_norm=last)
    return x
```

```python
import functools
import math

import jax
import jax.numpy as jnp
from jax import lax
from jax.experimental import pallas as pl
from jax.experimental.pallas import tpu as pltpu

F32 = jnp.float32
BF16 = jnp.bfloat16
I32 = jnp.int32
U32 = jnp.uint32

RMS_EPS = 1e-6
POOL_WINDOWS = (2, 4, 8, 16)
SSM_GROUP_DIM = 16
SSM_STATE = 64
LAMBDA_RE_MAX = -1e-4
TOP_K = 4
SWIGLU_ALPHA = 1.702
SWIGLU_LIMIT = 7.0

LANES = 128
V7X_VMEM_LIMIT = 56 * 1024 * 1024

CHUNK = 16
OCT = LANES // SSM_GROUP_DIM
SUB = 128
SUPER = 9 * SUB


def _cparams(sem, vmem=V7X_VMEM_LIMIT):
    return pltpu.CompilerParams(dimension_semantics=sem, vmem_limit_bytes=vmem)


def _const_spec(shape):
    nd = len(shape)
    return pl.BlockSpec(shape, lambda *_: (0,) * nd, pipeline_mode=pl.Buffered(1))


def _rms_mod(x, g, sh, sc):
    y = x * lax.rsqrt(jnp.mean(x * x, axis=-1, keepdims=True) + RMS_EPS) * g
    return y * (1.0 + sc) + sh


def _pair_width(d):
    return min(2 * LANES, d // 2)


def _pack_rows(h):
    d = h.shape[1]
    pair = _pair_width(d)
    hb = lax.bitcast_convert_type(h.astype(BF16).astype(F32), U32)
    words = []
    for n in range(d // (2 * pair)):
        lo = hb[:, 2 * pair * n:2 * pair * n + pair]
        hi = hb[:, 2 * pair * n + pair:2 * pair * (n + 1)]
        words.append((lo >> 16) | (hi & jnp.uint32(0xFFFF0000)))
    return jnp.concatenate(words, axis=1)


def _unpack_rows(w):
    d = 2 * w.shape[1]
    pair = _pair_width(d)
    cols = []
    for n in range(d // (2 * pair)):
        wn = w[:, pair * n:pair * (n + 1)]
        cols.append(lax.bitcast_convert_type(wn << 16, F32))
        cols.append(lax.bitcast_convert_type(wn & jnp.uint32(0xFFFF0000), F32))
    return jnp.concatenate(cols, axis=1)


def _ada_kernel(c_ref, w_ref, b_ref, o_ref):
    c = c_ref[...]
    ca = c * jax.nn.sigmoid(c)
    o_ref[...] = jnp.dot(ca.astype(BF16), w_ref[...].astype(BF16),
                         preferred_element_type=F32) + b_ref[...]


def _ada(c8, w, b, tn=1024):
    d, n = w.shape
    tn = math.gcd(tn, n)
    return pl.pallas_call(
        _ada_kernel,
        out_shape=jax.ShapeDtypeStruct((c8.shape[0], n), F32),
        grid=(n // tn,),
        in_specs=[pl.BlockSpec(c8.shape, lambda j: (0, 0)),
                  pl.BlockSpec((d, tn), lambda j: (0, j)),
                  pl.BlockSpec((1, tn), lambda j: (0, j))],
        out_specs=pl.BlockSpec((c8.shape[0], tn), lambda j: (0, j)),
        compiler_params=_cparams(("arbitrary",)),
        name="ada",
    )(c8, w, b.reshape(1, n))


def _inproj_kernel(x_ref, g_ref, sh_ref, sc_ref, w_ref, z_ref, *, ncol):
    h = _rms_mod(x_ref[0], g_ref[...], sh_ref[0], sc_ref[0]).astype(BF16)
    n = w_ref.shape[1]
    for c in range(n // ncol):
        z_ref[0, :, c * ncol:(c + 1) * ncol] = jnp.dot(
            h, w_ref[:, c * ncol:(c + 1) * ncol], preferred_element_type=F32).astype(BF16)


def _inproj(x, g, sh, sc, w, tm=512, ncol=1024):
    b, s, d = x.shape
    n = w.shape[1]
    ncol = math.gcd(ncol, n)
    return pl.pallas_call(
        functools.partial(_inproj_kernel, ncol=ncol),
        out_shape=jax.ShapeDtypeStruct((b, s, n), BF16),
        grid=(b, s // tm),
        in_specs=[pl.BlockSpec((1, tm, d), lambda bi, i: (bi, i, 0)),
                  _const_spec((1, d)),
                  pl.BlockSpec((1, 1, d), lambda bi, i: (bi, 0, 0)),
                  pl.BlockSpec((1, 1, d), lambda bi, i: (bi, 0, 0)),
                  _const_spec((d, n))],
        out_specs=pl.BlockSpec((1, tm, n), lambda bi, i: (bi, i, 0)),
        compiler_params=_cparams(("arbitrary", "arbitrary")),
        name="inproj",
    )(x, g.reshape(1, d), sh.reshape(b, 1, d), sc.reshape(b, 1, d), w)


def _s5_weights(lam_re, lam_im, log_dt, b_re, b_im, c_re, c_im):
    hp = lax.Precision.HIGHEST
    g_all, p = lam_re.shape
    h = b_re.shape[-1]
    no = g_all // OCT
    dt = jnp.exp(log_dt.astype(F32))[:, None]
    lre = jnp.minimum(lam_re.astype(F32), LAMBDA_RE_MAX)
    lim = lam_im.astype(F32)
    mag = jnp.exp(lre * dt)
    ang = lim * dt
    ab_re = mag * jnp.cos(ang)
    ab_im = mag * jnp.sin(ang)
    den = lre * lre + lim * lim
    nr = ab_re - 1.0
    ni = ab_im
    f_re = (nr * lre + ni * lim) / den
    f_im = (ni * lre - nr * lim) / den
    br_, bi_ = b_re.astype(F32), b_im.astype(F32)
    bb_re = f_re[..., None] * br_ - f_im[..., None] * bi_
    bb_im = f_re[..., None] * bi_ + f_im[..., None] * br_
    cr, ci = c_re.astype(F32), c_im.astype(F32)
    j = jnp.arange(CHUNK + 1, dtype=F32)[:, None, None]
    pw_mag = jnp.exp(j * (lre * dt)[None])
    pw_re = pw_mag * jnp.cos(j * ang[None])
    pw_im = pw_mag * jnp.sin(j * ang[None])
    pw_re, pw_im, bb_re, bb_im = lax.optimization_barrier((pw_re, pw_im, bb_re, bb_im))
    cp_re = cr[None] * pw_re[:, :, None, :] - ci[None] * pw_im[:, :, None, :]
    cp_im = cr[None] * pw_im[:, :, None, :] + ci[None] * pw_re[:, :, None, :]
    assert 2 * p == LANES and h * OCT == LANES
    tk = jnp.arange(CHUNK)

    bt_re = jnp.swapaxes(bb_re, 1, 2)
    bt_im = jnp.swapaxes(bb_im, 1, 2)
    cg_re = jnp.swapaxes(cp_re[:CHUNK], 0, 1).reshape(g_all, CHUNK * h, p)
    cg_im = jnp.swapaxes(cp_im[:CHUNK], 0, 1).reshape(g_all, CHUNK * h, p)
    klag = (jnp.einsum('gip,gap->gia', bt_re, cg_re, precision=hp)
            - jnp.einsum('gip,gap->gia', bt_im, cg_im, precision=hp))
    klag = jnp.transpose(klag.reshape(g_all, h, CHUNK, h), (2, 0, 1, 3))
    lag_t = klag.reshape(CHUNK, g_all * h, h)
    bt_re = bt_re[None]
    bt_im = bt_im[None]
    pwr = pw_re[CHUNK - 1 - tk][:, :, None, :]
    pwi = pw_im[CHUNK - 1 - tk][:, :, None, :]
    slabs = jnp.stack([pwr * bt_re - pwi * bt_im, pwr * bt_im + pwi * bt_re,
                       cp_re[1:], -cp_im[1:]]).reshape(4, CHUNK, g_all * h, p)
    are = pw_re[CHUNK].reshape(no, OCT, 1, p)
    aim = pw_im[CHUNK].reshape(no, OCT, 1, p)
    a1 = jnp.concatenate([are, are], axis=2).reshape(no, 1, OCT * LANES)
    a2 = jnp.concatenate([-aim, aim], axis=2).reshape(no, 1, OCT * LANES)
    return lag_t, slabs, a1, a2


def _gelu_tanh(x):
    return 0.5 * x * (1.0 + jnp.tanh(math.sqrt(2.0 / math.pi) * (x + 0.044715 * x * x * x)))


def _s5_kernel(u_ref, lag_ref, slab_ref, a1_ref, a2_ref, d_ref, y_ref,
               uscr, sscr, pscr, yscr, state, t_op, b_op, c_op):
    nb, tm, _ = u_ref.shape
    nc = tm // CHUNK
    nq = a1_ref.shape[-1] // LANES

    @pl.when(pl.program_id(1) == 0)
    def _():
        state[...] = jnp.zeros_like(state)
        rowg = lax.broadcasted_iota(I32, (LANES, LANES), 0) // SSM_GROUP_DIM
        colg = lax.broadcasted_iota(I32, (LANES, LANES), 1) // SSM_GROUP_DIM
        zero = jnp.zeros((LANES, LANES), BF16)
        rep = (lax.broadcasted_iota(I32, (SSM_GROUP_DIM, LANES), 1) % SSM_GROUP_DIM
               == lax.broadcasted_iota(I32, (SSM_GROUP_DIM, LANES), 0)).astype(BF16)
        lag = []
        for j in range(CHUNK):
            wide = jnp.dot(lag_ref[j].astype(BF16), rep, preferred_element_type=F32).astype(BF16)
            lag.append(jnp.where(rowg == colg, wide, zero))
        for k in range(CHUNK):
            in_k = jnp.concatenate([slab_ref[0, k], slab_ref[1, k]], axis=1).astype(BF16)
            out_k = jnp.concatenate([slab_ref[2, k], slab_ref[3, k]], axis=1).T.astype(BF16)
            for q in range(nq):
                b_op[k * LANES:(k + 1) * LANES, q * LANES:(q + 1) * LANES] = jnp.where(rowg == q, in_k, zero)
                c_op[q * LANES:(q + 1) * LANES, k * LANES:(k + 1) * LANES] = jnp.where(colg == q, out_k, zero)
            for t in range(CHUNK):
                t_op[k * LANES:(k + 1) * LANES, t * LANES:(t + 1) * LANES] = lag[t - k] if t >= k else zero

    for bi in range(nb):
        uscr[bi * tm:(bi + 1) * tm, :] = u_ref[bi].astype(F32)
    xk = jnp.concatenate(
        [uscr[pl.ds(k, nb * nc, stride=CHUNK), :] for k in range(CHUNK)], axis=1).astype(BF16)
    rr = nb * nc
    sc_all = jnp.dot(xk, b_op[...], preferred_element_type=F32)
    for q in range(nq):
        sscr[q * rr:(q + 1) * rr, :] = sc_all[:, q * LANES:(q + 1) * LANES]

    a1 = [a1_ref[0, :, q * LANES:(q + 1) * LANES] for q in range(nq)]
    a2 = [a2_ref[0, :, q * LANES:(q + 1) * LANES] for q in range(nq)]
    st = [state[:, q * LANES:(q + 1) * LANES] for q in range(nq)]
    for c in range(nc):
        for q in range(nq):
            pscr[pl.ds(q * rr + c, nb, stride=nc), :] = st[q]
            sc = sscr[pl.ds(q * rr + c, nb, stride=nc), :]
            st[q] = a1[q] * st[q] + a2[q] * pltpu.roll(st[q], LANES // 2, axis=1) + sc
    for q in range(nq):
        state[:, q * LANES:(q + 1) * LANES] = st[q]

    xprev = jnp.concatenate([pscr[q * rr:(q + 1) * rr, :] for q in range(nq)], axis=1)
    y = (jnp.dot(xk, t_op[...], preferred_element_type=F32)
         + jnp.dot(xprev.astype(BF16), c_op[...], preferred_element_type=F32))
    for t in range(CHUNK):
        yscr[pl.ds(t, nb * nc, stride=CHUNK), :] = y[:, t * LANES:(t + 1) * LANES]
    out = _gelu_tanh(yscr[...] + d_ref[...] * uscr[...])
    for bi in range(nb):
        y_ref[bi] = out[bi * tm:(bi + 1) * tm, :].astype(BF16)


def _s5(z, col0, lag_t, slabs, a1, a2, d_skip, tm=1024):
    nb, s, _ = z.shape
    no = a1.shape[0]
    w = no * LANES
    ns = a1.shape[-1]
    tm = min(tm, s)
    rows = nb * tm
    cb0 = col0 // LANES
    lag_spec = pl.BlockSpec((CHUNK, LANES, lag_t.shape[-1]), lambda o, i: (0, o, 0))
    slab_spec = pl.BlockSpec((4, CHUNK, LANES, slabs.shape[-1]), lambda o, i: (0, 0, o, 0))
    return pl.pallas_call(
        _s5_kernel,
        out_shape=jax.ShapeDtypeStruct((nb, s, w), BF16),
        grid=(no, s // tm),
        in_specs=[pl.BlockSpec((nb, tm, LANES), lambda o, i: (0, i, cb0 + o)),
                  lag_spec, slab_spec,
                  pl.BlockSpec((1, 1, ns), lambda o, i: (o, 0, 0)),
                  pl.BlockSpec((1, 1, ns), lambda o, i: (o, 0, 0)),
                  pl.BlockSpec((1, LANES), lambda o, i: (0, o))],
        out_specs=pl.BlockSpec((nb, tm, LANES), lambda o, i: (0, i, o)),
        scratch_shapes=[pltpu.VMEM((rows, LANES), F32),
                        pltpu.VMEM((rows // CHUNK * (ns // LANES), LANES), F32),
                        pltpu.VMEM((rows // CHUNK * (ns // LANES), LANES), F32),
                        pltpu.VMEM((rows, LANES), F32),
                        pltpu.VMEM((nb, ns), F32),
                        pltpu.VMEM((CHUNK * LANES, CHUNK * LANES), BF16),
                        pltpu.VMEM((CHUNK * LANES, ns), BF16),
                        pltpu.VMEM((ns, CHUNK * LANES), BF16)],
        compiler_params=_cparams(("arbitrary", "arbitrary")),
        name="s5",
    )(z, lag_t, slabs, a1, a2, d_skip.reshape(1, w))


def _mixout_kernel(up_ref, halo_ref, ys_ref, gp_ref, gs_ref, x_ref, gt_ref,
                   pw_ref, ps_ref, wpo_ref, wglu_ref, bglu_ref, wout_ref, o_ref, escr):
    tm = up_ref.shape[1]
    i = pl.program_id(1)
    gw = pw_ref.shape[1]
    hal = halo_ref.shape[1]
    d = o_ref.shape[2]
    escr[0:hal, :] = jnp.where(i > 0, halo_ref[0].astype(F32), 0.0)
    escr[hal:hal + tm, :] = up_ref[0].astype(F32)
    pos = (i * tm + lax.broadcasted_iota(I32, (tm, gw), 0) + 1).astype(F32)
    mixed = []
    for g, win in enumerate(POOL_WINDOWS):
        cols = slice(g * gw, (g + 1) * gw)
        cur = escr[hal:hal + tm, cols]
        acc = cur
        for j in range(1, win):
            acc = acc + escr[hal - j:hal - j + tm, cols]
        pooled = acc / jnp.minimum(pos, float(win)) - cur
        mixed.append(jnp.dot(pooled.astype(BF16), pw_ref[g], preferred_element_type=F32))
    mixed = jnp.concatenate(mixed, axis=1) * ps_ref[...]
    y_pool = jnp.dot(mixed.astype(BF16), wpo_ref[...], preferred_element_type=F32)
    glu = jnp.dot(ys_ref[0], wglu_ref[...], preferred_element_type=F32) + bglu_ref[...]
    y_ssm = glu[:, :d] * jax.nn.sigmoid(glu[:, d:])
    merged = (jax.nn.sigmoid(gp_ref[0].astype(F32)) * y_pool
              + jax.nn.sigmoid(gs_ref[0].astype(F32)) * y_ssm)
    mix = jnp.dot(merged.astype(BF16), wout_ref[...], preferred_element_type=F32)
    o_ref[0] = x_ref[0] + gt_ref[0] * mix


def _mixout(z, ys, x, gt, pool_w, pool_scale, w_pool_out, w_glu, b_glu, w_out, tm=256):
    b, s, d = x.shape
    pwid = w_pool_out.shape[0]
    hal = max(POOL_WINDOWS)
    tpb = tm // hal
    gcol = (pwid + ys.shape[2]) // d
    return pl.pallas_call(
        _mixout_kernel,
        out_shape=jax.ShapeDtypeStruct((b, s, d), F32),
        grid=(b, s // tm),
        in_specs=[pl.BlockSpec((1, tm, pwid), lambda bi, i: (bi, i, 0)),
                  pl.BlockSpec((1, hal, pwid), lambda bi, i: (bi, jnp.maximum(i * tpb - 1, 0), 0)),
                  pl.BlockSpec((1, tm, ys.shape[2]), lambda bi, i: (bi, i, 0)),
                  pl.BlockSpec((1, tm, d), lambda bi, i: (bi, i, gcol)),
                  pl.BlockSpec((1, tm, d), lambda bi, i: (bi, i, gcol + 1)),
                  pl.BlockSpec((1, tm, d), lambda bi, i: (bi, i, 0)),
                  pl.BlockSpec((1, 1, d), lambda bi, i: (bi, 0, 0)),
                  _const_spec(pool_w.shape),
                  _const_spec((1, pwid)),
                  _const_spec(w_pool_out.shape),
                  _const_spec(w_glu.shape),
                  _const_spec((1, w_glu.shape[1])),
                  _const_spec(w_out.shape)],
        out_specs=pl.BlockSpec((1, tm, d), lambda bi, i: (bi, i, 0)),
        scratch_shapes=[pltpu.VMEM((hal + tm, pwid), F32)],
        compiler_params=_cparams(("arbitrary", "arbitrary")),
        name="mixout",
    )(z, z, ys, z, z, x, gt.reshape(b, 1, d), pool_w, pool_scale.reshape(1, pwid),
      w_pool_out, w_glu, b_glu.reshape(1, -1), w_out)


def _router_kernel(x_ref, g_ref, sh_ref, sc_ref, wr_ref, br_ref,
                   hp_ref, idx_ref, wt_ref, rank_ref, cnt_ref, carry, *, n_exp):
    tm = x_ref.shape[1]
    first = (pl.program_id(0) == 0) & (pl.program_id(1) == 0)

    @pl.when(first)
    def _():
        carry[...] = jnp.zeros_like(carry)

    h2 = _rms_mod(x_ref[0], g_ref[...], sh_ref[0], sc_ref[0])
    hp_ref[...] = _pack_rows(h2)
    h_hi = h2.astype(BF16)
    h_lo = (h2 - h_hi.astype(F32)).astype(BF16)
    logits = (jnp.dot(h_hi, wr_ref[0], preferred_element_type=F32)
              + jnp.dot(h_hi, wr_ref[1], preferred_element_type=F32)
              + jnp.dot(h_lo, wr_ref[0], preferred_element_type=F32)) + br_ref[...]
    lt = logits.T[:n_exp, :]
    eid = lax.broadcasted_iota(I32, lt.shape, 0).astype(F32)
    vals, idxs, hots = [], [], []
    v = lt
    for _ in range(TOP_K):
        m = jnp.max(v, axis=0, keepdims=True)
        sel = jnp.min(jnp.where(v == m, eid, float(n_exp)), axis=0, keepdims=True)
        hot = eid == sel
        vals.append(m)
        idxs.append(sel)
        hots.append(hot)
        v = jnp.where(hot, -jnp.inf, v)
    ex = [jnp.exp(m - vals[0]) for m in vals]
    tot = ex[0] + ex[1] + ex[2] + ex[3]
    msum = sum(h.astype(F32) for h in hots)
    tri = (lax.broadcasted_iota(I32, (tm, tm), 0) < lax.broadcasted_iota(I32, (tm, tm), 1))
    before = jnp.dot(msum.astype(BF16), tri.astype(BF16), preferred_element_type=F32) + carry[:, 0:1]
    for k in range(TOP_K):
        idx_ref[k:k + 1, :] = idxs[k].astype(I32)
        wt_ref[k:k + 1, :] = ex[k] / tot
        rank_ref[k:k + 1, :] = jnp.sum(jnp.where(hots[k], before, 0.0), axis=0,
                                       keepdims=True).astype(I32)
    carry[...] = carry[...] + jnp.sum(msum, axis=1, keepdims=True)
    cnt_ref[...] = carry[...].astype(I32)


def _router(x1, g, sh, sc, w_router, b_router, tm=512):
    b, s, d = x1.shape
    t = b * s
    e = w_router.shape[1]
    wd = d // 2
    wr = jnp.dot(w_router.astype(F32), jnp.eye(e, LANES, dtype=F32), precision=lax.Precision.HIGHEST)
    wr_hi = wr.astype(BF16)
    wr = jnp.stack([wr_hi, (wr - wr_hi.astype(F32)).astype(BF16)])
    br = jnp.full((1, LANES), -1e30, F32).at[0, :e].set(b_router.astype(F32))
    spb = s // tm
    return pl.pallas_call(
        functools.partial(_router_kernel, n_exp=e),
        out_shape=[jax.ShapeDtypeStruct((t, wd), U32),
                   jax.ShapeDtypeStruct((TOP_K, t), I32),
                   jax.ShapeDtypeStruct((TOP_K, t), F32),
                   jax.ShapeDtypeStruct((TOP_K, t), I32),
                   jax.ShapeDtypeStruct((e, LANES), I32)],
        grid=(b, spb),
        in_specs=[pl.BlockSpec((1, tm, d), lambda bi, i: (bi, i, 0)),
                  _const_spec((1, d)),
                  pl.BlockSpec((1, 1, d), lambda bi, i: (bi, 0, 0)),
                  pl.BlockSpec((1, 1, d), lambda bi, i: (bi, 0, 0)),
                  _const_spec((2, d, LANES)),
                  _const_spec((1, LANES))],
        out_specs=[pl.BlockSpec((tm, wd), lambda bi, i: (bi * spb + i, 0)),
                   pl.BlockSpec((TOP_K, tm), lambda bi, i: (0, bi * spb + i)),
                   pl.BlockSpec((TOP_K, tm), lambda bi, i: (0, bi * spb + i)),
                   pl.BlockSpec((TOP_K, tm), lambda bi, i: (0, bi * spb + i)),
                   pl.BlockSpec((e, LANES), lambda bi, i: (0, 0))],
        scratch_shapes=[pltpu.VMEM((e, LANES), F32)],
        compiler_params=_cparams(("arbitrary", "arbitrary")),
        name="router",
    )(x1, g.reshape(1, d), sh.reshape(b, 1, d), sc.reshape(b, 1, d), wr, br)


def _dispatch_kernel(dest_ref, zflag_ref, hp_ref, xs_ref, zbuf, sem, zsem, *, n_tok):
    tm = hp_ref.shape[0]
    i = pl.program_id(0)

    @pl.when(i == 0)
    def _():
        zbuf[...] = jnp.zeros_like(zbuf)

        def zcopy(q):
            r0 = pl.multiple_of(q * SUB, SUB)
            return pltpu.make_async_copy(zbuf, xs_ref.at[pl.ds(r0, SUB)], zsem)

        def zstart(q, carry):
            pl.when(zflag_ref[q] != 0)(lambda: zcopy(q).start())
            return carry

        def zwait(q, carry):
            pl.when(zflag_ref[q] != 0)(lambda: zcopy(q).wait())
            return carry

        lax.fori_loop(0, zflag_ref.shape[0], zstart, 0)
        lax.fori_loop(0, zflag_ref.shape[0], zwait, 0)

    def issue(t, carry):
        for k in range(TOP_K):
            dst = dest_ref[k * n_tok + i * tm + t]
            pltpu.make_async_copy(hp_ref.at[pl.ds(t, 1)], xs_ref.at[pl.ds(dst, 1)],
                                  sem).start(priority=k % 2)
        return carry

    lax.fori_loop(0, tm, issue, 0, unroll=8)
    for k in range(TOP_K):
        pltpu.make_async_copy(hp_ref, xs_ref.at[pl.ds(0, tm)], sem).wait()


def _dispatch(dest_flat, zflag, hp, n_rows, tm=256):
    t, wd = hp.shape
    return pl.pallas_call(
        functools.partial(_dispatch_kernel, n_tok=t),
        out_shape=jax.ShapeDtypeStruct((n_rows, wd), U32),
        grid_spec=pltpu.PrefetchScalarGridSpec(
            num_scalar_prefetch=2,
            grid=(t // tm,),
            in_specs=[pl.BlockSpec((tm, wd), lambda i, *_: (i, 0))],
            out_specs=pl.BlockSpec(memory_space=pl.ANY),
            scratch_shapes=[pltpu.VMEM((SUB, wd), U32),
                            pltpu.SemaphoreType.DMA,
                            pltpu.SemaphoreType.DMA]),
        compiler_params=_cparams(("arbitrary",)),
        name="dispatch",
    )(dest_flat, zflag, hp)


def _experts_kernel(sbe_ref, nsub_ref, nused_ref, x_ref, b1_ref, b2_ref, w1_hbm, w2_hbm, o_ref,
                    xb, actb, yb, wt, sem):
    s = pl.program_id(0)
    n = nsub_ref[s]
    nused = nused_ref[0]
    nj, _, tf = actb.shape
    nn, _, pair = yb.shape
    d = xb.shape[1]
    f = nj * tf
    nslot = wt.shape[0]
    ntile = nj + nn
    per = SUPER // SUB

    def a_copies(e, j, slot):
        return [pltpu.make_async_copy(
            w1_hbm.at[e, :, pl.ds(pl.multiple_of(part * f + j * tf, tf), tf)],
            wt.at[slot, part, 0:d, 0:tf], sem.at[slot]) for part in range(2)]

    def b_copies(e, c, slot):
        return [pltpu.make_async_copy(
            w2_hbm.at[e, :, pl.ds(pl.multiple_of((2 * c + part) * pair, pair), pair)],
            wt.at[slot, part, 0:f, 0:pair], sem.at[slot]) for part in range(2)]

    def slot_of(sb, q):
        return lax.rem(sb * ntile + q, nslot)

    def start_tile(sb, q):
        e = sbe_ref[sb]
        slot = slot_of(sb, q)

        @pl.when(q < nj)
        def _():
            for cp in a_copies(e, q, slot):
                cp.start()

        @pl.when(q >= nj)
        def _():
            for cp in b_copies(e, q - nj, slot):
                cp.start()

    def prefetch(q):
        qq = q + nslot - 1

        @pl.when(qq < ntile)
        def _():
            start_tile(s, qq)

        @pl.when((qq >= ntile) & (s + 1 < nused))
        def _():
            start_tile(s + 1, qq - ntile)

    @pl.when((s == 0) & (nused > 0))
    def _():
        for q in range(nslot - 1):
            start_tile(0, jnp.int32(q))

    def run(v):
        m = v * SUB
        e = sbe_ref[s]
        for r in range(v):
            rows = slice(r * SUB, (r + 1) * SUB)
            xb[rows, :] = _unpack_rows(x_ref[rows, :]).astype(BF16)

        def a_body(j, carry):
            slot = slot_of(s, j)
            for cp in a_copies(e, j, slot):
                cp.wait()
            prefetch(j)
            x = xb[0:m, :]
            g = jnp.dot(x, wt[slot, 0, 0:d, 0:tf].astype(BF16), preferred_element_type=F32) + b1_ref[0, j]
            l = jnp.dot(x, wt[slot, 1, 0:d, 0:tf].astype(BF16), preferred_element_type=F32) + b1_ref[0, nj + j]
            xg = jnp.minimum(g, SWIGLU_LIMIT)
            xl = jnp.clip(l, -SWIGLU_LIMIT, SWIGLU_LIMIT)
            act = xg * jax.nn.sigmoid(SWIGLU_ALPHA * xg) * (xl + 1.0)
            actb[j, 0:m, :] = act.astype(BF16)
            return carry

        lax.fori_loop(0, nj, a_body, 0)

        def b_body(c, carry):
            slot = slot_of(s, nj + c)
            for cp in b_copies(e, c, slot):
                cp.wait()
            prefetch(nj + c)
            halves = []
            for part in range(2):
                w = wt[slot, part, 0:f, 0:pair].astype(BF16)
                y = b2_ref[0, 2 * c + part] + jnp.dot(actb[0, 0:m, :], w[0:tf, :],
                                                      preferred_element_type=F32)
                for j in range(1, nj):
                    y = y + jnp.dot(actb[j, 0:m, :], w[j * tf:(j + 1) * tf, :],
                                    preferred_element_type=F32)
                halves.append(y)
            yb[c, 0:m, :] = _pack_rows(jnp.concatenate(halves, axis=1))
            return carry

        lax.fori_loop(0, nn, b_body, 0)

        for c in range(nn):
            o_ref[0:m, c * pair:(c + 1) * pair] = yb[c, 0:m, :]
        if m < SUPER:
            o_ref[m:SUPER, :] = jnp.zeros((SUPER - m, o_ref.shape[1]), U32)

    for v in range(1, per + 1):
        pl.when((s < nused) & (n == v))(functools.partial(run, v))

    @pl.when(s >= nused)
    def _():
        o_ref[...] = jnp.zeros(o_ref.shape, U32)


def _experts(sbe, nsub, nused, xs, w1, b1, w2, b2, n_super, tf=256, nslot=3):
    n_exp, d, f2 = w1.shape
    f = f2 // 2
    tf = min(tf, f)
    pair = _pair_width(d)
    nj = f // tf
    nn = d // (2 * pair)
    wd = xs.shape[1]

    def x_map(s, sbe, nsub, nused):
        return (jnp.minimum(s, nused[0] - 1), 0)

    return pl.pallas_call(
        _experts_kernel,
        out_shape=jax.ShapeDtypeStruct((n_super * SUPER, wd), U32),
        grid_spec=pltpu.PrefetchScalarGridSpec(
            num_scalar_prefetch=3,
            grid=(n_super,),
            in_specs=[pl.BlockSpec((SUPER, wd), x_map),
                      pl.BlockSpec((1, 2 * nj, 1, tf), lambda s, sbe, nsub, nu: (sbe[s], 0, 0, 0)),
                      pl.BlockSpec((1, 2 * nn, 1, pair), lambda s, sbe, nsub, nu: (sbe[s], 0, 0, 0)),
                      pl.BlockSpec(memory_space=pl.ANY),
                      pl.BlockSpec(memory_space=pl.ANY)],
            out_specs=pl.BlockSpec((SUPER, wd), lambda s, *_: (s, 0)),
            scratch_shapes=[pltpu.VMEM((SUPER, d), BF16),
                            pltpu.VMEM((nj, SUPER, tf), BF16),
                            pltpu.VMEM((nn, SUPER, pair), U32),
                            pltpu.VMEM((nslot, 2, max(d, f), max(tf, pair)), F32),
                            pltpu.SemaphoreType.DMA((nslot,))]),
        compiler_params=_cparams(("arbitrary",)),
        name="experts",
    )(sbe, nsub, nused, xs, b1.reshape(n_exp, 2 * nj, 1, tf), b2.reshape(n_exp, 2 * nn, 1, pair), w1, w2)


def _combine_kernel(dest_ref, x_ref, wt_ref, gt_ref, g_ref, sh_ref, sc_ref, ys_ref, o_ref,
                    gbuf, sem, *, n_tok, tiles_per_batch, apply_norm):
    tm = x_ref.shape[1]
    i = pl.program_id(0) * tiles_per_batch + pl.program_id(1)
    n_tiles = pl.num_programs(0) * tiles_per_batch
    slot = lax.rem(i, 2)

    def gather(tile, buf):
        def issue(t, carry):
            for k in range(TOP_K):
                src = dest_ref[k * n_tok + tile * tm + t]
                pltpu.make_async_copy(ys_ref.at[pl.ds(src, 1)], gbuf.at[buf, k, pl.ds(t, 1)],
                                      sem.at[buf]).start(priority=k % 2)
            return carry

        lax.fori_loop(0, tm, issue, 0, unroll=8)

    @pl.when(i == 0)
    def _():
        gather(i, slot)

    @pl.when(i + 1 < n_tiles)
    def _():
        gather(i + 1, 1 - slot)

    for k in range(TOP_K):
        pltpu.make_async_copy(ys_ref.at[pl.ds(0, tm)], gbuf.at[slot, k], sem.at[slot]).wait()
    wt = wt_ref[...]
    ffn = wt[:, 0:1] * _unpack_rows(gbuf[slot, 0])
    for k in range(1, TOP_K):
        ffn = ffn + wt[:, k:k + 1] * _unpack_rows(gbuf[slot, k])
    x2 = x_ref[0] + gt_ref[0] * ffn
    o_ref[0] = _rms_mod(x2, g_ref[...], sh_ref[0], sc_ref[0]) if apply_norm else x2


def _combine(dest_flat, x1, wt_t, gt, g, sh, sc, ys, apply_norm, tm=256):
    b, s, d = x1.shape
    t = b * s
    spb = s // tm
    wd = ys.shape[1]
    return pl.pallas_call(
        functools.partial(_combine_kernel, n_tok=t, tiles_per_batch=spb, apply_norm=apply_norm),
        out_shape=jax.ShapeDtypeStruct((b, s, d), F32),
        grid_spec=pltpu.PrefetchScalarGridSpec(
            num_scalar_prefetch=1,
            grid=(b, spb),
            in_specs=[pl.BlockSpec((1, tm, d), lambda bi, i, *_: (bi, i, 0)),
                      pl.BlockSpec((tm, TOP_K), lambda bi, i, *_: (bi * spb + i, 0)),
                      pl.BlockSpec((1, 1, d), lambda bi, i, *_: (bi, 0, 0)),
                      pl.BlockSpec((1, d), lambda bi, i, *_: (0, 0)),
                      pl.BlockSpec((1, 1, d), lambda bi, i, *_: (bi, 0, 0)),
                      pl.BlockSpec((1, 1, d), lambda bi, i, *_: (bi, 0, 0)),
                      pl.BlockSpec(memory_space=pl.ANY)],
            out_specs=pl.BlockSpec((1, tm, d), lambda bi, i, *_: (bi, i, 0)),
            scratch_shapes=[pltpu.VMEM((2, TOP_K, tm, wd), U32),
                            pltpu.SemaphoreType.DMA((2,))]),
        compiler_params=_cparams(("arbitrary", "arbitrary")),
        name="combine",
    )(dest_flat, x1, wt_t, gt.reshape(b, 1, d), g.reshape(1, d), sh.reshape(b, 1, d),
      sc.reshape(b, 1, d), ys)


def _dest_kernel(start_ref, rps_ref, idx_ref, rank_ref, o_ref):
    idx = idx_ref[...]
    start = jnp.zeros(idx.shape, I32)
    per = jnp.ones(idx.shape, I32)
    for e in range(start_ref.shape[0]):
        sel = idx == e
        start = jnp.where(sel, start_ref[e], start)
        per = jnp.where(sel, rps_ref[e], per)
    rank = rank_ref[...]
    q = jnp.floor(rank.astype(F32) / per.astype(F32)).astype(I32)
    rem = rank - q * per
    q = jnp.where(rem < 0, q - 1, jnp.where(rem >= per, q + 1, q))
    rem = rank - q * per
    o_ref[...] = (start + q) * SUPER + rem


def _dest(sb_start, rps, idx, rank):
    return pl.pallas_call(
        _dest_kernel,
        out_shape=jax.ShapeDtypeStruct(idx.shape, I32),
        grid_spec=pltpu.PrefetchScalarGridSpec(
            num_scalar_prefetch=2,
            grid=(1,),
            in_specs=[pl.BlockSpec(idx.shape, lambda i, *_: (0, 0)),
                      pl.BlockSpec(idx.shape, lambda i, *_: (0, 0))],
            out_specs=pl.BlockSpec(idx.shape, lambda i, *_: (0, 0))),
        compiler_params=_cparams(("arbitrary",)),
        name="dest",
    )(sb_start.astype(I32), rps.astype(I32), idx, rank)


def _plan(idx, rank, counts, n_super):
    n_exp = counts.shape[0]
    nsb = (counts + SUPER - 1) // SUPER
    rps = (counts + jnp.maximum(nsb, 1) - 1) // jnp.maximum(nsb, 1)
    rps = jnp.maximum(rps, 1).astype(I32)
    sb_end = jnp.cumsum(nsb)
    sb_start = sb_end - nsb
    dest = _dest(sb_start, rps, idx, rank)
    s_ids = jnp.arange(n_super, dtype=I32)
    nused = sb_end[-1]
    s_eff = jnp.minimum(s_ids, nused - 1)
    sbe = jnp.minimum(jnp.sum((sb_end[None, :] <= s_eff[:, None]).astype(I32), axis=1), n_exp - 1)
    hot = sbe[:, None] == jnp.arange(n_exp, dtype=I32)[None, :]
    look = lambda tab: jnp.sum(jnp.where(hot, tab.astype(I32)[None, :], 0), axis=1)
    s_rps, s_cnt, s_start = look(rps), look(counts), look(sb_start)
    in_sb = jnp.minimum(s_rps, s_cnt - (s_ids - s_start) * s_rps)
    per = SUPER // SUB
    left = in_sb[:, None] - jnp.arange(per, dtype=I32)[None, :] * SUB
    vrows = jnp.where((s_ids < nused)[:, None], jnp.clip(left, 0, SUB), 0)
    nsub = jnp.sum((vrows > 0).astype(I32), axis=1)
    zflag = (vrows < SUB).astype(I32).reshape(-1)
    return dest.reshape(-1).astype(I32), sbe.astype(I32), nsub, nused.reshape(1).astype(I32), zflag


def kernel(x, c, ada_w, ada_b, norm1_g, w_in, pool_w, pool_scale, w_pool_out, ssm_lam_re, ssm_lam_im,
           ssm_log_dt, ssm_b_re, ssm_b_im, ssm_c_re, ssm_c_im, ssm_d, w_glu, b_glu, w_out, norm2_g,
           w_router, b_router, w1, b1, w2, b2, final_ada_w, final_ada_b, final_norm_g):
    b, s, d = x.shape
    t = b * s
    depth = ada_w.shape[0]
    n_exp = w_router.shape[-1]
    pwid = w_pool_out.shape[1]
    n_super = (t * TOP_K) // SUPER + n_exp

    c8 = jnp.zeros((8, d), F32).at[:b].set(c.astype(F32))
    fmod = _ada(c8, final_ada_w, final_ada_b)[:b]
    sh_o, sc_o = jnp.split(fmod, 2, axis=-1)
    for l in range(depth):
        mod = _ada(c8, ada_w[l], ada_b[l])[:b]
        sh_m, sc_m, gt_m, sh_f, sc_f, gt_f = jnp.split(mod, 6, axis=-1)
        z = _inproj(x, norm1_g[l], sh_m, sc_m, w_in[l].astype(BF16))
        ops = _s5_weights(ssm_lam_re[l], ssm_lam_im[l], ssm_log_dt[l], ssm_b_re[l], ssm_b_im[l],
                          ssm_c_re[l], ssm_c_im[l])
        ys = _s5(z, pwid, *ops, ssm_d[l])
        x = _mixout(z, ys, x, gt_m, pool_w[l].astype(BF16), pool_scale[l], w_pool_out[l].astype(BF16),
                    w_glu[l].astype(BF16), b_glu[l], w_out[l].astype(BF16))
        hp, idx, wt, rank, cnt = _router(x, norm2_g[l], sh_f, sc_f, w_router[l], b_router[l])
        dest, sbe, nsub, nused, zflag = _plan(idx, rank, cnt[:, 0], n_super)
        xs = _dispatch(dest, zflag, hp, n_super * SUPER)
        ye = _experts(sbe, nsub, nused, xs, w1[l], b1[l], w2[l], b2[l], n_super)
        last = l == depth - 1
        x = _combine(dest, x, wt.T, gt_f, final_norm_g, sh_o, sc_o, ye, apply_norm=last)
    return x
```

```python
import functools
import math

import jax
import jax.numpy as jnp
from jax import lax
from jax.experimental import pallas as pl
from jax.experimental.pallas import tpu as pltpu

F32 = jnp.float32
BF16 = jnp.bfloat16
I32 = jnp.int32
U32 = jnp.uint32

RMS_EPS = 1e-6
POOL_WINDOWS = (2, 4, 8, 16)
SSM_GROUP_DIM = 16
SSM_STATE = 64
LAMBDA_RE_MAX = -1e-4
TOP_K = 4
SWIGLU_ALPHA = 1.702
SWIGLU_LIMIT = 7.0

LANES = 128
V7X_VMEM_LIMIT = 56 * 1024 * 1024

CHUNK = 16
OCT = LANES // SSM_GROUP_DIM
SUB = 128
SUPER = 9 * SUB


def _cparams(sem, vmem=V7X_VMEM_LIMIT):
    return pltpu.CompilerParams(dimension_semantics=sem, vmem_limit_bytes=vmem)


def _const_spec(shape):
    nd = len(shape)
    return pl.BlockSpec(shape, lambda *_: (0,) * nd, pipeline_mode=pl.Buffered(1))


def _rms_mod(x, g, sh, sc):
    y = x * lax.rsqrt(jnp.mean(x * x, axis=-1, keepdims=True) + RMS_EPS) * g
    return y * (1.0 + sc) + sh


def _pair_width(d):
    return min(2 * LANES, d // 2)


def _pack_rows(h):
    d = h.shape[1]
    pair = _pair_width(d)
    hb = lax.bitcast_convert_type(h.astype(BF16).astype(F32), U32)
    words = []
    for n in range(d // (2 * pair)):
        lo = hb[:, 2 * pair * n:2 * pair * n + pair]
        hi = hb[:, 2 * pair * n + pair:2 * pair * (n + 1)]
        words.append((lo >> 16) | (hi & jnp.uint32(0xFFFF0000)))
    return jnp.concatenate(words, axis=1)


def _unpack_rows(w):
    d = 2 * w.shape[1]
    pair = _pair_width(d)
    cols = []
    for n in range(d // (2 * pair)):
        wn = w[:, pair * n:pair * (n + 1)]
        cols.append(lax.bitcast_convert_type(wn << 16, F32))
        cols.append(lax.bitcast_convert_type(wn & jnp.uint32(0xFFFF0000), F32))
    return jnp.concatenate(cols, axis=1)


def _ada_kernel(c_ref, w_ref, b_ref, o_ref):
    c = c_ref[...]
    ca = c * jax.nn.sigmoid(c)
    o_ref[...] = jnp.dot(ca.astype(BF16), w_ref[...].astype(BF16),
                         preferred_element_type=F32) + b_ref[...]


def _ada(c8, w, b, tn=1024):
    d, n = w.shape
    tn = math.gcd(tn, n)
    return pl.pallas_call(
        _ada_kernel,
        out_shape=jax.ShapeDtypeStruct((c8.shape[0], n), F32),
        grid=(n // tn,),
        in_specs=[pl.BlockSpec(c8.shape, lambda j: (0, 0)),
                  pl.BlockSpec((d, tn), lambda j: (0, j)),
                  pl.BlockSpec((1, tn), lambda j: (0, j))],
        out_specs=pl.BlockSpec((c8.shape[0], tn), lambda j: (0, j)),
        compiler_params=_cparams(("arbitrary",)),
        name="ada",
    )(c8, w, b.reshape(1, n))


def _inproj_kernel(x_ref, g_ref, sh_ref, sc_ref, w_ref, z_ref, *, ncol):
    h = _rms_mod(x_ref[0], g_ref[...], sh_ref[0], sc_ref[0]).astype(BF16)
    n = w_ref.shape[1]
    for c in range(n // ncol):
        z_ref[0, :, c * ncol:(c + 1) * ncol] = jnp.dot(
            h, w_ref[:, c * ncol:(c + 1) * ncol], preferred_element_type=F32).astype(BF16)


def _inproj(x, g, sh, sc, w, tm=512, ncol=1024):
    b, s, d = x.shape
    n = w.shape[1]
    ncol = math.gcd(ncol, n)
    return pl.pallas_call(
        functools.partial(_inproj_kernel, ncol=ncol),
        out_shape=jax.ShapeDtypeStruct((b, s, n), BF16),
        grid=(b, s // tm),
        in_specs=[pl.BlockSpec((1, tm, d), lambda bi, i: (bi, i, 0)),
                  _const_spec((1, d)),
                  pl.BlockSpec((1, 1, d), lambda bi, i: (bi, 0, 0)),
                  pl.BlockSpec((1, 1, d), lambda bi, i: (bi, 0, 0)),
                  _const_spec((d, n))],
        out_specs=pl.BlockSpec((1, tm, n), lambda bi, i: (bi, i, 0)),
        compiler_params=_cparams(("arbitrary", "arbitrary")),
        name="inproj",
    )(x, g.reshape(1, d), sh.reshape(b, 1, d), sc.reshape(b, 1, d), w)


def _s5_weights(lam_re, lam_im, log_dt, b_re, b_im, c_re, c_im):
    hp = lax.Precision.HIGHEST
    g_all, p = lam_re.shape
    h = b_re.shape[-1]
    no = g_all // OCT
    dt = jnp.exp(log_dt.astype(F32))[:, None]
    lre = jnp.minimum(lam_re.astype(F32), LAMBDA_RE_MAX)
    lim = lam_im.astype(F32)
    mag = jnp.exp(lre * dt)
    ang = lim * dt
    ab_re = mag * jnp.cos(ang)
    ab_im = mag * jnp.sin(ang)
    den = lre * lre + lim * lim
    nr = ab_re - 1.0
    ni = ab_im
    f_re = (nr * lre + ni * lim) / den
    f_im = (ni * lre - nr * lim) / den
    br_, bi_ = b_re.astype(F32), b_im.astype(F32)
    bb_re = f_re[..., None] * br_ - f_im[..., None] * bi_
    bb_im = f_re[..., None] * bi_ + f_im[..., None] * br_
    cr, ci = c_re.astype(F32), c_im.astype(F32)
    j = jnp.arange(CHUNK + 1, dtype=F32)[:, None, None]
    pw_mag = jnp.exp(j * (lre * dt)[None])
    pw_re = pw_mag * jnp.cos(j * ang[None])
    pw_im = pw_mag * jnp.sin(j * ang[None])
    pw_re, pw_im, bb_re, bb_im = lax.optimization_barrier((pw_re, pw_im, bb_re, bb_im))
    cp_re = cr[None] * pw_re[:, :, None, :] - ci[None] * pw_im[:, :, None, :]
    cp_im = cr[None] * pw_im[:, :, None, :] + ci[None] * pw_re[:, :, None, :]
    assert 2 * p == LANES and h * OCT == LANES
    tk = jnp.arange(CHUNK)

    bt_re = jnp.swapaxes(bb_re, 1, 2)
    bt_im = jnp.swapaxes(bb_im, 1, 2)
    cg_re = jnp.swapaxes(cp_re[:CHUNK], 0, 1).reshape(g_all, CHUNK * h, p)
    cg_im = jnp.swapaxes(cp_im[:CHUNK], 0, 1).reshape(g_all, CHUNK * h, p)
    klag = (jnp.einsum('gip,gap->gia', bt_re, cg_re, precision=hp)
            - jnp.einsum('gip,gap->gia', bt_im, cg_im, precision=hp))
    klag = jnp.transpose(klag.reshape(g_all, h, CHUNK, h), (2, 0, 1, 3))
    lag_t = klag.reshape(CHUNK, g_all * h, h)
    bt_re = bt_re[None]
    bt_im = bt_im[None]
    pwr = pw_re[CHUNK - 1 - tk][:, :, None, :]
    pwi = pw_im[CHUNK - 1 - tk][:, :, None, :]
    slabs = jnp.stack([pwr * bt_re - pwi * bt_im, pwr * bt_im + pwi * bt_re,
                       cp_re[1:], -cp_im[1:]]).reshape(4, CHUNK, g_all * h, p)
    are = pw_re[CHUNK].reshape(no, OCT, 1, p)
    aim = pw_im[CHUNK].reshape(no, OCT, 1, p)
    a1 = jnp.concatenate([are, are], axis=2).reshape(no, 1, OCT * LANES)
    a2 = jnp.concatenate([-aim, aim], axis=2).reshape(no, 1, OCT * LANES)
    return lag_t, slabs, a1, a2


def _gelu_tanh(x):
    return 0.5 * x * (1.0 + jnp.tanh(math.sqrt(2.0 / math.pi) * (x + 0.044715 * x * x * x)))


def _s5_kernel(u_ref, lag_ref, slab_ref, a1_ref, a2_ref, d_ref, y_ref,
               uscr, sscr, pscr, yscr, state, t_op, b_op, c_op):
    nb, tm, _ = u_ref.shape
    nc = tm // CHUNK
    nq = a1_ref.shape[-1] // LANES

    @pl.when(pl.program_id(1) == 0)
    def _():
        state[...] = jnp.zeros_like(state)
        rowg = lax.broadcasted_iota(I32, (LANES, LANES), 0) // SSM_GROUP_DIM
        colg = lax.broadcasted_iota(I32, (LANES, LANES), 1) // SSM_GROUP_DIM
        zero = jnp.zeros((LANES, LANES), BF16)
        rep = (lax.broadcasted_iota(I32, (SSM_GROUP_DIM, LANES), 1) % SSM_GROUP_DIM
               == lax.broadcasted_iota(I32, (SSM_GROUP_DIM, LANES), 0)).astype(BF16)
        lag = []
        for j in range(CHUNK):
            wide = jnp.dot(lag_ref[j].astype(BF16), rep, preferred_element_type=F32).astype(BF16)
            lag.append(jnp.where(rowg == colg, wide, zero))
        for k in range(CHUNK):
            in_k = jnp.concatenate([slab_ref[0, k], slab_ref[1, k]], axis=1).astype(BF16)
            out_k = jnp.concatenate([slab_ref[2, k], slab_ref[3, k]], axis=1).T.astype(BF16)
            for q in range(nq):
                b_op[k * LANES:(k + 1) * LANES, q * LANES:(q + 1) * LANES] = jnp.where(rowg == q, in_k, zero)
                c_op[q * LANES:(q + 1) * LANES, k * LANES:(k + 1) * LANES] = jnp.where(colg == q, out_k, zero)
            for t in range(CHUNK):
                t_op[k * LANES:(k + 1) * LANES, t * LANES:(t + 1) * LANES] = lag[t - k] if t >= k else zero

    for bi in range(nb):
        uscr[bi * tm:(bi + 1) * tm, :] = u_ref[bi].astype(F32)
    xk = jnp.concatenate(
        [uscr[pl.ds(k, nb * nc, stride=CHUNK), :] for k in range(CHUNK)], axis=1).astype(BF16)
    rr = nb * nc
    sc_all = jnp.dot(xk, b_op[...], preferred_element_type=F32)
    for q in range(nq):
        sscr[q * rr:(q + 1) * rr, :] = sc_all[:, q * LANES:(q + 1) * LANES]

    a1 = [a1_ref[0, :, q * LANES:(q + 1) * LANES] for q in range(nq)]
    a2 = [a2_ref[0, :, q * LANES:(q + 1) * LANES] for q in range(nq)]
    st = [state[:, q * LANES:(q + 1) * LANES] for q in range(nq)]
    for c in range(nc):
        for q in range(nq):
            pscr[pl.ds(q * rr + c, nb, stride=nc), :] = st[q]
            sc = sscr[pl.ds(q * rr + c, nb, stride=nc), :]
            st[q] = a1[q] * st[q] + a2[q] * pltpu.roll(st[q], LANES // 2, axis=1) + sc
    for q in range(nq):
        state[:, q * LANES:(q + 1) * LANES] = st[q]

    xprev = jnp.concatenate([pscr[q * rr:(q + 1) * rr, :] for q in range(nq)], axis=1)
    y = (jnp.dot(xk, t_op[...], preferred_element_type=F32)
         + jnp.dot(xprev.astype(BF16), c_op[...], preferred_element_type=F32))
    for t in range(CHUNK):
        yscr[pl.ds(t, nb * nc, stride=CHUNK), :] = y[:, t * LANES:(t + 1) * LANES]
    out = _gelu_tanh(yscr[...] + d_ref[...] * uscr[...])
    for bi in range(nb):
        y_ref[bi] = out[bi * tm:(bi + 1) * tm, :].astype(BF16)


def _s5(z, col0, lag_t, slabs, a1, a2, d_skip, tm=1024):
    nb, s, _ = z.shape
    no = a1.shape[0]
    w = no * LANES
    ns = a1.shape[-1]
    tm = min(tm, s)
    rows = nb * tm
    cb0 = col0 // LANES
    lag_spec = pl.BlockSpec((CHUNK, LANES, lag_t.shape[-1]), lambda o, i: (0, o, 0))
    slab_spec = pl.BlockSpec((4, CHUNK, LANES, slabs.shape[-1]), lambda o, i: (0, 0, o, 0))
    return pl.pallas_call(
        _s5_kernel,
        out_shape=jax.ShapeDtypeStruct((nb, s, w), BF16),
        grid=(no, s // tm),
        in_specs=[pl.BlockSpec((nb, tm, LANES), lambda o, i: (0, i, cb0 + o)),
                  lag_spec, slab_spec,
                  pl.BlockSpec((1, 1, ns), lambda o, i: (o, 0, 0)),
                  pl.BlockSpec((1, 1, ns), lambda o, i: (o, 0, 0)),
                  pl.BlockSpec((1, LANES), lambda o, i: (0, o))],
        out_specs=pl.BlockSpec((nb, tm, LANES), lambda o, i: (0, i, o)),
        scratch_shapes=[pltpu.VMEM((rows, LANES), F32),
                        pltpu.VMEM((rows // CHUNK * (ns // LANES), LANES), F32),
                        pltpu.VMEM((rows // CHUNK * (ns // LANES), LANES), F32),
                        pltpu.VMEM((rows, LANES), F32),
                        pltpu.VMEM((nb, ns), F32),
                        pltpu.VMEM((CHUNK * LANES, CHUNK * LANES), BF16),
                        pltpu.VMEM((CHUNK * LANES, ns), BF16),
                        pltpu.VMEM((ns, CHUNK * LANES), BF16)],
        compiler_params=_cparams(("arbitrary", "arbitrary")),
        name="s5",
    )(z, lag_t, slabs, a1, a2, d_skip.reshape(1, w))


def _mixout_kernel(up_ref, halo_ref, ys_ref, gp_ref, gs_ref, x_ref, gt_ref,
                   pw_ref, ps_ref, wpo_ref, wglu_ref, bglu_ref, wout_ref, o_ref, escr):
    tm = up_ref.shape[1]
    i = pl.program_id(1)
    gw = pw_ref.shape[1]
    hal = halo_ref.shape[1]
    d = o_ref.shape[2]
    escr[0:hal, :] = jnp.where(i > 0, halo_ref[0].astype(F32), 0.0)
    escr[hal:hal + tm, :] = up_ref[0].astype(F32)
    pos = (i * tm + lax.broadcasted_iota(I32, (tm, gw), 0) + 1).astype(F32)
    mixed = []
    for g, win in enumerate(POOL_WINDOWS):
        cols = slice(g * gw, (g + 1) * gw)
        cur = escr[hal:hal + tm, cols]
        acc = cur
        for j in range(1, win):
            acc = acc + escr[hal - j:hal - j + tm, cols]
        pooled = acc / jnp.minimum(pos, float(win)) - cur
        mixed.append(jnp.dot(pooled.astype(BF16), pw_ref[g], preferred_element_type=F32))
    mixed = jnp.concatenate(mixed, axis=1) * ps_ref[...]
    y_pool = jnp.dot(mixed.astype(BF16), wpo_ref[...], preferred_element_type=F32)
    glu = jnp.dot(ys_ref[0], wglu_ref[...], preferred_element_type=F32) + bglu_ref[...]
    y_ssm = glu[:, :d] * jax.nn.sigmoid(glu[:, d:])
    merged = (jax.nn.sigmoid(gp_ref[0].astype(F32)) * y_pool
              + jax.nn.sigmoid(gs_ref[0].astype(F32)) * y_ssm)
    mix = jnp.dot(merged.astype(BF16), wout_ref[...], preferred_element_type=F32)
    o_ref[0] = x_ref[0] + gt_ref[0] * mix


def _mixout(z, ys, x, gt, pool_w, pool_scale, w_pool_out, w_glu, b_glu, w_out, tm=256):
    b, s, d = x.shape
    pwid = w_pool_out.shape[0]
    hal = max(POOL_WINDOWS)
    tpb = tm // hal
    gcol = (pwid + ys.shape[2]) // d
    return pl.pallas_call(
        _mixout_kernel,
        out_shape=jax.ShapeDtypeStruct((b, s, d), F32),
        grid=(b, s // tm),
        in_specs=[pl.BlockSpec((1, tm, pwid), lambda bi, i: (bi, i, 0)),
                  pl.BlockSpec((1, hal, pwid), lambda bi, i: (bi, jnp.maximum(i * tpb - 1, 0), 0)),
                  pl.BlockSpec((1, tm, ys.shape[2]), lambda bi, i: (bi, i, 0)),
                  pl.BlockSpec((1, tm, d), lambda bi, i: (bi, i, gcol)),
                  pl.BlockSpec((1, tm, d), lambda bi, i: (bi, i, gcol + 1)),
                  pl.BlockSpec((1, tm, d), lambda bi, i: (bi, i, 0)),
                  pl.BlockSpec((1, 1, d), lambda bi, i: (bi, 0, 0)),
                  _const_spec(pool_w.shape),
                  _const_spec((1, pwid)),
                  _const_spec(w_pool_out.shape),
                  _const_spec(w_glu.shape),
                  _const_spec((1, w_glu.shape[1])),
                  _const_spec(w_out.shape)],
        out_specs=pl.BlockSpec((1, tm, d), lambda bi, i: (bi, i, 0)),
        scratch_shapes=[pltpu.VMEM((hal + tm, pwid), F32)],
        compiler_params=_cparams(("arbitrary", "arbitrary")),
        name="mixout",
    )(z, z, ys, z, z, x, gt.reshape(b, 1, d), pool_w, pool_scale.reshape(1, pwid),
      w_pool_out, w_glu, b_glu.reshape(1, -1), w_out)


def _router_kernel(x_ref, g_ref, sh_ref, sc_ref, wr_ref, br_ref,
                   hp_ref, idx_ref, wt_ref, rank_ref, cnt_ref, carry, *, n_exp):
    tm = x_ref.shape[1]
    first = (pl.program_id(0) == 0) & (pl.program_id(1) == 0)

    @pl.when(first)
    def _():
        carry[...] = jnp.zeros_like(carry)

    h2 = _rms_mod(x_ref[0], g_ref[...], sh_ref[0], sc_ref[0])
    hp_ref[...] = _pack_rows(h2)
    h_hi = h2.astype(BF16)
    h_lo = (h2 - h_hi.astype(F32)).astype(BF16)
    logits = (jnp.dot(h_hi, wr_ref[0], preferred_element_type=F32)
              + jnp.dot(h_hi, wr_ref[1], preferred_element_type=F32)
              + jnp.dot(h_lo, wr_ref[0], preferred_element_type=F32)) + br_ref[...]
    lt = logits.T[:n_exp, :]
    eid = lax.broadcasted_iota(I32, lt.shape, 0).astype(F32)
    vals, idxs, hots = [], [], []
    v = lt
    for _ in range(TOP_K):
        m = jnp.max(v, axis=0, keepdims=True)
        sel = jnp.min(jnp.where(v == m, eid, float(n_exp)), axis=0, keepdims=True)
        hot = eid == sel
        vals.append(m)
        idxs.append(sel)
        hots.append(hot)
        v = jnp.where(hot, -jnp.inf, v)
    ex = [jnp.exp(m - vals[0]) for m in vals]
    tot = ex[0] + ex[1] + ex[2] + ex[3]
    msum = sum(h.astype(F32) for h in hots)
    tri = (lax.broadcasted_iota(I32, (tm, tm), 0) < lax.broadcasted_iota(I32, (tm, tm), 1))
    before = jnp.dot(msum.astype(BF16), tri.astype(BF16), preferred_element_type=F32) + carry[:, 0:1]
    for k in range(TOP_K):
        idx_ref[k:k + 1, :] = idxs[k].astype(I32)
        wt_ref[k:k + 1, :] = ex[k] / tot
        rank_ref[k:k + 1, :] = jnp.sum(jnp.where(hots[k], before, 0.0), axis=0,
                                       keepdims=True).astype(I32)
    carry[...] = carry[...] + jnp.sum(msum, axis=1, keepdims=True)
    cnt_ref[...] = carry[...].astype(I32)


def _router(x1, g, sh, sc, w_router, b_router, tm=512):
    b, s, d = x1.shape
    t = b * s
    e = w_router.shape[1]
    wd = d // 2
    wr = jnp.dot(w_router.astype(F32), jnp.eye(e, LANES, dtype=F32), precision=lax.Precision.HIGHEST)
    wr_hi = wr.astype(BF16)
    wr = jnp.stack([wr_hi, (wr - wr_hi.astype(F32)).astype(BF16)])
    br = jnp.full((1, LANES), -1e30, F32).at[0, :e].set(b_router.astype(F32))
    spb = s // tm
    return pl.pallas_call(
        functools.partial(_router_kernel, n_exp=e),
        out_shape=[jax.ShapeDtypeStruct((t, wd), U32),
                   jax.ShapeDtypeStruct((TOP_K, t), I32),
                   jax.ShapeDtypeStruct((TOP_K, t), F32),
                   jax.ShapeDtypeStruct((TOP_K, t), I32),
                   jax.ShapeDtypeStruct((e, LANES), I32)],
        grid=(b, spb),
        in_specs=[pl.BlockSpec((1, tm, d), lambda bi, i: (bi, i, 0)),
                  _const_spec((1, d)),
                  pl.BlockSpec((1, 1, d), lambda bi, i: (bi, 0, 0)),
                  pl.BlockSpec((1, 1, d), lambda bi, i: (bi, 0, 0)),
                  _const_spec((2, d, LANES)),
                  _const_spec((1, LANES))],
        out_specs=[pl.BlockSpec((tm, wd), lambda bi, i: (bi * spb + i, 0)),
                   pl.BlockSpec((TOP_K, tm), lambda bi, i: (0, bi * spb + i)),
                   pl.BlockSpec((TOP_K, tm), lambda bi, i: (0, bi * spb + i)),
                   pl.BlockSpec((TOP_K, tm), lambda bi, i: (0, bi * spb + i)),
                   pl.BlockSpec((e, LANES), lambda bi, i: (0, 0))],
        scratch_shapes=[pltpu.VMEM((e, LANES), F32)],
        compiler_params=_cparams(("arbitrary", "arbitrary")),
        name="router",
    )(x1, g.reshape(1, d), sh.reshape(b, 1, d), sc.reshape(b, 1, d), wr, br)


def _dispatch_kernel(dest_ref, zflag_ref, hp_ref, xs_ref, zbuf, sem, zsem, *, n_tok):
    tm = hp_ref.shape[0]
    i = pl.program_id(0)

    @pl.when(i == 0)
    def _():
        zbuf[...] = jnp.zeros_like(zbuf)

        def zcopy(q):
            r0 = pl.multiple_of(q * SUB, SUB)
            return pltpu.make_async_copy(zbuf, xs_ref.at[pl.ds(r0, SUB)], zsem)

        def zstart(q, carry):
            pl.when(zflag_ref[q] != 0)(lambda: zcopy(q).start())
            return carry

        def zwait(q, carry):
            pl.when(zflag_ref[q] != 0)(lambda: zcopy(q).wait())
            return carry

        lax.fori_loop(0, zflag_ref.shape[0], zstart, 0)
        lax.fori_loop(0, zflag_ref.shape[0], zwait, 0)

    def issue(t, carry):
        for k in range(TOP_K):
            dst = dest_ref[k * n_tok + i * tm + t]
            pltpu.make_async_copy(hp_ref.at[pl.ds(t, 1)], xs_ref.at[pl.ds(dst, 1)],
                                  sem).start(priority=k % 2)
        return carry

    lax.fori_loop(0, tm, issue, 0, unroll=8)
    for k in range(TOP_K):
        pltpu.make_async_copy(hp_ref, xs_ref.at[pl.ds(0, tm)], sem).wait()


def _dispatch(dest_flat, zflag, hp, n_rows, tm=1024):
    t, wd = hp.shape
    tm = min(tm, t)
    return pl.pallas_call(
        functools.partial(_dispatch_kernel, n_tok=t),
        out_shape=jax.ShapeDtypeStruct((n_rows, wd), U32),
        grid_spec=pltpu.PrefetchScalarGridSpec(
            num_scalar_prefetch=2,
            grid=(t // tm,),
            in_specs=[pl.BlockSpec((tm, wd), lambda i, *_: (i, 0))],
            out_specs=pl.BlockSpec(memory_space=pl.ANY),
            scratch_shapes=[pltpu.VMEM((SUB, wd), U32),
                            pltpu.SemaphoreType.DMA,
                            pltpu.SemaphoreType.DMA]),
        compiler_params=_cparams(("arbitrary",)),
        name="dispatch",
    )(dest_flat, zflag, hp)


def _experts_kernel(sbe_ref, nsub_ref, nused_ref, x_ref, b1_ref, b2_ref, w1_hbm, w2_hbm, o_ref,
                    xb, actb, yb, wt, sem):
    s = pl.program_id(0)
    n = nsub_ref[s]
    nused = nused_ref[0]
    nj, _, tf = actb.shape
    nn, _, pair = yb.shape
    d = xb.shape[1]
    f = nj * tf
    nslot = wt.shape[0]
    ntile = nj + nn
    per = SUPER // SUB

    def a_copies(e, j, slot):
        return [pltpu.make_async_copy(
            w1_hbm.at[e, :, pl.ds(pl.multiple_of(part * f + j * tf, tf), tf)],
            wt.at[slot, part, 0:d, 0:tf], sem.at[slot]) for part in range(2)]

    def b_copies(e, c, slot):
        return [pltpu.make_async_copy(
            w2_hbm.at[e, :, pl.ds(pl.multiple_of((2 * c + part) * pair, pair), pair)],
            wt.at[slot, part, 0:f, 0:pair], sem.at[slot]) for part in range(2)]

    def slot_of(sb, q):
        return lax.rem(sb * ntile + q, nslot)

    def start_tile(sb, q):
        e = sbe_ref[sb]
        slot = slot_of(sb, q)

        @pl.when(q < nj)
        def _():
            for cp in a_copies(e, q, slot):
                cp.start()

        @pl.when(q >= nj)
        def _():
            for cp in b_copies(e, q - nj, slot):
                cp.start()

    def prefetch(q):
        qq = q + nslot - 1

        @pl.when(qq < ntile)
        def _():
            start_tile(s, qq)

        @pl.when((qq >= ntile) & (s + 1 < nused))
        def _():
            start_tile(s + 1, qq - ntile)

    @pl.when((s == 0) & (nused > 0))
    def _():
        for q in range(nslot - 1):
            start_tile(0, jnp.int32(q))

    def run(v):
        m = v * SUB
        e = sbe_ref[s]
        for r in range(v):
            rows = slice(r * SUB, (r + 1) * SUB)
            xb[rows, :] = _unpack_rows(x_ref[rows, :]).astype(BF16)

        def a_body(j, carry):
            slot = slot_of(s, j)
            for cp in a_copies(e, j, slot):
                cp.wait()
            prefetch(j)
            x = xb[0:m, :]
            g = jnp.dot(x, wt[slot, 0, 0:d, 0:tf].astype(BF16), preferred_element_type=F32) + b1_ref[0, j]
            l = jnp.dot(x, wt[slot, 1, 0:d, 0:tf].astype(BF16), preferred_element_type=F32) + b1_ref[0, nj + j]
            xg = jnp.minimum(g, SWIGLU_LIMIT)
            xl = jnp.clip(l, -SWIGLU_LIMIT, SWIGLU_LIMIT)
            act = xg * jax.nn.sigmoid(SWIGLU_ALPHA * xg) * (xl + 1.0)
            actb[j, 0:m, :] = act.astype(BF16)
            return carry

        lax.fori_loop(0, nj, a_body, 0)

        def b_body(c, carry):
            slot = slot_of(s, nj + c)
            for cp in b_copies(e, c, slot):
                cp.wait()
            prefetch(nj + c)
            halves = []
            for part in range(2):
                w = wt[slot, part, 0:f, 0:pair].astype(BF16)
                y = b2_ref[0, 2 * c + part] + jnp.dot(actb[0, 0:m, :], w[0:tf, :],
                                                      preferred_element_type=F32)
                for j in range(1, nj):
                    y = y + jnp.dot(actb[j, 0:m, :], w[j * tf:(j + 1) * tf, :],
                                    preferred_element_type=F32)
                halves.append(y)
            yb[c, 0:m, :] = _pack_rows(jnp.concatenate(halves, axis=1))
            return carry

        lax.fori_loop(0, nn, b_body, 0)

        for c in range(nn):
            o_ref[0:m, c * pair:(c + 1) * pair] = yb[c, 0:m, :]
        if m < SUPER:
            o_ref[m:SUPER, :] = jnp.zeros((SUPER - m, o_ref.shape[1]), U32)

    for v in range(1, per + 1):
        pl.when((s < nused) & (n == v))(functools.partial(run, v))

    @pl.when(s >= nused)
    def _():
        o_ref[...] = jnp.zeros(o_ref.shape, U32)


def _experts(sbe, nsub, nused, xs, w1, b1, w2, b2, n_super, tf=256, nslot=3):
    n_exp, d, f2 = w1.shape
    f = f2 // 2
    tf = min(tf, f)
    pair = _pair_width(d)
    nj = f // tf
    nn = d // (2 * pair)
    wd = xs.shape[1]

    def x_map(s, sbe, nsub, nused):
        return (jnp.minimum(s, nused[0] - 1), 0)

    return pl.pallas_call(
        _experts_kernel,
        out_shape=jax.ShapeDtypeStruct((n_super * SUPER, wd), U32),
        grid_spec=pltpu.PrefetchScalarGridSpec(
            num_scalar_prefetch=3,
            grid=(n_super,),
            in_specs=[pl.BlockSpec((SUPER, wd), x_map),
                      pl.BlockSpec((1, 2 * nj, 1, tf), lambda s, sbe, nsub, nu: (sbe[s], 0, 0, 0)),
                      pl.BlockSpec((1, 2 * nn, 1, pair), lambda s, sbe, nsub, nu: (sbe[s], 0, 0, 0)),
                      pl.BlockSpec(memory_space=pl.ANY),
                      pl.BlockSpec(memory_space=pl.ANY)],
            out_specs=pl.BlockSpec((SUPER, wd), lambda s, *_: (s, 0)),
            scratch_shapes=[pltpu.VMEM((SUPER, d), BF16),
                            pltpu.VMEM((nj, SUPER, tf), BF16),
                            pltpu.VMEM((nn, SUPER, pair), U32),
                            pltpu.VMEM((nslot, 2, max(d, f), max(tf, pair)), F32),
                            pltpu.SemaphoreType.DMA((nslot,))]),
        compiler_params=_cparams(("arbitrary",)),
        name="experts",
    )(sbe, nsub, nused, xs, b1.reshape(n_exp, 2 * nj, 1, tf), b2.reshape(n_exp, 2 * nn, 1, pair), w1, w2)


def _combine_kernel(dest_ref, x_ref, wt_ref, gt_ref, g_ref, sh_ref, sc_ref, ys_ref, o_ref,
                    gbuf, sem, *, n_tok, tiles_per_batch, apply_norm):
    tm = x_ref.shape[1]
    i = pl.program_id(0) * tiles_per_batch + pl.program_id(1)
    n_tiles = pl.num_programs(0) * tiles_per_batch
    slot = lax.rem(i, 2)

    def gather(tile, buf):
        def issue(t, carry):
            for k in range(TOP_K):
                src = dest_ref[k * n_tok + tile * tm + t]
                pltpu.make_async_copy(ys_ref.at[pl.ds(src, 1)], gbuf.at[buf, k, pl.ds(t, 1)],
                                      sem.at[buf]).start(priority=k % 2)
            return carry

        lax.fori_loop(0, tm, issue, 0, unroll=8)

    @pl.when(i == 0)
    def _():
        gather(i, slot)

    @pl.when(i + 1 < n_tiles)
    def _():
        gather(i + 1, 1 - slot)

    for k in range(TOP_K):
        pltpu.make_async_copy(ys_ref.at[pl.ds(0, tm)], gbuf.at[slot, k], sem.at[slot]).wait()
    wt = wt_ref[...]
    ffn = wt[:, 0:1] * _unpack_rows(gbuf[slot, 0])
    for k in range(1, TOP_K):
        ffn = ffn + wt[:, k:k + 1] * _unpack_rows(gbuf[slot, k])
    x2 = x_ref[0] + gt_ref[0] * ffn
    o_ref[0] = _rms_mod(x2, g_ref[...], sh_ref[0], sc_ref[0]) if apply_norm else x2


def _combine(dest_flat, x1, wt_t, gt, g, sh, sc, ys, apply_norm, tm=512):
    b, s, d = x1.shape
    tm = min(tm, s)
    t = b * s
    spb = s // tm
    wd = ys.shape[1]
    return pl.pallas_call(
        functools.partial(_combine_kernel, n_tok=t, tiles_per_batch=spb, apply_norm=apply_norm),
        out_shape=jax.ShapeDtypeStruct((b, s, d), F32),
        grid_spec=pltpu.PrefetchScalarGridSpec(
            num_scalar_prefetch=1,
            grid=(b, spb),
            in_specs=[pl.BlockSpec((1, tm, d), lambda bi, i, *_: (bi, i, 0)),
                      pl.BlockSpec((tm, TOP_K), lambda bi, i, *_: (bi * spb + i, 0)),
                      pl.BlockSpec((1, 1, d), lambda bi, i, *_: (bi, 0, 0)),
                      pl.BlockSpec((1, d), lambda bi, i, *_: (0, 0)),
                      pl.BlockSpec((1, 1, d), lambda bi, i, *_: (bi, 0, 0)),
                      pl.BlockSpec((1, 1, d), lambda bi, i, *_: (bi, 0, 0)),
                      pl.BlockSpec(memory_space=pl.ANY)],
            out_specs=pl.BlockSpec((1, tm, d), lambda bi, i, *_: (bi, i, 0)),
            scratch_shapes=[pltpu.VMEM((2, TOP_K, tm, wd), U32),
                            pltpu.SemaphoreType.DMA((2,))]),
        compiler_params=_cparams(("arbitrary", "arbitrary")),
        name="combine",
    )(dest_flat, x1, wt_t, gt.reshape(b, 1, d), g.reshape(1, d), sh.reshape(b, 1, d),
      sc.reshape(b, 1, d), ys)


def _dest_kernel(start_ref, rps_ref, idx_ref, rank_ref, o_ref):
    idx = idx_ref[...]
    start = jnp.zeros(idx.shape, I32)
    per = jnp.ones(idx.shape, I32)
    for e in range(start_ref.shape[0]):
        sel = idx == e
        start = jnp.where(sel, start_ref[e], start)
        per = jnp.where(sel, rps_ref[e], per)
    rank = rank_ref[...]
    q = jnp.floor(rank.astype(F32) / per.astype(F32)).astype(I32)
    rem = rank - q * per
    q = jnp.where(rem < 0, q - 1, jnp.where(rem >= per, q + 1, q))
    rem = rank - q * per
    o_ref[...] = (start + q) * SUPER + rem


def _dest(sb_start, rps, idx, rank):
    return pl.pallas_call(
        _dest_kernel,
        out_shape=jax.ShapeDtypeStruct(idx.shape, I32),
        grid_spec=pltpu.PrefetchScalarGridSpec(
            num_scalar_prefetch=2,
            grid=(1,),
            in_specs=[pl.BlockSpec(idx.shape, lambda i, *_: (0, 0)),
                      pl.BlockSpec(idx.shape, lambda i, *_: (0, 0))],
            out_specs=pl.BlockSpec(idx.shape, lambda i, *_: (0, 0))),
        compiler_params=_cparams(("arbitrary",)),
        name="dest",
    )(sb_start.astype(I32), rps.astype(I32), idx, rank)


def _plan(idx, rank, counts, n_super):
    n_exp = counts.shape[0]
    nsb = (counts + SUPER - 1) // SUPER
    rps = (counts + jnp.maximum(nsb, 1) - 1) // jnp.maximum(nsb, 1)
    rps = jnp.maximum(rps, 1).astype(I32)
    sb_end = jnp.cumsum(nsb)
    sb_start = sb_end - nsb
    dest = _dest(sb_start, rps, idx, rank)
    s_ids = jnp.arange(n_super, dtype=I32)
    nused = sb_end[-1]
    s_eff = jnp.minimum(s_ids, nused - 1)
    sbe = jnp.minimum(jnp.sum((sb_end[None, :] <= s_eff[:, None]).astype(I32), axis=1), n_exp - 1)
    hot = sbe[:, None] == jnp.arange(n_exp, dtype=I32)[None, :]
    look = lambda tab: jnp.sum(jnp.where(hot, tab.astype(I32)[None, :], 0), axis=1)
    s_rps, s_cnt, s_start = look(rps), look(counts), look(sb_start)
    in_sb = jnp.minimum(s_rps, s_cnt - (s_ids - s_start) * s_rps)
    per = SUPER // SUB
    left = in_sb[:, None] - jnp.arange(per, dtype=I32)[None, :] * SUB
    vrows = jnp.where((s_ids < nused)[:, None], jnp.clip(left, 0, SUB), 0)
    nsub = jnp.sum((vrows > 0).astype(I32), axis=1)
    zflag = (vrows < SUB).astype(I32).reshape(-1)
    return dest.reshape(-1).astype(I32), sbe.astype(I32), nsub, nused.reshape(1).astype(I32), zflag


def kernel(x, c, ada_w, ada_b, norm1_g, w_in, pool_w, pool_scale, w_pool_out, ssm_lam_re, ssm_lam_im,
           ssm_log_dt, ssm_b_re, ssm_b_im, ssm_c_re, ssm_c_im, ssm_d, w_glu, b_glu, w_out, norm2_g,
           w_router, b_router, w1, b1, w2, b2, final_ada_w, final_ada_b, final_norm_g):
    b, s, d = x.shape
    t = b * s
    depth = ada_w.shape[0]
    n_exp = w_router.shape[-1]
    pwid = w_pool_out.shape[1]
    n_super = (t * TOP_K) // SUPER + n_exp

    c8 = jnp.zeros((8, d), F32).at[:b].set(c.astype(F32))
    fmod = _ada(c8, final_ada_w, final_ada_b)[:b]
    sh_o, sc_o = jnp.split(fmod, 2, axis=-1)
    for l in range(depth):
        mod = _ada(c8, ada_w[l], ada_b[l])[:b]
        sh_m, sc_m, gt_m, sh_f, sc_f, gt_f = jnp.split(mod, 6, axis=-1)
        z = _inproj(x, norm1_g[l], sh_m, sc_m, w_in[l].astype(BF16))
        ops = _s5_weights(ssm_lam_re[l], ssm_lam_im[l], ssm_log_dt[l], ssm_b_re[l], ssm_b_im[l],
                          ssm_c_re[l], ssm_c_im[l])
        ys = _s5(z, pwid, *ops, ssm_d[l])
        x = _mixout(z, ys, x, gt_m, pool_w[l].astype(BF16), pool_scale[l], w_pool_out[l].astype(BF16),
                    w_glu[l].astype(BF16), b_glu[l], w_out[l].astype(BF16))
        hp, idx, wt, rank, cnt = _router(x, norm2_g[l], sh_f, sc_f, w_router[l], b_router[l])
        dest, sbe, nsub, nused, zflag = _plan(idx, rank, cnt[:, 0], n_super)
        xs = _dispatch(dest, zflag, hp, n_super * SUPER)
        ye = _experts(sbe, nsub, nused, xs, w1[l], b1[l], w2[l], b2[l], n_super)
        last = l == depth - 1
        x = _combine(dest, x, wt.T, gt_f, final_norm_g, sh_o, sc_o, ye, apply_norm=last)
    return x
```

```python
import functools
import math

import jax
import jax.numpy as jnp
from jax import lax
from jax.experimental import pallas as pl
from jax.experimental.pallas import tpu as pltpu

F32 = jnp.float32
BF16 = jnp.bfloat16
I32 = jnp.int32
U32 = jnp.uint32

RMS_EPS = 1e-6
POOL_WINDOWS = (2, 4, 8, 16)
SSM_GROUP_DIM = 16
LAMBDA_RE_MAX = -1e-4
TOP_K = 4
SWIGLU_ALPHA = 1.702
SWIGLU_LIMIT = 7.0

LANES = 128
V7X_VMEM_LIMIT = 56 * 1024 * 1024

CHUNK = 16
OCT = LANES // SSM_GROUP_DIM
SUB = 128
SUPER = 9 * SUB


def _cparams(sem, vmem=V7X_VMEM_LIMIT):
    return pltpu.CompilerParams(dimension_semantics=sem, vmem_limit_bytes=vmem)


def _const_spec(shape):
    nd = len(shape)
    return pl.BlockSpec(shape, lambda *_: (0,) * nd, pipeline_mode=pl.Buffered(1))


def _rms_mod(x, g, sh, sc):
    y = x * lax.rsqrt(jnp.mean(x * x, axis=-1, keepdims=True) + RMS_EPS) * g
    return y * (1.0 + sc) + sh


def _pair_width(d):
    return min(2 * LANES, d // 2)


def _pack_rows(h):
    d = h.shape[1]
    pair = _pair_width(d)
    hb = lax.bitcast_convert_type(h.astype(BF16).astype(F32), U32)
    words = []
    for n in range(d // (2 * pair)):
        lo = hb[:, 2 * pair * n:2 * pair * n + pair]
        hi = hb[:, 2 * pair * n + pair:2 * pair * (n + 1)]
        words.append((lo >> 16) | (hi & jnp.uint32(0xFFFF0000)))
    return jnp.concatenate(words, axis=1)


def _unpack_rows(w):
    d = 2 * w.shape[1]
    pair = _pair_width(d)
    cols = []
    for n in range(d // (2 * pair)):
        wn = w[:, pair * n:pair * (n + 1)]
        cols.append(lax.bitcast_convert_type(wn << 16, F32))
        cols.append(lax.bitcast_convert_type(wn & jnp.uint32(0xFFFF0000), F32))
    return jnp.concatenate(cols, axis=1)


def _ada_kernel(c_ref, w_ref, b_ref, o_ref):
    c = c_ref[...]
    ca = c * jax.nn.sigmoid(c)
    o_ref[...] = jnp.dot(ca.astype(BF16), w_ref[...].astype(BF16),
                         preferred_element_type=F32) + b_ref[...]


def _ada(c8, w, b, tn=1024):
    d, n = w.shape
    tn = math.gcd(tn, n)
    return pl.pallas_call(
        _ada_kernel,
        out_shape=jax.ShapeDtypeStruct((c8.shape[0], n), F32),
        grid=(n // tn,),
        in_specs=[pl.BlockSpec(c8.shape, lambda j: (0, 0)),
                  pl.BlockSpec((d, tn), lambda j: (0, j)),
                  pl.BlockSpec((1, tn), lambda j: (0, j))],
        out_specs=pl.BlockSpec((c8.shape[0], tn), lambda j: (0, j)),
        compiler_params=_cparams(("arbitrary",)),
        name="ada",
    )(c8, w, b.reshape(1, n))


def _inproj_kernel(x_ref, g_ref, sh_ref, sc_ref, w_ref, z_ref, *, ncol):
    h = _rms_mod(x_ref[0], g_ref[...], sh_ref[0], sc_ref[0]).astype(BF16)
    n = w_ref.shape[1]
    for c in range(n // ncol):
        z_ref[0, :, c * ncol:(c + 1) * ncol] = jnp.dot(
            h, w_ref[:, c * ncol:(c + 1) * ncol], preferred_element_type=F32).astype(BF16)


def _inproj(x, g, sh, sc, w, tm=512, ncol=1024):
    b, s, d = x.shape
    n = w.shape[1]
    ncol = math.gcd(ncol, n)
    return pl.pallas_call(
        functools.partial(_inproj_kernel, ncol=ncol),
        out_shape=jax.ShapeDtypeStruct((b, s, n), BF16),
        grid=(b, s // tm),
        in_specs=[pl.BlockSpec((1, tm, d), lambda bi, i: (bi, i, 0)),
                  _const_spec((1, d)),
                  pl.BlockSpec((1, 1, d), lambda bi, i: (bi, 0, 0)),
                  pl.BlockSpec((1, 1, d), lambda bi, i: (bi, 0, 0)),
                  _const_spec((d, n))],
        out_specs=pl.BlockSpec((1, tm, n), lambda bi, i: (bi, i, 0)),
        compiler_params=_cparams(("arbitrary", "arbitrary")),
        name="inproj",
    )(x, g.reshape(1, d), sh.reshape(b, 1, d), sc.reshape(b, 1, d), w)


def _s5_weights(lam_re, lam_im, log_dt, b_re, b_im, c_re, c_im):
    hp = lax.Precision.HIGHEST
    g_all, p = lam_re.shape
    h = b_re.shape[-1]
    no = g_all // OCT
    dt = jnp.exp(log_dt.astype(F32))[:, None]
    lre = jnp.minimum(lam_re.astype(F32), LAMBDA_RE_MAX)
    lim = lam_im.astype(F32)
    mag = jnp.exp(lre * dt)
    ang = lim * dt
    ab_re = mag * jnp.cos(ang)
    ab_im = mag * jnp.sin(ang)
    den = lre * lre + lim * lim
    nr = ab_re - 1.0
    ni = ab_im
    f_re = (nr * lre + ni * lim) / den
    f_im = (ni * lre - nr * lim) / den
    br_, bi_ = b_re.astype(F32), b_im.astype(F32)
    bb_re = f_re[..., None] * br_ - f_im[..., None] * bi_
    bb_im = f_re[..., None] * bi_ + f_im[..., None] * br_
    cr, ci = c_re.astype(F32), c_im.astype(F32)
    j = jnp.arange(CHUNK + 1, dtype=F32)[:, None, None]
    pw_mag = jnp.exp(j * (lre * dt)[None])
    pw_re = pw_mag * jnp.cos(j * ang[None])
    pw_im = pw_mag * jnp.sin(j * ang[None])
    pw_re, pw_im, bb_re, bb_im = lax.optimization_barrier((pw_re, pw_im, bb_re, bb_im))
    cp_re = cr[None] * pw_re[:, :, None, :] - ci[None] * pw_im[:, :, None, :]
    cp_im = cr[None] * pw_im[:, :, None, :] + ci[None] * pw_re[:, :, None, :]
    assert 2 * p == LANES and h * OCT == LANES
    tk = jnp.arange(CHUNK)

    bt_re = jnp.swapaxes(bb_re, 1, 2)
    bt_im = jnp.swapaxes(bb_im, 1, 2)
    cg_re = jnp.swapaxes(cp_re[:CHUNK], 0, 1).reshape(g_all, CHUNK * h, p)
    cg_im = jnp.swapaxes(cp_im[:CHUNK], 0, 1).reshape(g_all, CHUNK * h, p)
    klag = (jnp.einsum('gip,gap->gia', bt_re, cg_re, precision=hp)
            - jnp.einsum('gip,gap->gia', bt_im, cg_im, precision=hp))
    klag = jnp.transpose(klag.reshape(g_all, h, CHUNK, h), (2, 0, 1, 3))
    lag_t = klag.reshape(CHUNK, g_all * h, h)
    bt_re = bt_re[None]
    bt_im = bt_im[None]
    pwr = pw_re[CHUNK - 1 - tk][:, :, None, :]
    pwi = pw_im[CHUNK - 1 - tk][:, :, None, :]
    slabs = jnp.stack([pwr * bt_re - pwi * bt_im, pwr * bt_im + pwi * bt_re,
                       cp_re[1:], -cp_im[1:]]).reshape(4, CHUNK, g_all * h, p)
    are = pw_re[CHUNK].reshape(no, OCT, 1, p)
    aim = pw_im[CHUNK].reshape(no, OCT, 1, p)
    a1 = jnp.concatenate([are, are], axis=2).reshape(no, 1, OCT * LANES)
    a2 = jnp.concatenate([-aim, aim], axis=2).reshape(no, 1, OCT * LANES)
    return lag_t, slabs, a1, a2


def _gelu_tanh(x):
    return 0.5 * x * (1.0 + jnp.tanh(math.sqrt(2.0 / math.pi) * (x + 0.044715 * x * x * x)))


def _s5_kernel(u_ref, lag_ref, slab_ref, a1_ref, a2_ref, d_ref, y_ref,
               uscr, sscr, pscr, yscr, state, t_op, b_op, c_op):
    nb, tm, _ = u_ref.shape
    nc = tm // CHUNK
    nq = a1_ref.shape[-1] // LANES

    @pl.when(pl.program_id(1) == 0)
    def _():
        state[...] = jnp.zeros_like(state)
        rowg = lax.broadcasted_iota(I32, (LANES, LANES), 0) // SSM_GROUP_DIM
        colg = lax.broadcasted_iota(I32, (LANES, LANES), 1) // SSM_GROUP_DIM
        zero = jnp.zeros((LANES, LANES), BF16)
        rep = (lax.broadcasted_iota(I32, (SSM_GROUP_DIM, LANES), 1) % SSM_GROUP_DIM
               == lax.broadcasted_iota(I32, (SSM_GROUP_DIM, LANES), 0)).astype(BF16)
        lag = []
        for j in range(CHUNK):
            wide = jnp.dot(lag_ref[j].astype(BF16), rep, preferred_element_type=F32).astype(BF16)
            lag.append(jnp.where(rowg == colg, wide, zero))
        for k in range(CHUNK):
            in_k = jnp.concatenate([slab_ref[0, k], slab_ref[1, k]], axis=1).astype(BF16)
            out_k = jnp.concatenate([slab_ref[2, k], slab_ref[3, k]], axis=1).T.astype(BF16)
            for q in range(nq):
                b_op[k * LANES:(k + 1) * LANES, q * LANES:(q + 1) * LANES] = jnp.where(rowg == q, in_k, zero)
                c_op[q * LANES:(q + 1) * LANES, k * LANES:(k + 1) * LANES] = jnp.where(colg == q, out_k, zero)
            for t in range(CHUNK):
                t_op[k * LANES:(k + 1) * LANES, t * LANES:(t + 1) * LANES] = lag[t - k] if t >= k else zero

    for bi in range(nb):
        uscr[bi * tm:(bi + 1) * tm, :] = u_ref[bi].astype(F32)
    xk = jnp.concatenate(
        [uscr[pl.ds(k, nb * nc, stride=CHUNK), :] for k in range(CHUNK)], axis=1).astype(BF16)
    rr = nb * nc
    sc_all = jnp.dot(xk, b_op[...], preferred_element_type=F32)
    for q in range(nq):
        sscr[q * rr:(q + 1) * rr, :] = sc_all[:, q * LANES:(q + 1) * LANES]

    a1 = [a1_ref[0, :, q * LANES:(q + 1) * LANES] for q in range(nq)]
    a2 = [a2_ref[0, :, q * LANES:(q + 1) * LANES] for q in range(nq)]
    st = [state[:, q * LANES:(q + 1) * LANES] for q in range(nq)]
    for c in range(nc):
        for q in range(nq):
            pscr[pl.ds(q * rr + c, nb, stride=nc), :] = st[q]
            sc = sscr[pl.ds(q * rr + c, nb, stride=nc), :]
            st[q] = a1[q] * st[q] + a2[q] * pltpu.roll(st[q], LANES // 2, axis=1) + sc
    for q in range(nq):
        state[:, q * LANES:(q + 1) * LANES] = st[q]

    xprev = jnp.concatenate([pscr[q * rr:(q + 1) * rr, :] for q in range(nq)], axis=1)
    y = (jnp.dot(xk, t_op[...], preferred_element_type=F32)
         + jnp.dot(xprev.astype(BF16), c_op[...], preferred_element_type=F32))
    for t in range(CHUNK):
        yscr[pl.ds(t, nb * nc, stride=CHUNK), :] = y[:, t * LANES:(t + 1) * LANES]
    out = _gelu_tanh(yscr[...] + d_ref[...] * uscr[...])
    for bi in range(nb):
        y_ref[bi] = out[bi * tm:(bi + 1) * tm, :].astype(BF16)


def _s5(z, col0, lag_t, slabs, a1, a2, d_skip, tm=1024):
    nb, s, _ = z.shape
    no = a1.shape[0]
    w = no * LANES
    ns = a1.shape[-1]
    tm = min(tm, s)
    rows = nb * tm
    cb0 = col0 // LANES
    lag_spec = pl.BlockSpec((CHUNK, LANES, lag_t.shape[-1]), lambda o, i: (0, o, 0))
    slab_spec = pl.BlockSpec((4, CHUNK, LANES, slabs.shape[-1]), lambda o, i: (0, 0, o, 0))
    return pl.pallas_call(
        _s5_kernel,
        out_shape=jax.ShapeDtypeStruct((nb, s, w), BF16),
        grid=(no, s // tm),
        in_specs=[pl.BlockSpec((nb, tm, LANES), lambda o, i: (0, i, cb0 + o)),
                  lag_spec, slab_spec,
                  pl.BlockSpec((1, 1, ns), lambda o, i: (o, 0, 0)),
                  pl.BlockSpec((1, 1, ns), lambda o, i: (o, 0, 0)),
                  pl.BlockSpec((1, LANES), lambda o, i: (0, o))],
        out_specs=pl.BlockSpec((nb, tm, LANES), lambda o, i: (0, i, o)),
        scratch_shapes=[pltpu.VMEM((rows, LANES), F32),
                        pltpu.VMEM((rows // CHUNK * (ns // LANES), LANES), F32),
                        pltpu.VMEM((rows // CHUNK * (ns // LANES), LANES), F32),
                        pltpu.VMEM((rows, LANES), F32),
                        pltpu.VMEM((nb, ns), F32),
                        pltpu.VMEM((CHUNK * LANES, CHUNK * LANES), BF16),
                        pltpu.VMEM((CHUNK * LANES, ns), BF16),
                        pltpu.VMEM((ns, CHUNK * LANES), BF16)],
        compiler_params=_cparams(("arbitrary", "arbitrary")),
        name="s5",
    )(z, lag_t, slabs, a1, a2, d_skip.reshape(1, w))


def _mixout_kernel(up_ref, halo_ref, ys_ref, gp_ref, gs_ref, x_ref, gt_ref,
                   pw_ref, ps_ref, wpo_ref, wglu_ref, bglu_ref, wout_ref, o_ref, escr):
    tm = up_ref.shape[1]
    i = pl.program_id(1)
    gw = pw_ref.shape[1]
    hal = halo_ref.shape[1]
    d = o_ref.shape[2]
    escr[0:hal, :] = jnp.where(i > 0, halo_ref[0].astype(F32), 0.0)
    escr[hal:hal + tm, :] = up_ref[0].astype(F32)
    pos = (i * tm + lax.broadcasted_iota(I32, (tm, gw), 0) + 1).astype(F32)
    mixed = []
    for g, win in enumerate(POOL_WINDOWS):
        cols = slice(g * gw, (g + 1) * gw)
        cur = escr[hal:hal + tm, cols]
        acc = cur
        for j in range(1, win):
            acc = acc + escr[hal - j:hal - j + tm, cols]
        pooled = acc / jnp.minimum(pos, float(win)) - cur
        mixed.append(jnp.dot(pooled.astype(BF16), pw_ref[g], preferred_element_type=F32))
    mixed = jnp.concatenate(mixed, axis=1) * ps_ref[...]
    y_pool = jnp.dot(mixed.astype(BF16), wpo_ref[...], preferred_element_type=F32)
    glu = jnp.dot(ys_ref[0], wglu_ref[...], preferred_element_type=F32) + bglu_ref[...]
    y_ssm = glu[:, :d] * jax.nn.sigmoid(glu[:, d:])
    merged = (jax.nn.sigmoid(gp_ref[0].astype(F32)) * y_pool
              + jax.nn.sigmoid(gs_ref[0].astype(F32)) * y_ssm)
    mix = jnp.dot(merged.astype(BF16), wout_ref[...], preferred_element_type=F32)
    o_ref[0] = x_ref[0] + gt_ref[0] * mix


def _mixout(z, ys, x, gt, pool_w, pool_scale, w_pool_out, w_glu, b_glu, w_out, tm=256):
    b, s, d = x.shape
    pwid = w_pool_out.shape[0]
    hal = max(POOL_WINDOWS)
    tpb = tm // hal
    gcol = (pwid + ys.shape[2]) // d
    return pl.pallas_call(
        _mixout_kernel,
        out_shape=jax.ShapeDtypeStruct((b, s, d), F32),
        grid=(b, s // tm),
        in_specs=[pl.BlockSpec((1, tm, pwid), lambda bi, i: (bi, i, 0)),
                  pl.BlockSpec((1, hal, pwid), lambda bi, i: (bi, jnp.maximum(i * tpb - 1, 0), 0)),
                  pl.BlockSpec((1, tm, ys.shape[2]), lambda bi, i: (bi, i, 0)),
                  pl.BlockSpec((1, tm, d), lambda bi, i: (bi, i, gcol)),
                  pl.BlockSpec((1, tm, d), lambda bi, i: (bi, i, gcol + 1)),
                  pl.BlockSpec((1, tm, d), lambda bi, i: (bi, i, 0)),
                  pl.BlockSpec((1, 1, d), lambda bi, i: (bi, 0, 0)),
                  _const_spec(pool_w.shape),
                  _const_spec((1, pwid)),
                  _const_spec(w_pool_out.shape),
                  _const_spec(w_glu.shape),
                  _const_spec((1, w_glu.shape[1])),
                  _const_spec(w_out.shape)],
        out_specs=pl.BlockSpec((1, tm, d), lambda bi, i: (bi, i, 0)),
        scratch_shapes=[pltpu.VMEM((hal + tm, pwid), F32)],
        compiler_params=_cparams(("arbitrary", "arbitrary")),
        name="mixout",
    )(z, z, ys, z, z, x, gt.reshape(b, 1, d), pool_w, pool_scale.reshape(1, pwid),
      w_pool_out, w_glu, b_glu.reshape(1, -1), w_out)


def _router_kernel(x_ref, g_ref, sh_ref, sc_ref, wr_ref, br_ref,
                   hp_ref, idx_ref, wt_ref, rank_ref, cnt_ref, carry, *, n_exp):
    tm = x_ref.shape[1]
    first = (pl.program_id(0) == 0) & (pl.program_id(1) == 0)

    @pl.when(first)
    def _():
        carry[...] = jnp.zeros_like(carry)

    h2 = _rms_mod(x_ref[0], g_ref[...], sh_ref[0], sc_ref[0])
    hp_ref[...] = _pack_rows(h2)
    h_hi = h2.astype(BF16)
    h_lo = (h2 - h_hi.astype(F32)).astype(BF16)
    logits = (jnp.dot(h_hi, wr_ref[0], preferred_element_type=F32)
              + jnp.dot(h_hi, wr_ref[1], preferred_element_type=F32)
              + jnp.dot(h_lo, wr_ref[0], preferred_element_type=F32)) + br_ref[...]
    lt = logits.T[:n_exp, :]
    eid = lax.broadcasted_iota(I32, lt.shape, 0).astype(F32)
    vals, idxs, hots = [], [], []
    v = lt
    for _ in range(TOP_K):
        m = jnp.max(v, axis=0, keepdims=True)
        sel = jnp.min(jnp.where(v == m, eid, float(n_exp)), axis=0, keepdims=True)
        hot = eid == sel
        vals.append(m)
        idxs.append(sel)
        hots.append(hot)
        v = jnp.where(hot, -jnp.inf, v)
    ex = [jnp.exp(m - vals[0]) for m in vals]
    tot = ex[0] + ex[1] + ex[2] + ex[3]
    msum = sum(h.astype(F32) for h in hots)
    tri = (lax.broadcasted_iota(I32, (tm, tm), 0) < lax.broadcasted_iota(I32, (tm, tm), 1))
    before = jnp.dot(msum.astype(BF16), tri.astype(BF16), preferred_element_type=F32) + carry[:, 0:1]
    for k in range(TOP_K):
        idx_ref[k:k + 1, :] = idxs[k].astype(I32)
        wt_ref[k:k + 1, :] = ex[k] / tot
        rank_ref[k:k + 1, :] = jnp.sum(jnp.where(hots[k], before, 0.0), axis=0,
                                       keepdims=True).astype(I32)
    carry[...] = carry[...] + jnp.sum(msum, axis=1, keepdims=True)
    cnt_ref[...] = carry[...].astype(I32)


def _router(x1, g, sh, sc, w_router, b_router, tm=512):
    b, s, d = x1.shape
    t = b * s
    e = w_router.shape[1]
    wd = d // 2
    wr = jnp.dot(w_router.astype(F32), jnp.eye(e, LANES, dtype=F32), precision=lax.Precision.HIGHEST)
    wr_hi = wr.astype(BF16)
    wr = jnp.stack([wr_hi, (wr - wr_hi.astype(F32)).astype(BF16)])
    br = jnp.full((1, LANES), -1e30, F32).at[0, :e].set(b_router.astype(F32))
    spb = s // tm
    return pl.pallas_call(
        functools.partial(_router_kernel, n_exp=e),
        out_shape=[jax.ShapeDtypeStruct((t, wd), U32),
                   jax.ShapeDtypeStruct((TOP_K, t), I32),
                   jax.ShapeDtypeStruct((TOP_K, t), F32),
                   jax.ShapeDtypeStruct((TOP_K, t), I32),
                   jax.ShapeDtypeStruct((e, LANES), I32)],
        grid=(b, spb),
        in_specs=[pl.BlockSpec((1, tm, d), lambda bi, i: (bi, i, 0)),
                  _const_spec((1, d)),
                  pl.BlockSpec((1, 1, d), lambda bi, i: (bi, 0, 0)),
                  pl.BlockSpec((1, 1, d), lambda bi, i: (bi, 0, 0)),
                  _const_spec((2, d, LANES)),
                  _const_spec((1, LANES))],
        out_specs=[pl.BlockSpec((tm, wd), lambda bi, i: (bi * spb + i, 0)),
                   pl.BlockSpec((TOP_K, tm), lambda bi, i: (0, bi * spb + i)),
                   pl.BlockSpec((TOP_K, tm), lambda bi, i: (0, bi * spb + i)),
                   pl.BlockSpec((TOP_K, tm), lambda bi, i: (0, bi * spb + i)),
                   pl.BlockSpec((e, LANES), lambda bi, i: (0, 0))],
        scratch_shapes=[pltpu.VMEM((e, LANES), F32)],
        compiler_params=_cparams(("arbitrary", "arbitrary")),
        name="router",
    )(x1, g.reshape(1, d), sh.reshape(b, 1, d), sc.reshape(b, 1, d), wr, br)


def _dispatch_kernel(dest_ref, zflag_ref, hp_ref, xs_ref, zbuf, sem, zsem, *, n_tok):
    tm = hp_ref.shape[0]
    i = pl.program_id(0)

    @pl.when(i == 0)
    def _():
        zbuf[...] = jnp.zeros_like(zbuf)

        def zcopy(q):
            r0 = pl.multiple_of(q * SUB, SUB)
            return pltpu.make_async_copy(zbuf, xs_ref.at[pl.ds(r0, SUB)], zsem)

        def zstart(q, carry):
            pl.when(zflag_ref[q] != 0)(lambda: zcopy(q).start())
            return carry

        def zwait(q, carry):
            pl.when(zflag_ref[q] != 0)(lambda: zcopy(q).wait())
            return carry

        lax.fori_loop(0, zflag_ref.shape[0], zstart, 0)
        lax.fori_loop(0, zflag_ref.shape[0], zwait, 0)

    def issue(t, carry):
        for k in range(TOP_K):
            dst = dest_ref[k * n_tok + i * tm + t]
            pltpu.make_async_copy(hp_ref.at[pl.ds(t, 1)], xs_ref.at[pl.ds(dst, 1)],
                                  sem).start(priority=k % 2)
        return carry

    lax.fori_loop(0, tm, issue, 0, unroll=8)
    for k in range(TOP_K):
        pltpu.make_async_copy(hp_ref, xs_ref.at[pl.ds(0, tm)], sem).wait()


def _dispatch(dest_flat, zflag, hp, n_rows, tm=1024):
    t, wd = hp.shape
    tm = min(tm, t)
    return pl.pallas_call(
        functools.partial(_dispatch_kernel, n_tok=t),
        out_shape=jax.ShapeDtypeStruct((n_rows, wd), U32),
        grid_spec=pltpu.PrefetchScalarGridSpec(
            num_scalar_prefetch=2,
            grid=(t // tm,),
            in_specs=[pl.BlockSpec((tm, wd), lambda i, *_: (i, 0))],
            out_specs=pl.BlockSpec(memory_space=pl.ANY),
            scratch_shapes=[pltpu.VMEM((SUB, wd), U32),
                            pltpu.SemaphoreType.DMA,
                            pltpu.SemaphoreType.DMA]),
        compiler_params=_cparams(("arbitrary",)),
        name="dispatch",
    )(dest_flat, zflag, hp)


def _experts_kernel(sbe_ref, nsub_ref, nused_ref, x_ref, b1_ref, b2_ref, w1_hbm, w2_hbm, o_ref,
                    xb, actb, yb, wt, sem):
    s = pl.program_id(0)
    n = nsub_ref[s]
    nused = nused_ref[0]
    nj, _, tf = actb.shape
    nn, _, pair = yb.shape
    d = xb.shape[1]
    f = nj * tf
    nslot = wt.shape[0]
    ntile = nj + nn
    per = SUPER // SUB

    def a_copies(e, j, slot):
        return [pltpu.make_async_copy(
            w1_hbm.at[e, :, pl.ds(pl.multiple_of(part * f + j * tf, tf), tf)],
            wt.at[slot, part, 0:d, 0:tf], sem.at[slot]) for part in range(2)]

    def b_copies(e, c, slot):
        return [pltpu.make_async_copy(
            w2_hbm.at[e, :, pl.ds(pl.multiple_of((2 * c + part) * pair, pair), pair)],
            wt.at[slot, part, 0:f, 0:pair], sem.at[slot]) for part in range(2)]

    def slot_of(sb, q):
        return lax.rem(sb * ntile + q, nslot)

    def start_tile(sb, q):
        e = sbe_ref[sb]
        slot = slot_of(sb, q)

        @pl.when(q < nj)
        def _():
            for cp in a_copies(e, q, slot):
                cp.start()

        @pl.when(q >= nj)
        def _():
            for cp in b_copies(e, q - nj, slot):
                cp.start()

    def prefetch(q):
        qq = q + nslot - 1

        @pl.when(qq < ntile)
        def _():
            start_tile(s, qq)

        @pl.when((qq >= ntile) & (s + 1 < nused))
        def _():
            start_tile(s + 1, qq - ntile)

    @pl.when((s == 0) & (nused > 0))
    def _():
        for q in range(nslot - 1):
            start_tile(0, jnp.int32(q))

    def run(v):
        m = v * SUB
        e = sbe_ref[s]
        for r in range(v):
            rows = slice(r * SUB, (r + 1) * SUB)
            xb[rows, :] = _unpack_rows(x_ref[rows, :]).astype(BF16)

        def a_body(j, carry):
            slot = slot_of(s, j)
            for cp in a_copies(e, j, slot):
                cp.wait()
            prefetch(j)
            x = xb[0:m, :]
            g = jnp.dot(x, wt[slot, 0, 0:d, 0:tf].astype(BF16), preferred_element_type=F32) + b1_ref[0, j]
            l = jnp.dot(x, wt[slot, 1, 0:d, 0:tf].astype(BF16), preferred_element_type=F32) + b1_ref[0, nj + j]
            xg = jnp.minimum(g, SWIGLU_LIMIT)
            xl = jnp.clip(l, -SWIGLU_LIMIT, SWIGLU_LIMIT)
            act = xg * jax.nn.sigmoid(SWIGLU_ALPHA * xg) * (xl + 1.0)
            actb[j, 0:m, :] = act.astype(BF16)
            return carry

        lax.fori_loop(0, nj, a_body, 0)

        def b_body(c, carry):
            slot = slot_of(s, nj + c)
            for cp in b_copies(e, c, slot):
                cp.wait()
            prefetch(nj + c)
            halves = []
            for part in range(2):
                w = wt[slot, part, 0:f, 0:pair].astype(BF16)
                y = b2_ref[0, 2 * c + part] + jnp.dot(actb[0, 0:m, :], w[0:tf, :],
                                                      preferred_element_type=F32)
                for j in range(1, nj):
                    y = y + jnp.dot(actb[j, 0:m, :], w[j * tf:(j + 1) * tf, :],
                                    preferred_element_type=F32)
                halves.append(y)
            yb[c, 0:m, :] = _pack_rows(jnp.concatenate(halves, axis=1))
            return carry

        lax.fori_loop(0, nn, b_body, 0)

        for c in range(nn):
            o_ref[0:m, c * pair:(c + 1) * pair] = yb[c, 0:m, :]
        if m < SUPER:
            o_ref[m:SUPER, :] = jnp.zeros((SUPER - m, o_ref.shape[1]), U32)

    for v in range(1, per + 1):
        pl.when((s < nused) & (n == v))(functools.partial(run, v))

    @pl.when(s >= nused)
    def _():
        o_ref[...] = jnp.zeros(o_ref.shape, U32)


def _experts(sbe, nsub, nused, xs, w1, b1, w2, b2, n_super, tf=256, nslot=3):
    n_exp, d, f2 = w1.shape
    f = f2 // 2
    tf = min(tf, f)
    pair = _pair_width(d)
    nj = f // tf
    nn = d // (2 * pair)
    wd = xs.shape[1]

    def x_map(s, sbe, nsub, nused):
        return (jnp.minimum(s, nused[0] - 1), 0)

    return pl.pallas_call(
        _experts_kernel,
        out_shape=jax.ShapeDtypeStruct((n_super * SUPER, wd), U32),
        grid_spec=pltpu.PrefetchScalarGridSpec(
            num_scalar_prefetch=3,
            grid=(n_super,),
            in_specs=[pl.BlockSpec((SUPER, wd), x_map),
                      pl.BlockSpec((1, 2 * nj, 1, tf), lambda s, sbe, nsub, nu: (sbe[s], 0, 0, 0)),
                      pl.BlockSpec((1, 2 * nn, 1, pair), lambda s, sbe, nsub, nu: (sbe[s], 0, 0, 0)),
                      pl.BlockSpec(memory_space=pl.ANY),
                      pl.BlockSpec(memory_space=pl.ANY)],
            out_specs=pl.BlockSpec((SUPER, wd), lambda s, *_: (s, 0)),
            scratch_shapes=[pltpu.VMEM((SUPER, d), BF16),
                            pltpu.VMEM((nj, SUPER, tf), BF16),
                            pltpu.VMEM((nn, SUPER, pair), U32),
                            pltpu.VMEM((nslot, 2, max(d, f), max(tf, pair)), F32),
                            pltpu.SemaphoreType.DMA((nslot,))]),
        compiler_params=_cparams(("arbitrary",)),
        name="experts",
    )(sbe, nsub, nused, xs, b1.reshape(n_exp, 2 * nj, 1, tf), b2.reshape(n_exp, 2 * nn, 1, pair), w1, w2)


def _combine_kernel(dest_ref, x_ref, wt_ref, gt_ref, g_ref, sh_ref, sc_ref, ys_ref, o_ref,
                    gbuf, sem, *, n_tok, tiles_per_batch, apply_norm):
    tm = x_ref.shape[1]
    i = pl.program_id(0) * tiles_per_batch + pl.program_id(1)
    n_tiles = pl.num_programs(0) * tiles_per_batch
    slot = lax.rem(i, 2)

    def gather(tile, buf):
        def issue(t, carry):
            for k in range(TOP_K):
                src = dest_ref[k * n_tok + tile * tm + t]
                pltpu.make_async_copy(ys_ref.at[pl.ds(src, 1)], gbuf.at[buf, k, pl.ds(t, 1)],
                                      sem.at[buf]).start(priority=k % 2)
            return carry

        lax.fori_loop(0, tm, issue, 0, unroll=8)

    @pl.when(i == 0)
    def _():
        gather(i, slot)

    @pl.when(i + 1 < n_tiles)
    def _():
        gather(i + 1, 1 - slot)

    for k in range(TOP_K):
        pltpu.make_async_copy(ys_ref.at[pl.ds(0, tm)], gbuf.at[slot, k], sem.at[slot]).wait()
    wt = wt_ref[...]
    ffn = wt[:, 0:1] * _unpack_rows(gbuf[slot, 0])
    for k in range(1, TOP_K):
        ffn = ffn + wt[:, k:k + 1] * _unpack_rows(gbuf[slot, k])
    x2 = x_ref[0] + gt_ref[0] * ffn
    o_ref[0] = _rms_mod(x2, g_ref[...], sh_ref[0], sc_ref[0]) if apply_norm else x2


def _combine(dest_flat, x1, wt_t, gt, g, sh, sc, ys, apply_norm, tm=512):
    b, s, d = x1.shape
    tm = min(tm, s)
    t = b * s
    spb = s // tm
    wd = ys.shape[1]
    return pl.pallas_call(
        functools.partial(_combine_kernel, n_tok=t, tiles_per_batch=spb, apply_norm=apply_norm),
        out_shape=jax.ShapeDtypeStruct((b, s, d), F32),
        grid_spec=pltpu.PrefetchScalarGridSpec(
            num_scalar_prefetch=1,
            grid=(b, spb),
            in_specs=[pl.BlockSpec((1, tm, d), lambda bi, i, *_: (bi, i, 0)),
                      pl.BlockSpec((tm, TOP_K), lambda bi, i, *_: (bi * spb + i, 0)),
                      pl.BlockSpec((1, 1, d), lambda bi, i, *_: (bi, 0, 0)),
                      pl.BlockSpec((1, d), lambda bi, i, *_: (0, 0)),
                      pl.BlockSpec((1, 1, d), lambda bi, i, *_: (bi, 0, 0)),
                      pl.BlockSpec((1, 1, d), lambda bi, i, *_: (bi, 0, 0)),
                      pl.BlockSpec(memory_space=pl.ANY)],
            out_specs=pl.BlockSpec((1, tm, d), lambda bi, i, *_: (bi, i, 0)),
            scratch_shapes=[pltpu.VMEM((2, TOP_K, tm, wd), U32),
                            pltpu.SemaphoreType.DMA((2,))]),
        compiler_params=_cparams(("arbitrary", "arbitrary")),
        name="combine",
    )(dest_flat, x1, wt_t, gt.reshape(b, 1, d), g.reshape(1, d), sh.reshape(b, 1, d),
      sc.reshape(b, 1, d), ys)


def _dest_kernel(start_ref, rps_ref, idx_ref, rank_ref, o_ref):
    idx = idx_ref[...]
    start = jnp.zeros(idx.shape, I32)
    per = jnp.ones(idx.shape, I32)
    for e in range(start_ref.shape[0]):
        sel = idx == e
        start = jnp.where(sel, start_ref[e], start)
        per = jnp.where(sel, rps_ref[e], per)
    rank = rank_ref[...]
    q = jnp.floor(rank.astype(F32) / per.astype(F32)).astype(I32)
    rem = rank - q * per
    q = jnp.where(rem < 0, q - 1, jnp.where(rem >= per, q + 1, q))
    rem = rank - q * per
    o_ref[...] = (start + q) * SUPER + rem


def _dest(sb_start, rps, idx, rank):
    return pl.pallas_call(
        _dest_kernel,
        out_shape=jax.ShapeDtypeStruct(idx.shape, I32),
        grid_spec=pltpu.PrefetchScalarGridSpec(
            num_scalar_prefetch=2,
            grid=(1,),
            in_specs=[pl.BlockSpec(idx.shape, lambda i, *_: (0, 0)),
                      pl.BlockSpec(idx.shape, lambda i, *_: (0, 0))],
            out_specs=pl.BlockSpec(idx.shape, lambda i, *_: (0, 0))),
        compiler_params=_cparams(("arbitrary",)),
        name="dest",
    )(sb_start.astype(I32), rps.astype(I32), idx, rank)


def _plan(idx, rank, counts, n_super):
    n_exp = counts.shape[0]
    nsb = (counts + SUPER - 1) // SUPER
    rps = (counts + jnp.maximum(nsb, 1) - 1) // jnp.maximum(nsb, 1)
    rps = jnp.maximum(rps, 1).astype(I32)
    sb_end = jnp.cumsum(nsb)
    sb_start = sb_end - nsb
    dest = _dest(sb_start, rps, idx, rank)
    s_ids = jnp.arange(n_super, dtype=I32)
    nused = sb_end[-1]
    s_eff = jnp.minimum(s_ids, nused - 1)
    sbe = jnp.minimum(jnp.sum((sb_end[None, :] <= s_eff[:, None]).astype(I32), axis=1), n_exp - 1)
    hot = sbe[:, None] == jnp.arange(n_exp, dtype=I32)[None, :]
    look = lambda tab: jnp.sum(jnp.where(hot, tab.astype(I32)[None, :], 0), axis=1)
    s_rps, s_cnt, s_start = look(rps), look(counts), look(sb_start)
    in_sb = jnp.minimum(s_rps, s_cnt - (s_ids - s_start) * s_rps)
    per = SUPER // SUB
    left = in_sb[:, None] - jnp.arange(per, dtype=I32)[None, :] * SUB
    vrows = jnp.where((s_ids < nused)[:, None], jnp.clip(left, 0, SUB), 0)
    nsub = jnp.sum((vrows > 0).astype(I32), axis=1)
    zflag = (vrows < SUB).astype(I32).reshape(-1)
    return dest.reshape(-1).astype(I32), sbe.astype(I32), nsub, nused.reshape(1).astype(I32), zflag


def kernel(x, c, ada_w, ada_b, norm1_g, w_in, pool_w, pool_scale, w_pool_out, ssm_lam_re, ssm_lam_im,
           ssm_log_dt, ssm_b_re, ssm_b_im, ssm_c_re, ssm_c_im, ssm_d, w_glu, b_glu, w_out, norm2_g,
           w_router, b_router, w1, b1, w2, b2, final_ada_w, final_ada_b, final_norm_g):
    b, s, d = x.shape
    t = b * s
    depth = ada_w.shape[0]
    n_exp = w_router.shape[-1]
    pwid = w_pool_out.shape[1]
    n_super = (t * TOP_K) // SUPER + n_exp

    c8 = jnp.zeros((8, d), F32).at[:b].set(c.astype(F32))
    fmod = _ada(c8, final_ada_w, final_ada_b)[:b]
    sh_o, sc_o = jnp.split(fmod, 2, axis=-1)
    for l in range(depth):
        mod = _ada(c8, ada_w[l], ada_b[l])[:b]
        sh_m, sc_m, gt_m, sh_f, sc_f, gt_f = jnp.split(mod, 6, axis=-1)
        z = _inproj(x, norm1_g[l], sh_m, sc_m, w_in[l].astype(BF16))
        ops = _s5_weights(ssm_lam_re[l], ssm_lam_im[l], ssm_log_dt[l], ssm_b_re[l], ssm_b_im[l],
                          ssm_c_re[l], ssm_c_im[l])
        ys = _s5(z, pwid, *ops, ssm_d[l])
        x = _mixout(z, ys, x, gt_m, pool_w[l].astype(BF16), pool_scale[l], w_pool_out[l].astype(BF16),
                    w_glu[l].astype(BF16), b_glu[l], w_out[l].astype(BF16))
        hp, idx, wt, rank, cnt = _router(x, norm2_g[l], sh_f, sc_f, w_router[l], b_router[l])
        dest, sbe, nsub, nused, zflag = _plan(idx, rank, cnt[:, 0], n_super)
        xs = _dispatch(dest, zflag, hp, n_super * SUPER)
        ye = _experts(sbe, nsub, nused, xs, w1[l], b1[l], w2[l], b2[l], n_super)
        last = l == depth - 1
        x = _combine(dest, x, wt.T, gt_f, final_norm_g, sh_o, sc_o, ye, apply_norm=last)
    return x
```
